```python
import math
import jax, jax.numpy as jnp
from jax import lax
import numpy as np

D_MODEL = 2048
BATCH = 8
SEQ = 2048
DEPTH = 1

GLA_HEADS = 4
GLA_DK = D_MODEL // 2 // GLA_HEADS
GLA_DV = D_MODEL // GLA_HEADS
GLA_QK_W = GLA_HEADS * GLA_DK
GLA_V_W = GLA_HEADS * GLA_DV
GLA_LOWRANK = 16
GLA_GATE_NORM = 16.0
GLA_CHUNK = 64

DIL_PATTERNS = ((128, 1), (512, 4), (2048, 16))
DIL_SLOTS = 8
DIL_HEAD_DIM = 128
DIL_W = DIL_SLOTS * DIL_HEAD_DIM
DIL_STEPS = 128
DIL_BLOCK = 128
N_DIL_HEADS = len(DIL_PATTERNS) * DIL_SLOTS

REL_BUCKETS = 32
REL_MAX_DIST = 2048

D_FF = -(-8 * D_MODEL // (3 * 256)) * 256

RMS_EPS = 1e-6
NEG_INF = -1e30

D_IN = 2 * GLA_QK_W + 2 * GLA_V_W + GLA_LOWRANK + 3 * len(DIL_PATTERNS) * DIL_W + 2 * D_MODEL

kernel_name = "hybrid_gla_dilated_gated_merge"


def _rmsnorm(x, g):
    xf = x.astype(jnp.float32)
    y = xf * lax.rsqrt(jnp.mean(xf * xf, axis=-1, keepdims=True) + RMS_EPS)
    return (y * g.astype(jnp.float32)).astype(x.dtype)


def _split_points():
    sizes = [GLA_QK_W, GLA_QK_W, GLA_V_W, GLA_V_W, GLA_LOWRANK] + [DIL_W] * (3 * len(DIL_PATTERNS)) + [D_MODEL, D_MODEL]
    return [int(v) for v in np.cumsum(sizes)[:-1]]


def _t5_bucket(dist):
    max_exact = REL_BUCKETS // 2
    d = np.maximum(dist, 1).astype(np.float64)
    large = max_exact + (np.log(d / max_exact) / math.log(REL_MAX_DIST / max_exact)
                         * (REL_BUCKETS - max_exact)).astype(np.int64)
    large = np.minimum(large, REL_BUCKETS - 1)
    return np.where(dist < max_exact, dist, large).astype(np.int32)


def _gla(q, k, v, gk, g_out, norm_w):
    B, S, H, DK = q.shape
    DV = v.shape[-1]
    C = GLA_CHUNK
    nc = S // C
    scale = DK ** -0.5

    def chunks(t):
        return t.astype(jnp.float32).reshape(B, nc, C, H, t.shape[-1])

    qc, kc, vc, gc = chunks(q), chunks(k), chunks(v), chunks(gk)
    b = jnp.cumsum(gc, axis=2)
    b_last = b[:, :, -1:]
    q_e = qc * jnp.exp(b) * scale
    k_e = kc * jnp.exp(-b)
    k_to_end = kc * jnp.exp(b_last - b)
    causal = np.tril(np.ones((C, C), dtype=bool))
    att = jnp.einsum('bnihk,bnjhk->bnhij', q_e, k_e)
    att = jnp.where(causal, att, 0.0)
    o_intra = jnp.einsum('bnhij,bnjhv->bnihv', att, vc)

    def step(state, xs):
        q_n, k_n, v_n, decay_n = xs
        o = jnp.einsum('bihk,bhkv->bihv', q_n, state)
        state = decay_n[..., None] * state + jnp.einsum('bjhk,bjhv->bhkv', k_n, v_n)
        return state, o

    xs = (jnp.moveaxis(q_e, 1, 0), jnp.moveaxis(k_to_end, 1, 0),
          jnp.moveaxis(vc, 1, 0), jnp.moveaxis(jnp.exp(b_last[:, :, 0]), 1, 0))
    state0 = jnp.zeros((B, H, DK, DV), jnp.float32)
    _, o_inter = lax.scan(step, state0, xs)
    o = (o_intra + jnp.moveaxis(o_inter, 0, 1)).reshape(B, S, H, DV)
    o = _rmsnorm(o, norm_w) * jax.nn.silu(g_out.astype(jnp.float32))
    return o.reshape(B, S, H * DV).astype(v.dtype)


def _dilated_group(q, k, v, bias_table_g, dilation):
    B, S, H, E = q.shape
    L = S // dilation
    n_blk = -(-L // DIL_BLOCK)
    Lp = n_blk * DIL_BLOCK
    scale = E ** -0.5

    def to_classes(t):
        t = t.reshape(B, L, dilation, H, E).transpose(0, 2, 3, 1, 4)
        t = jnp.pad(t, ((0, 0), (0, 0), (0, 0), (0, Lp - L), (0, 0)))
        return t.reshape(B, dilation, H, n_blk, DIL_BLOCK, E)

    def with_prev(t):
        prev = jnp.pad(t[:, :, :, :-1], ((0, 0), (0, 0), (0, 0), (1, 0), (0, 0), (0, 0)))
        return jnp.concatenate([prev, t], axis=4)

    def from_classes(t):
        e = t.shape[-1]
        t = t.reshape(B, dilation, H, Lp, e)[:, :, :, :L]
        return t.transpose(0, 3, 1, 2, 4).reshape(B, S, H, e)

    qb = to_classes(q)
    kw = with_prev(to_classes(k))
    vw = with_prev(to_classes(v))

    a_idx = np.arange(DIL_BLOCK)[:, None]
    c_idx = np.arange(2 * DIL_BLOCK)[None, :]
    steps = DIL_BLOCK + a_idx - c_idx
    in_band = (steps >= 0) & (steps <= DIL_STEPS)
    bucket = _t5_bucket(np.clip(steps, 0, None) * dilation)
    bias = jnp.transpose(bias_table_g[bucket], (2, 0, 1)).astype(jnp.float32)
    has_prev = (np.arange(n_blk) > 0)[:, None, None]
    valid = in_band[None] & ((c_idx >= DIL_BLOCK)[None] | has_prev)

    s = jnp.einsum('brhnqe,brhnke->brhnqk', qb, kw).astype(jnp.float32) * scale
    s = jnp.where(valid, s + bias[:, None], NEG_INF)
    m = jnp.max(s, axis=-1, keepdims=True)
    p = jnp.exp(s - m)
    l = jnp.sum(p, axis=-1, keepdims=True)
    num = jnp.einsum('brhnqk,brhnke->brhnqe', p, vw.astype(jnp.float32))
    return from_classes(num), from_classes(m), from_classes(l)


def _dilated_attention(parts, rel_bias):
    nums, ms, ls = [], [], []
    for gi, (window, dilation) in enumerate(DIL_PATTERNS):
        q, k, v = parts[3 * gi], parts[3 * gi + 1], parts[3 * gi + 2]
        B, S = q.shape[:2]
        shp = (B, S, DIL_SLOTS, DIL_HEAD_DIM)
        num, m, l = _dilated_group(q.reshape(shp), k.reshape(shp), v.reshape(shp),
                                   rel_bias[:, gi * DIL_SLOTS:(gi + 1) * DIL_SLOTS], dilation)
        nums.append(num); ms.append(m); ls.append(l)
    m_all = functools_max(ms)
    w = [jnp.exp(m - m_all) for m in ms]
    numer = sum(wi * ni for wi, ni in zip(w, nums))
    denom = sum(wi * li for wi, li in zip(w, ls))
    o = numer / denom
    B, S = o.shape[:2]
    return o.reshape(B, S, DIL_W)


def functools_max(arrs):
    out = arrs[0]
    for a in arrs[1:]:
        out = jnp.maximum(out, a)
    return out


def setup_inputs(seed: int = 0) -> dict:
    key = jax.random.key(seed)
    ks = jax.random.split(key, 16)
    f32 = jnp.float32

    def normal(k, shape, scale):
        return jax.random.normal(k, shape, f32) * scale

    return {
        "x": normal(ks[0], (BATCH, SEQ, D_MODEL), 1.0),
        "attn_norm": 1.0 + normal(ks[1], (DEPTH, D_MODEL), 0.01),
        "w_in": normal(ks[2], (DEPTH, D_MODEL, D_IN), D_MODEL ** -0.5),
        "w_gk_up": normal(ks[3], (DEPTH, GLA_LOWRANK, GLA_QK_W), GLA_LOWRANK ** -0.5),
        "b_gk": normal(ks[4], (DEPTH, GLA_QK_W), 0.01),
        "gla_norm": 1.0 + normal(ks[5], (DEPTH, GLA_DV), 0.01),
        "b_gate": normal(ks[6], (DEPTH, 2 * D_MODEL), 0.01),
        "w_branch_gla": normal(ks[7], (DEPTH, GLA_V_W, D_MODEL), GLA_V_W ** -0.5),
        "w_branch_dil": normal(ks[8], (DEPTH, DIL_W, D_MODEL), DIL_W ** -0.5),
        "w_out": normal(ks[9], (DEPTH, D_MODEL, D_MODEL), D_MODEL ** -0.5),
        "ffn_norm": 1.0 + normal(ks[10], (DEPTH, D_MODEL), 0.01),
        "w_ffn_in": normal(ks[11], (DEPTH, D_MODEL, 2 * D_FF), D_MODEL ** -0.5),
        "w_ffn_out": normal(ks[12], (DEPTH, D_FF, D_MODEL), D_FF ** -0.5),
        "rel_bias": normal(ks[13], (REL_BUCKETS, N_DIL_HEADS), 0.5),
        "final_norm": 1.0 + normal(ks[14], (D_MODEL,), 0.01),
    }


def reference(x, attn_norm, w_in, w_gk_up, b_gk, gla_norm, b_gate, w_branch_gla, w_branch_dil,
              w_out, ffn_norm, w_ffn_in, w_ffn_out, rel_bias, final_norm):
    B, S = x.shape[:2]
    split_points = _split_points()
    n_dil = 3 * len(DIL_PATTERNS)
    for l in range(DEPTH):
        h = _rmsnorm(x, attn_norm[l])
        z = h @ w_in[l]
        parts = jnp.split(z, split_points, axis=-1)
        q_a, k_a, v_a, g_a, lr = parts[:5]
        dil_parts = parts[5:5 + n_dil]
        gate_a_logit, gate_b_logit = parts[5 + n_dil], parts[6 + n_dil]

        gk = jax.nn.log_sigmoid((lr @ w_gk_up[l] + b_gk[l]).astype(jnp.float32)) / GLA_GATE_NORM
        shp_k = (B, S, GLA_HEADS, GLA_DK)
        shp_v = (B, S, GLA_HEADS, GLA_DV)
        o_gla = _gla(q_a.reshape(shp_k), k_a.reshape(shp_k), v_a.reshape(shp_v),
                     gk.reshape(shp_k), g_a.reshape(shp_v), gla_norm[l])
        o_dil = _dilated_attention(dil_parts, rel_bias).astype(x.dtype)

        gate_a = jax.nn.sigmoid(gate_a_logit + b_gate[l, :D_MODEL])
        gate_b = jax.nn.sigmoid(gate_b_logit + b_gate[l, D_MODEL:])
        merged = gate_a * (o_gla @ w_branch_gla[l]) + gate_b * (o_dil @ w_branch_dil[l])
        x = x + merged @ w_out[l]

        hf = _rmsnorm(x, ffn_norm[l])
        gu = hf @ w_ffn_in[l]
        gate, up = gu[..., :D_FF], gu[..., D_FF:]
        x = x + (jax.nn.silu(gate) * up) @ w_ffn_out[l]
    return _rmsnorm(x, final_norm)
```

```python
import functools

import numpy as np
import jax
import jax.numpy as jnp
from jax import lax
from jax.experimental import pallas as pl
from jax.experimental.pallas import tpu as pltpu

F32 = jnp.float32
BF16 = jnp.bfloat16

D_MODEL = 2048
GLA_HEADS = 4
GLA_DK = 256
GLA_DV = 512
GLA_QK_W = GLA_HEADS * GLA_DK
GLA_V_W = GLA_HEADS * GLA_DV
GLA_LOWRANK = 16
GLA_GATE_NORM = 16.0
GLA_BLOCK = 128
DIL_DILATIONS = (1, 4, 16)
DIL_SLOTS = 8
DIL_HEAD_DIM = 128
DIL_W = DIL_SLOTS * DIL_HEAD_DIM
DIL_STEPS = 128
DIL_BLOCK = 128
REL_BUCKETS = 32
REL_MAX_DIST = 2048
D_FF = 5632
RMS_EPS = 1e-6
NEG_INF = -1e30

LANES = 128
PROJ_ROWS = 1024
VMEM_LIMIT = 56 * 1024 * 1024


def _params(n_axes, vmem=VMEM_LIMIT):
    return pltpu.CompilerParams(dimension_semantics=("arbitrary",) * n_axes, vmem_limit_bytes=vmem)


def _dot(a, b):
    return jnp.dot(a, b, preferred_element_type=F32)


def _dot_t(a, b, dims):
    return lax.dot_general(a, b, (dims, ((), ())), preferred_element_type=F32)


def _split_bf16(x):
    hi = x.astype(BF16)
    lo = (x - hi.astype(F32)).astype(BF16)
    return hi, lo


def _in_proj_kernel(x_ref, g_ref, w_ref, o_ref, h_ref, *, n_cls, rows):
    @pl.when(pl.program_id(2) == 0)
    def _():
        for r in range(n_cls):
            x = x_ref[:, r * D_MODEL:(r + 1) * D_MODEL]
            ms = jnp.mean(x * x, axis=-1, keepdims=True)
            h_ref[r * rows:(r + 1) * rows, :] = (x * lax.rsqrt(ms + RMS_EPS) * g_ref[...]).astype(BF16)

    res = _dot(h_ref[...], w_ref[...])
    for r in range(n_cls):
        o_ref[r] = res[r * rows:(r + 1) * rows].astype(o_ref.dtype)


def _in_proj(x, norm_w, w, dilation, bn):
    B, S, D = x.shape
    N = w.shape[1]
    L = S // dilation
    n_cls = min(dilation, PROJ_ROWS // min(L, PROJ_ROWS))
    rows = PROJ_ROWS // n_cls
    xv = x.reshape(B, L, dilation * D)
    grid = (B, (dilation // n_cls) * (L // rows), N // bn)
    n_row_tiles = L // rows

    def x_map(b, t, j):
        return (b, t % n_row_tiles, t // n_row_tiles)

    def o_map(b, t, j):
        return (b, t // n_row_tiles, t % n_row_tiles, j)

    return pl.pallas_call(
        functools.partial(_in_proj_kernel, n_cls=n_cls, rows=rows),
        grid=grid,
        in_specs=[
            pl.BlockSpec((None, rows, n_cls * D), x_map),
            pl.BlockSpec((1, D), lambda b, t, j: (0, 0)),
            pl.BlockSpec((D, bn), lambda b, t, j: (0, j)),
        ],
        out_specs=pl.BlockSpec((None, n_cls, rows, bn), o_map),
        out_shape=jax.ShapeDtypeStruct((B, dilation, L, N), BF16),
        scratch_shapes=[pltpu.VMEM((PROJ_ROWS, D), BF16)],
        compiler_params=_params(3),
        name=f"in_proj_d{dilation}",
    )(xv, norm_w, w)


def _gla_kernel(q_ref, k_ref, v_ref, g_ref, lr_ref, wup_ref, bgk_ref, nw_ref, o_ref, s_ref):
    C = q_ref.shape[0]

    @pl.when(pl.program_id(1) == 0)
    def _():
        s_ref[...] = jnp.zeros_like(s_ref)

    row = lax.broadcasted_iota(jnp.int32, (C, C), 0)
    col = lax.broadcasted_iota(jnp.int32, (C, C), 1)
    causal = row >= col
    tri = jnp.where(causal, 1.0, 0.0).astype(BF16)
    ones = jnp.ones((C, LANES), BF16)
    lr = lr_ref[...]
    scale = GLA_DK ** -0.5
    for h in range(GLA_HEADS):
        ks = slice(h * GLA_DK, (h + 1) * GLA_DK)
        vs = slice(h * GLA_DV, (h + 1) * GLA_DV)
        pre = _dot(lr, wup_ref[:, ks]) + bgk_ref[:, ks]
        log_sig = jnp.minimum(pre, 0.0) - jnp.log1p(jnp.exp(-jnp.abs(pre)))
        gk_hi, gk_lo = _split_bf16(log_sig * (1.0 / GLA_GATE_NORM))
        b = _dot(tri, gk_hi) + _dot(tri, gk_lo)
        tot_col = _dot_t(gk_hi, ones, ((0,), (0,))) + _dot_t(gk_lo, ones, ((0,), (0,)))
        b_last = b[C - 1:C, :]
        b_mid = b[C // 2 - 1:C // 2, :]
        q = q_ref[:, ks].astype(F32)
        k = k_ref[:, ks].astype(F32)
        v = v_ref[:, vs]
        q_mid = (q * jnp.exp(b - b_mid) * scale).astype(BF16)
        k_mid = (k * jnp.exp(b_mid - b)).astype(BF16)
        att = _dot_t(q_mid, k_mid, ((1,), (1,)))
        att = jnp.where(causal, att, 0.0).astype(BF16)
        s_old = s_ref[h]
        q_in = (q * jnp.exp(b) * scale).astype(BF16)
        o = _dot(att, v) + _dot(q_in, s_old.astype(BF16))
        k_end = (k * jnp.exp(b_last - b)).astype(BF16)
        decay = jnp.exp(tot_col)
        s_ref[h] = jnp.concatenate([decay] * (GLA_DV // LANES), axis=1) * s_old + _dot_t(k_end, v, ((0,), (0,)))
        ms = jnp.mean(o * o, axis=-1, keepdims=True)
        g = g_ref[:, vs].astype(F32)
        y = (o * lax.rsqrt(ms + RMS_EPS) * nw_ref[...]) * (g * jax.nn.sigmoid(g))
        o_ref[:, vs] = y.astype(o_ref.dtype)


def _gla(z, lr_col_block, wup, bgk, norm_w, B, S):
    C = GLA_BLOCK
    nc = S // C
    return pl.pallas_call(
        _gla_kernel,
        grid=(B, nc),
        in_specs=[
            pl.BlockSpec((C, GLA_QK_W), lambda b, c: (b * nc + c, 0)),
            pl.BlockSpec((C, GLA_QK_W), lambda b, c: (b * nc + c, 1)),
            pl.BlockSpec((C, GLA_V_W), lambda b, c: (b * nc + c, 1)),
            pl.BlockSpec((C, GLA_V_W), lambda b, c: (b * nc + c, 2)),
            pl.BlockSpec((C, LANES), lambda b, c: (b * nc + c, lr_col_block)),
            pl.BlockSpec((LANES, GLA_QK_W), lambda b, c: (0, 0)),
            pl.BlockSpec((1, GLA_QK_W), lambda b, c: (0, 0)),
            pl.BlockSpec((1, GLA_DV), lambda b, c: (0, 0)),
        ],
        out_specs=pl.BlockSpec((C, GLA_V_W), lambda b, c: (b * nc + c, 0)),
        out_shape=jax.ShapeDtypeStruct((B * S, GLA_V_W), BF16),
        scratch_shapes=[pltpu.VMEM((GLA_HEADS, GLA_DK, GLA_DV), F32)],
        compiler_params=_params(2),
        name="gla",
    )(z, z, z, z, z, wup, bgk, norm_w)


def _dil_attn_kernel(q_ref, k_ref, v_ref, bias_ref, o_ref, lse_ref, *, n_cls, n_blk):
    scale = DIL_HEAD_DIM ** -0.5
    lane = lax.broadcasted_iota(jnp.int32, (DIL_BLOCK, LANES), 1)

    def block(c, start, first):
        lse_tile = jnp.zeros((DIL_BLOCK, LANES), F32)
        rows_q = pl.ds(start, DIL_BLOCK)
        rows_k = rows_q if first else pl.ds(start - DIL_BLOCK, 2 * DIL_BLOCK)
        for h in range(DIL_SLOTS):
            cq = slice(h * DIL_HEAD_DIM, (h + 1) * DIL_HEAD_DIM)
            q = q_ref[c, rows_q, cq]
            kw = k_ref[c, rows_k, cq]
            vw = v_ref[c, rows_k, cq]
            bias = bias_ref[h, :, DIL_BLOCK:] if first else bias_ref[h]
            s = _dot_t(q, kw, ((1,), (1,))) * scale + bias
            m = jnp.max(s, axis=-1, keepdims=True)
            p = jnp.exp(s - m)
            l = jnp.sum(p, axis=-1, keepdims=True)
            num = _dot(p.astype(BF16), vw)
            o_ref[c, rows_q, cq] = (num / l).astype(o_ref.dtype)
            lse_tile = jnp.where(lane == h, m + jnp.log(l), lse_tile)
        lse_ref[c, rows_q, :] = lse_tile

    for c in range(n_cls):
        block(c, 0, True)
        if n_blk > 1:
            def body(n, carry, c=c):
                block(c, pl.multiple_of(n * DIL_BLOCK, DIL_BLOCK), False)
                return carry
            lax.fori_loop(1, n_blk, body, 0)


def _dil_attn(qkv, col_block, bias):
    B, d, L, _ = qkv.shape
    n_cls = min(d, max(1, 512 // L))

    def part(p):
        return pl.BlockSpec((None, n_cls, L, DIL_W), lambda b, r: (b, r, 0, col_block + p))

    return pl.pallas_call(
        functools.partial(_dil_attn_kernel, n_cls=n_cls, n_blk=L // DIL_BLOCK),
        grid=(B, d // n_cls),
        in_specs=[
            part(0), part(1), part(2),
            pl.BlockSpec((DIL_SLOTS, DIL_BLOCK, 2 * DIL_BLOCK), lambda b, r: (0, 0, 0)),
        ],
        out_specs=[
            pl.BlockSpec((None, n_cls, L, DIL_W), lambda b, r: (b, r, 0, 0)),
            pl.BlockSpec((None, n_cls, L, LANES), lambda b, r: (b, r, 0, 0)),
        ],
        out_shape=[
            jax.ShapeDtypeStruct((B, d, L, DIL_W), BF16),
            jax.ShapeDtypeStruct((B, d, L, LANES), F32),
        ],
        compiler_params=_params(2),
        name=f"dil_attn_d{d}",
    )(qkv, qkv, qkv, bias)


def _dil_combine_kernel(o1_ref, o4_ref, o16_ref, l1_ref, l4_ref, l16_ref, e_ref, out_ref):
    for r in range(16):
        r4, q4 = r % 4, r // 4
        outs = (
            o1_ref[:, r * DIL_W:(r + 1) * DIL_W],
            o4_ref[r4, :, q4 * DIL_W:(q4 + 1) * DIL_W],
            o16_ref[r],
        )
        lses = (
            l1_ref[:, r * LANES:(r + 1) * LANES],
            l4_ref[r4, :, q4 * LANES:(q4 + 1) * LANES],
            l16_ref[r],
        )
        m = jnp.maximum(jnp.maximum(lses[0], lses[1]), lses[2])
        es = [jnp.exp(l - m) for l in lses]
        den = es[0] + es[1] + es[2]
        acc = jnp.zeros((outs[0].shape[0], DIL_W), F32)
        for e, o in zip(es, outs):
            w_hi, w_lo = _split_bf16(e / den)
            w_wide = _dot(w_hi, e_ref[...]) + _dot(w_lo, e_ref[...])
            acc = acc + w_wide * o.astype(F32)
        out_ref[:, r * DIL_W:(r + 1) * DIL_W] = acc.astype(out_ref.dtype)


def _dil_combine(o1, o4, o16, l1, l4, l16, expand):
    B, _, S, _ = o1.shape
    R = S // 16
    o1v = o1.reshape(B, R, 16 * DIL_W)
    o4v = o4.reshape(B, 4, R, 4 * DIL_W)
    l1v = l1.reshape(B, R, 16 * LANES)
    l4v = l4.reshape(B, 4, R, 4 * LANES)
    out = pl.pallas_call(
        _dil_combine_kernel,
        grid=(B,),
        in_specs=[
            pl.BlockSpec((None, R, 16 * DIL_W), lambda b: (b, 0, 0)),
            pl.BlockSpec((None, 4, R, 4 * DIL_W), lambda b: (b, 0, 0, 0)),
            pl.BlockSpec((None, 16, R, DIL_W), lambda b: (b, 0, 0, 0)),
            pl.BlockSpec((None, R, 16 * LANES), lambda b: (b, 0, 0)),
            pl.BlockSpec((None, 4, R, 4 * LANES), lambda b: (b, 0, 0, 0)),
            pl.BlockSpec((None, 16, R, LANES), lambda b: (b, 0, 0, 0)),
            pl.BlockSpec((LANES, DIL_W), lambda b: (0, 0)),
        ],
        out_specs=pl.BlockSpec((None, R, 16 * DIL_W), lambda b: (b, 0, 0)),
        out_shape=jax.ShapeDtypeStruct((B, R, 16 * DIL_W), BF16),
        compiler_params=_params(1),
        name="dil_combine",
    )(o1v, o4v, o16, l1v, l4v, l16, expand)
    return out.reshape(B * S, DIL_W)


def _t5_bucket(dist):
    max_exact = REL_BUCKETS // 2
    d = np.maximum(dist, 1).astype(np.float64)
    large = max_exact + (np.log(d / max_exact) / np.log(REL_MAX_DIST / max_exact)
                         * (REL_BUCKETS - max_exact)).astype(np.int64)
    large = np.minimum(large, REL_BUCKETS - 1)
    return np.where(dist < max_exact, dist, large).astype(np.int32)


def _band_bias(rel_bias_g, dilation):
    a_idx = np.arange(DIL_BLOCK)[:, None]
    c_idx = np.arange(2 * DIL_BLOCK)[None, :]
    steps = DIL_BLOCK + a_idx - c_idx
    in_band = (steps >= 0) & (steps <= DIL_STEPS)
    bucket = _t5_bucket(np.clip(steps, 0, None) * dilation)
    bias = jnp.transpose(rel_bias_g[bucket], (2, 0, 1)).astype(F32)
    return jnp.where(in_band[None], bias, NEG_INF)


def _merge_kernel(og_ref, od_ref, ga_ref, gb_ref, x_ref, wa_ref, wb_ref, wo_ref, bg_ref, o_ref):
    a = _dot(og_ref[...], wa_ref[...])
    bm = _dot(od_ref[...], wb_ref[...])
    gate_a = jax.nn.sigmoid(ga_ref[...].astype(F32) + bg_ref[:, :D_MODEL])
    gate_b = jax.nn.sigmoid(gb_ref[...].astype(F32) + bg_ref[:, D_MODEL:])
    merged = (gate_a * a + gate_b * bm).astype(BF16)
    o_ref[...] = x_ref[...] + _dot(merged, wo_ref[...])


def _merge(o_gla, o_dil, z, gate_col_block, x2d, wa, wb, wo, b_gate, bm):
    M = x2d.shape[0]
    const = lambda i: (0, 0)
    resident = dict(pipeline_mode=pl.Buffered(1))
    return pl.pallas_call(
        _merge_kernel,
        grid=(M // bm,),
        in_specs=[
            pl.BlockSpec((bm, GLA_V_W), lambda i: (i, 0)),
            pl.BlockSpec((bm, DIL_W), lambda i: (i, 0)),
            pl.BlockSpec((bm, D_MODEL), lambda i: (i, gate_col_block)),
            pl.BlockSpec((bm, D_MODEL), lambda i: (i, gate_col_block + 1)),
            pl.BlockSpec((bm, D_MODEL), lambda i: (i, 0)),
            pl.BlockSpec((GLA_V_W, D_MODEL), const, **resident),
            pl.BlockSpec((DIL_W, D_MODEL), const, **resident),
            pl.BlockSpec((D_MODEL, D_MODEL), const, **resident),
            pl.BlockSpec((1, 2 * D_MODEL), const),
        ],
        out_specs=pl.BlockSpec((bm, D_MODEL), lambda i: (i, 0)),
        out_shape=jax.ShapeDtypeStruct((M, D_MODEL), F32),
        compiler_params=_params(1),
        name="merge_out_proj",
    )(o_gla, o_dil, z, z, x2d, wa, wb, wo, b_gate)


def _ffn_kernel(x_ref, nw_ref, wg_ref, wu_ref, wd_ref, fw_ref, o_ref, h_ref, acc_ref):
    f = pl.program_id(1)

    @pl.when(f == 0)
    def _():
        x = x_ref[...]
        ms = jnp.mean(x * x, axis=-1, keepdims=True)
        h_ref[...] = (x * lax.rsqrt(ms + RMS_EPS) * nw_ref[...]).astype(BF16)
        acc_ref[...] = x

    h = h_ref[...]
    gate = _dot(h, wg_ref[...])
    up = _dot(h, wu_ref[...])
    act = (gate * jax.nn.sigmoid(gate) * up).astype(BF16)
    acc_ref[...] += _dot(act, wd_ref[...])

    @pl.when(f == pl.num_programs(1) - 1)
    def _():
        y = acc_ref[...]
        ms = jnp.mean(y * y, axis=-1, keepdims=True)
        o_ref[...] = y * lax.rsqrt(ms + RMS_EPS) * fw_ref[...]


def _ffn(x1, norm_w, w_in, w_out, final_w, bm, bf):
    M = x1.shape[0]
    nf = D_FF // bf
    return pl.pallas_call(
        _ffn_kernel,
        grid=(M // bm, nf),
        in_specs=[
            pl.BlockSpec((bm, D_MODEL), lambda i, f: (i, 0)),
            pl.BlockSpec((1, D_MODEL), lambda i, f: (0, 0)),
            pl.BlockSpec((D_MODEL, bf), lambda i, f: (0, f)),
            pl.BlockSpec((D_MODEL, bf), lambda i, f: (0, f + nf)),
            pl.BlockSpec((bf, D_MODEL), lambda i, f: (f, 0)),
            pl.BlockSpec((1, D_MODEL), lambda i, f: (0, 0)),
        ],
        out_specs=pl.BlockSpec((bm, D_MODEL), lambda i, f: (i, 0)),
        out_shape=jax.ShapeDtypeStruct((M, D_MODEL), F32),
        scratch_shapes=[pltpu.VMEM((bm, D_MODEL), BF16), pltpu.VMEM((bm, D_MODEL), F32)],
        compiler_params=_params(2),
        name="swiglu_final_norm",
    )(x1, norm_w, w_in, w_in, w_out, final_w)


def kernel(x, attn_norm, w_in, w_gk_up, b_gk, gla_norm, b_gate, w_branch_gla, w_branch_dil,
           w_out, ffn_norm, w_ffn_in, w_ffn_out, rel_bias, final_norm):
    B, S, D = x.shape
    assert D == D_MODEL and S % (16 * DIL_BLOCK) == 0 and attn_norm.shape[0] == 1

    o_lr = 2 * GLA_QK_W + 2 * GLA_V_W
    o_dil = o_lr + GLA_LOWRANK
    o_gate = o_dil + 9 * DIL_W
    wi = w_in[0]
    w_nat = jnp.concatenate([
        wi[:, :o_lr], wi[:, o_gate:], wi[:, o_dil:o_dil + 3 * DIL_W], wi[:, o_lr:o_dil],
        jnp.zeros((D, LANES - GLA_LOWRANK), wi.dtype)], axis=1).astype(BF16)
    w_d4 = wi[:, o_dil + 3 * DIL_W:o_dil + 6 * DIL_W].astype(BF16)
    w_d16 = wi[:, o_dil + 6 * DIL_W:o_dil + 9 * DIL_W].astype(BF16)
    n_nat = w_nat.shape[1]
    gate_col_block = o_lr // D_MODEL
    d1_col = o_lr + 2 * D_MODEL
    lr_col_block = (d1_col + 3 * DIL_W) // LANES

    an = attn_norm[0][None, :]
    z_nat = _in_proj(x, an, w_nat, 1, n_nat // 7)
    z_d4 = _in_proj(x, an, w_d4, 4, 3 * DIL_W // 2)
    z_d16 = _in_proj(x, an, w_d16, 16, 3 * DIL_W // 2)
    z2d = z_nat.reshape(B * S, n_nat)

    wup = jnp.concatenate([w_gk_up[0], jnp.zeros((LANES - GLA_LOWRANK, GLA_QK_W), F32)], axis=0).astype(BF16)
    o_gla = _gla(z2d, lr_col_block, wup, b_gk[0][None, :], gla_norm[0][None, :], B, S)

    groups = []
    for gi, (zg, col_block) in enumerate(((z_nat, d1_col // DIL_W), (z_d4, 0), (z_d16, 0))):
        bias = _band_bias(rel_bias[:, gi * DIL_SLOTS:(gi + 1) * DIL_SLOTS], DIL_DILATIONS[gi])
        groups.append(_dil_attn(zg, col_block, bias))
    o_dil = _dil_combine(groups[0][0], groups[1][0], groups[2][0],
                         groups[0][1], groups[1][1], groups[2][1], jnp.asarray(_expand_matrix(), BF16))

    x2d = x.reshape(B * S, D)
    x1 = _merge(o_gla, o_dil, z2d, gate_col_block, x2d,
                w_branch_gla[0].astype(BF16), w_branch_dil[0].astype(BF16), w_out[0].astype(BF16),
                b_gate[0][None, :], 256)
    out = _ffn(x1, ffn_norm[0][None, :], w_ffn_in[0].astype(BF16), w_ffn_out[0].astype(BF16),
               final_norm[None, :], 512, 512)
    return out.reshape(B, S, D)


def _expand_matrix():
    e = np.zeros((LANES, DIL_W), np.float32)
    for h in range(DIL_SLOTS):
        e[h, h * DIL_HEAD_DIM:(h + 1) * DIL_HEAD_DIM] = 1.0
    return e
```

```python
import functools

import numpy as np
import jax
import jax.numpy as jnp
from jax import lax
from jax.experimental import pallas as pl
from jax.experimental.pallas import tpu as pltpu

F32 = jnp.float32
BF16 = jnp.bfloat16

D_MODEL = 2048
GLA_HEADS = 4
GLA_DK = 256
GLA_DV = 512
GLA_QK_W = GLA_HEADS * GLA_DK
GLA_V_W = GLA_HEADS * GLA_DV
GLA_LOWRANK = 16
GLA_GATE_NORM = 16.0
GLA_BLOCK = 128
DIL_DILATIONS = (1, 4, 16)
DIL_SLOTS = 8
DIL_HEAD_DIM = 128
DIL_W = DIL_SLOTS * DIL_HEAD_DIM
DIL_STEPS = 128
DIL_BLOCK = 128
REL_BUCKETS = 32
REL_MAX_DIST = 2048
D_FF = 5632
RMS_EPS = 1e-6
NEG_INF = -1e30

LANES = 128
PROJ_ROWS = 1024
VMEM_LIMIT = 56 * 1024 * 1024


def _params(n_axes, vmem=VMEM_LIMIT):
    return pltpu.CompilerParams(dimension_semantics=("arbitrary",) * n_axes, vmem_limit_bytes=vmem)


def _dot(a, b):
    return jnp.dot(a, b, preferred_element_type=F32)


def _dot_t(a, b, dims):
    return lax.dot_general(a, b, (dims, ((), ())), preferred_element_type=F32)


def _split_bf16(x):
    hi = x.astype(BF16)
    lo = (x - hi.astype(F32)).astype(BF16)
    return hi, lo


def _in_proj_kernel(x_ref, g_ref, w_ref, o_ref, h_ref, *scratch, dilation):
    rows = x_ref.shape[0]
    bn = w_ref.shape[1]

    @pl.when(pl.program_id(2) == 0)
    def _():
        x = x_ref[...]
        ms = jnp.mean(x * x, axis=-1, keepdims=True)
        h_ref[...] = (x * lax.rsqrt(ms + RMS_EPS) * g_ref[...]).astype(BF16)

    res = _dot(h_ref[...], w_ref[...])
    if dilation == 1:
        o_ref[0] = res.astype(o_ref.dtype)
    else:
        slab_ref, = scratch
        for c in range(bn // LANES):
            slab_ref[c] = res[:, c * LANES:(c + 1) * LANES]
        for r in range(dilation):
            for c in range(bn // LANES):
                o_ref[r, :, c * LANES:(c + 1) * LANES] = slab_ref[
                    c, pl.ds(r, rows // dilation, stride=dilation), :].astype(o_ref.dtype)


def _in_proj(x, norm_w, w, col0, n_cols, dilation, bn):
    B, S, D = x.shape
    rows = PROJ_ROWS
    assert col0 % bn == 0 and n_cols % bn == 0
    scratch = [pltpu.VMEM((rows, D), BF16)]
    if dilation > 1:
        scratch.append(pltpu.VMEM((bn // LANES, rows, LANES), F32))
    return pl.pallas_call(
        functools.partial(_in_proj_kernel, dilation=dilation),
        grid=(B, S // rows, n_cols // bn),
        in_specs=[
            pl.BlockSpec((None, rows, D), lambda b, t, j: (b, t, 0)),
            pl.BlockSpec((1, D), lambda b, t, j: (0, 0)),
            pl.BlockSpec((D, bn), lambda b, t, j: (0, col0 // bn + j)),
        ],
        out_specs=pl.BlockSpec((None, dilation, rows // dilation, bn), lambda b, t, j: (b, 0, t, j)),
        out_shape=jax.ShapeDtypeStruct((B, dilation, S // dilation, n_cols), BF16),
        scratch_shapes=scratch,
        compiler_params=_params(3),
        name=f"in_proj_d{dilation}",
    )(x, norm_w, w)


def _gla_kernel(q_ref, k_ref, v_ref, g_ref, lr_ref, wup_ref, bgk_ref, nw_ref, o_ref, s_ref):
    C = q_ref.shape[0]

    @pl.when(pl.program_id(1) == 0)
    def _():
        s_ref[...] = jnp.zeros_like(s_ref)

    row = lax.broadcasted_iota(jnp.int32, (C, C), 0)
    col = lax.broadcasted_iota(jnp.int32, (C, C), 1)
    causal = row >= col
    tri = jnp.where(causal, 1.0, 0.0).astype(BF16)
    ones = jnp.ones((C, LANES), BF16)
    lr = lr_ref[...]
    scale = GLA_DK ** -0.5
    for h in range(GLA_HEADS):
        ks = slice(h * GLA_DK, (h + 1) * GLA_DK)
        vs = slice(h * GLA_DV, (h + 1) * GLA_DV)
        pre = _dot(lr, wup_ref[:, ks]) + bgk_ref[:, ks]
        log_sig = jnp.minimum(pre, 0.0) - jnp.log1p(jnp.exp(-jnp.abs(pre)))
        gk_hi, gk_lo = _split_bf16(log_sig * (1.0 / GLA_GATE_NORM))
        b = _dot(tri, gk_hi) + _dot(tri, gk_lo)
        tot_col = _dot_t(gk_hi, ones, ((0,), (0,))) + _dot_t(gk_lo, ones, ((0,), (0,)))
        b_last = b[C - 1:C, :]
        b_mid = b[C // 2 - 1:C // 2, :]
        q = q_ref[:, ks].astype(F32)
        k = k_ref[:, ks].astype(F32)
        v = v_ref[:, vs]
        q_mid = (q * jnp.exp(b - b_mid) * scale).astype(BF16)
        k_mid = (k * jnp.exp(b_mid - b)).astype(BF16)
        att = _dot_t(q_mid, k_mid, ((1,), (1,)))
        att = jnp.where(causal, att, 0.0).astype(BF16)
        s_old = s_ref[h]
        q_in = (q * jnp.exp(b) * scale).astype(BF16)
        o = _dot(att, v) + _dot(q_in, s_old.astype(BF16))
        k_end = (k * jnp.exp(b_last - b)).astype(BF16)
        decay = jnp.exp(tot_col)
        s_ref[h] = jnp.concatenate([decay] * (GLA_DV // LANES), axis=1) * s_old + _dot_t(k_end, v, ((0,), (0,)))
        ms = jnp.mean(o * o, axis=-1, keepdims=True)
        g = g_ref[:, vs].astype(F32)
        y = (o * lax.rsqrt(ms + RMS_EPS) * nw_ref[...]) * (g * jax.nn.sigmoid(g))
        o_ref[:, vs] = y.astype(o_ref.dtype)


def _gla(z, lr_col_block, wup, bgk, norm_w, B, S):
    C = GLA_BLOCK
    nc = S // C
    return pl.pallas_call(
        _gla_kernel,
        grid=(B, nc),
        in_specs=[
            pl.BlockSpec((C, GLA_QK_W), lambda b, c: (b * nc + c, 0)),
            pl.BlockSpec((C, GLA_QK_W), lambda b, c: (b * nc + c, 1)),
            pl.BlockSpec((C, GLA_V_W), lambda b, c: (b * nc + c, 1)),
            pl.BlockSpec((C, GLA_V_W), lambda b, c: (b * nc + c, 2)),
            pl.BlockSpec((C, LANES), lambda b, c: (b * nc + c, lr_col_block)),
            pl.BlockSpec((LANES, GLA_QK_W), lambda b, c: (0, 0)),
            pl.BlockSpec((1, GLA_QK_W), lambda b, c: (0, 0)),
            pl.BlockSpec((1, GLA_DV), lambda b, c: (0, 0)),
        ],
        out_specs=pl.BlockSpec((C, GLA_V_W), lambda b, c: (b * nc + c, 0)),
        out_shape=jax.ShapeDtypeStruct((B * S, GLA_V_W), BF16),
        scratch_shapes=[pltpu.VMEM((GLA_HEADS, GLA_DK, GLA_DV), F32)],
        compiler_params=_params(2),
        name="gla",
    )(z, z, z, z, z, wup, bgk, norm_w)


def _dil_attn_kernel(q_ref, k_ref, v_ref, bucket_ref, table_ref, o_ref, lse_ref, bias_ref, *,
                     n_cls, n_blk, buckets, head0):
    scale = DIL_HEAD_DIM ** -0.5
    lane = lax.broadcasted_iota(jnp.int32, (DIL_BLOCK, LANES), 1)

    @pl.when((pl.program_id(0) == 0) & (pl.program_id(1) == 0))
    def _():
        bucket = bucket_ref[...]
        for h in range(DIL_SLOTS):
            bias = jnp.full(bucket.shape, NEG_INF, F32)
            for k in buckets:
                bias = jnp.where(bucket == k, table_ref[k, head0 + h], bias)
            bias_ref[h] = bias

    def block(c, start, first):
        lse_tile = jnp.zeros((DIL_BLOCK, LANES), F32)
        rows_q = pl.ds(start, DIL_BLOCK)
        rows_k = rows_q if first else pl.ds(start - DIL_BLOCK, 2 * DIL_BLOCK)
        for h in range(DIL_SLOTS):
            cq = slice(h * DIL_HEAD_DIM, (h + 1) * DIL_HEAD_DIM)
            q = q_ref[c, rows_q, cq]
            kw = k_ref[c, rows_k, cq]
            vw = v_ref[c, rows_k, cq]
            bias = bias_ref[h, :, DIL_BLOCK:] if first else bias_ref[h]
            s = _dot_t(q, kw, ((1,), (1,))) * scale + bias
            m = jnp.max(s, axis=-1, keepdims=True)
            p = jnp.exp(s - m)
            l = jnp.sum(p, axis=-1, keepdims=True)
            num = _dot(p.astype(BF16), vw)
            o_ref[c, rows_q, cq] = (num / l).astype(o_ref.dtype)
            lse_tile = jnp.where(lane == h, m + jnp.log(l), lse_tile)
        lse_ref[c, rows_q, :] = lse_tile

    for c in range(n_cls):
        block(c, 0, True)
        if n_blk > 1:
            def body(n, carry, c=c):
                block(c, pl.multiple_of(n * DIL_BLOCK, DIL_BLOCK), False)
                return carry
            lax.fori_loop(1, n_blk, body, 0)


def _band_buckets(dilation):
    a_idx = np.arange(DIL_BLOCK)[:, None]
    c_idx = np.arange(2 * DIL_BLOCK)[None, :]
    steps = DIL_BLOCK + a_idx - c_idx
    in_band = (steps >= 0) & (steps <= DIL_STEPS)
    bucket = _t5_bucket(np.clip(steps, 0, None) * dilation)
    return np.where(in_band, bucket, -1).astype(np.int32)


def _dil_attn(qkv, col_block, rel_bias, group):
    B, d, L, _ = qkv.shape
    n_cls = min(d, max(1, 512 // L))
    bucket = _band_buckets(d)
    buckets = tuple(int(k) for k in np.unique(bucket) if k >= 0)

    def part(p):
        return pl.BlockSpec((None, n_cls, L, DIL_W), lambda b, r: (b, r, 0, col_block + p))

    return pl.pallas_call(
        functools.partial(_dil_attn_kernel, n_cls=n_cls, n_blk=L // DIL_BLOCK, buckets=buckets,
                          head0=group * DIL_SLOTS),
        grid=(B, d // n_cls),
        in_specs=[
            part(0), part(1), part(2),
            pl.BlockSpec((DIL_BLOCK, 2 * DIL_BLOCK), lambda b, r: (0, 0)),
            pl.BlockSpec(memory_space=pltpu.SMEM),
        ],
        scratch_shapes=[pltpu.VMEM((DIL_SLOTS, DIL_BLOCK, 2 * DIL_BLOCK), F32)],
        out_specs=[
            pl.BlockSpec((None, n_cls, L, DIL_W), lambda b, r: (b, r, 0, 0)),
            pl.BlockSpec((None, n_cls, L, LANES), lambda b, r: (b, r, 0, 0)),
        ],
        out_shape=[
            jax.ShapeDtypeStruct((B, d, L, DIL_W), BF16),
            jax.ShapeDtypeStruct((B, d, L, LANES), F32),
        ],
        compiler_params=_params(2),
        name=f"dil_attn_d{d}",
    )(qkv, qkv, qkv, jnp.asarray(bucket), rel_bias)


COMBINE_ROWS = 1024
COMBINE_CHUNK = 256


def _dil_combine_kernel(o1_ref, o4_ref, o16_ref, l1_ref, l4_ref, l16_ref, e_ref, out_ref,
                        s4_ref, s16_ref, sl4_ref, sl16_ref):
    T = o1_ref.shape[0]
    n_slab = DIL_W // LANES
    for d, o_ref, l_ref, s_ref, sl_ref in ((4, o4_ref, l4_ref, s4_ref, sl4_ref),
                                            (16, o16_ref, l16_ref, s16_ref, sl16_ref)):
        for r in range(d):
            rows = pl.ds(r, T // d, stride=d)
            sl_ref[rows, :] = l_ref[r]
            for c in range(n_slab):
                s_ref[c, rows, :] = o_ref[r, :, c * LANES:(c + 1) * LANES].astype(F32)

    def chunk(i, carry):
        rows = pl.ds(pl.multiple_of(i * COMBINE_CHUNK, COMBINE_CHUNK), COMBINE_CHUNK)
        lses = (l1_ref[rows, :], sl4_ref[rows, :], sl16_ref[rows, :])
        m = jnp.maximum(jnp.maximum(lses[0], lses[1]), lses[2])
        es = [jnp.exp(l - m) for l in lses]
        den = es[0] + es[1] + es[2]
        wide = []
        for e in es:
            w_hi, w_lo = _split_bf16(e / den)
            wide.append(_dot(w_hi, e_ref[...]) + _dot(w_lo, e_ref[...]))
        for c in range(n_slab):
            cs = slice(c * LANES, (c + 1) * LANES)
            acc = (wide[0][:, cs] * o1_ref[rows, cs].astype(F32)
                   + wide[1][:, cs] * s4_ref[c, rows, :]
                   + wide[2][:, cs] * s16_ref[c, rows, :])
            out_ref[rows, cs] = acc.astype(out_ref.dtype)
        return carry

    lax.fori_loop(0, T // COMBINE_CHUNK, chunk, 0)


def _dil_combine(o1, o4, o16, l1, l4, l16, expand):
    B, _, S, _ = o1.shape
    T = COMBINE_ROWS
    nt = S // T
    n_slab = DIL_W // LANES
    return pl.pallas_call(
        _dil_combine_kernel,
        grid=(B, nt),
        in_specs=[
            pl.BlockSpec((None, None, T, DIL_W), lambda b, t: (b, 0, t, 0)),
            pl.BlockSpec((None, 4, T // 4, DIL_W), lambda b, t: (b, 0, t, 0)),
            pl.BlockSpec((None, 16, T // 16, DIL_W), lambda b, t: (b, 0, t, 0)),
            pl.BlockSpec((None, None, T, LANES), lambda b, t: (b, 0, t, 0)),
            pl.BlockSpec((None, 4, T // 4, LANES), lambda b, t: (b, 0, t, 0)),
            pl.BlockSpec((None, 16, T // 16, LANES), lambda b, t: (b, 0, t, 0)),
            pl.BlockSpec((LANES, DIL_W), lambda b, t: (0, 0)),
        ],
        out_specs=pl.BlockSpec((T, DIL_W), lambda b, t: (b * nt + t, 0)),
        out_shape=jax.ShapeDtypeStruct((B * S, DIL_W), BF16),
        scratch_shapes=[
            pltpu.VMEM((n_slab, T, LANES), F32), pltpu.VMEM((n_slab, T, LANES), F32),
            pltpu.VMEM((T, LANES), F32), pltpu.VMEM((T, LANES), F32),
        ],
        compiler_params=_params(2),
        name="dil_combine",
    )(o1, o4, o16, l1, l4, l16, expand)


def _t5_bucket(dist):
    max_exact = REL_BUCKETS // 2
    d = np.maximum(dist, 1).astype(np.float64)
    large = max_exact + (np.log(d / max_exact) / np.log(REL_MAX_DIST / max_exact)
                         * (REL_BUCKETS - max_exact)).astype(np.int64)
    large = np.minimum(large, REL_BUCKETS - 1)
    return np.where(dist < max_exact, dist, large).astype(np.int32)


def _merge_kernel(og_ref, od_ref, ga_ref, gb_ref, x_ref, wa_ref, wb_ref, wo_ref, bg_ref, o_ref):
    a = _dot(og_ref[...], wa_ref[...])
    bm = _dot(od_ref[...], wb_ref[...])
    gate_a = jax.nn.sigmoid(ga_ref[...].astype(F32) + bg_ref[:, :D_MODEL])
    gate_b = jax.nn.sigmoid(gb_ref[...].astype(F32) + bg_ref[:, D_MODEL:])
    merged = (gate_a * a + gate_b * bm).astype(BF16)
    o_ref[...] = x_ref[...] + _dot(merged, wo_ref[...])


def _merge(o_gla, o_dil, z, gate_col_block, x2d, wa, wb, wo, b_gate, bm):
    M = x2d.shape[0]
    const = lambda i: (0, 0)
    resident = dict(pipeline_mode=pl.Buffered(1))
    return pl.pallas_call(
        _merge_kernel,
        grid=(M // bm,),
        in_specs=[
            pl.BlockSpec((bm, GLA_V_W), lambda i: (i, 0)),
            pl.BlockSpec((bm, DIL_W), lambda i: (i, 0)),
            pl.BlockSpec((bm, D_MODEL), lambda i: (i, gate_col_block)),
            pl.BlockSpec((bm, D_MODEL), lambda i: (i, gate_col_block + 1)),
            pl.BlockSpec((bm, D_MODEL), lambda i: (i, 0)),
            pl.BlockSpec((GLA_V_W, D_MODEL), const, **resident),
            pl.BlockSpec((DIL_W, D_MODEL), const, **resident),
            pl.BlockSpec((D_MODEL, D_MODEL), const, **resident),
            pl.BlockSpec((1, 2 * D_MODEL), const),
        ],
        out_specs=pl.BlockSpec((bm, D_MODEL), lambda i: (i, 0)),
        out_shape=jax.ShapeDtypeStruct((M, D_MODEL), F32),
        compiler_params=_params(1),
        name="merge_out_proj",
    )(o_gla, o_dil, z, z, x2d, wa, wb, wo, b_gate)


def _ffn_kernel(x_ref, nw_ref, wg_ref, wu_ref, wd_ref, fw_ref, o_ref, h_ref, acc_ref):
    f = pl.program_id(1)

    @pl.when(f == 0)
    def _():
        x = x_ref[...]
        ms = jnp.mean(x * x, axis=-1, keepdims=True)
        h_ref[...] = (x * lax.rsqrt(ms + RMS_EPS) * nw_ref[...]).astype(BF16)
        acc_ref[...] = x

    h = h_ref[...]
    gate = _dot(h, wg_ref[...])
    up = _dot(h, wu_ref[...])
    act = (gate * jax.nn.sigmoid(gate) * up).astype(BF16)
    acc_ref[...] += _dot(act, wd_ref[...])

    @pl.when(f == pl.num_programs(1) - 1)
    def _():
        y = acc_ref[...]
        ms = jnp.mean(y * y, axis=-1, keepdims=True)
        o_ref[...] = y * lax.rsqrt(ms + RMS_EPS) * fw_ref[...]


def _ffn(x1, norm_w, w_in, w_out, final_w, bm, bf):
    M = x1.shape[0]
    nf = D_FF // bf
    return pl.pallas_call(
        _ffn_kernel,
        grid=(M // bm, nf),
        in_specs=[
            pl.BlockSpec((bm, D_MODEL), lambda i, f: (i, 0)),
            pl.BlockSpec((1, D_MODEL), lambda i, f: (0, 0)),
            pl.BlockSpec((D_MODEL, bf), lambda i, f: (0, f)),
            pl.BlockSpec((D_MODEL, bf), lambda i, f: (0, f + nf)),
            pl.BlockSpec((bf, D_MODEL), lambda i, f: (f, 0)),
            pl.BlockSpec((1, D_MODEL), lambda i, f: (0, 0)),
        ],
        out_specs=pl.BlockSpec((bm, D_MODEL), lambda i, f: (i, 0)),
        out_shape=jax.ShapeDtypeStruct((M, D_MODEL), F32),
        scratch_shapes=[pltpu.VMEM((bm, D_MODEL), BF16), pltpu.VMEM((bm, D_MODEL), F32)],
        compiler_params=_params(2),
        name="swiglu_final_norm",
    )(x1, norm_w, w_in, w_in, w_out, final_w)


def kernel(x, attn_norm, w_in, w_gk_up, b_gk, gla_norm, b_gate, w_branch_gla, w_branch_dil,
           w_out, ffn_norm, w_ffn_in, w_ffn_out, rel_bias, final_norm):
    B, S, D = x.shape
    assert D == D_MODEL and S % (16 * DIL_BLOCK) == 0 and attn_norm.shape[0] == 1

    o_lr = 2 * GLA_QK_W + 2 * GLA_V_W
    o_dil = o_lr + GLA_LOWRANK
    o_gate = o_dil + 9 * DIL_W
    wi = w_in[0]
    n_nat = o_lr + 2 * D_MODEL + 3 * DIL_W + LANES
    bn_nat, bn_dil = n_nat // 7, 3 * DIL_W // 2
    col_d4 = -(-n_nat // bn_dil) * bn_dil
    col_d16 = col_d4 + 3 * DIL_W
    w_all = jnp.concatenate([
        wi[:, :o_lr], wi[:, o_gate:], wi[:, o_dil:o_dil + 3 * DIL_W], wi[:, o_lr:o_dil],
        jnp.zeros((D, LANES - GLA_LOWRANK + col_d4 - n_nat), wi.dtype),
        wi[:, o_dil + 3 * DIL_W:o_dil + 9 * DIL_W]], axis=1).astype(BF16)
    gate_col_block = o_lr // D_MODEL
    d1_col = o_lr + 2 * D_MODEL
    lr_col_block = (d1_col + 3 * DIL_W) // LANES

    an = attn_norm[0][None, :]
    z_nat = _in_proj(x, an, w_all, 0, n_nat, 1, bn_nat)
    z_d4 = _in_proj(x, an, w_all, col_d4, 3 * DIL_W, 4, bn_dil)
    z_d16 = _in_proj(x, an, w_all, col_d16, 3 * DIL_W, 16, bn_dil)
    z2d = z_nat.reshape(B * S, n_nat)

    wup = jnp.concatenate([w_gk_up[0], jnp.zeros((LANES - GLA_LOWRANK, GLA_QK_W), F32)], axis=0).astype(BF16)
    o_gla = _gla(z2d, lr_col_block, wup, b_gk[0][None, :], gla_norm[0][None, :], B, S)

    groups = []
    for gi, (zg, col_block) in enumerate(((z_nat, d1_col // DIL_W), (z_d4, 0), (z_d16, 0))):
        groups.append(_dil_attn(zg, col_block, rel_bias, gi))
    o_dil = _dil_combine(groups[0][0], groups[1][0], groups[2][0],
                         groups[0][1], groups[1][1], groups[2][1], jnp.asarray(_expand_matrix(), BF16))

    x2d = x.reshape(B * S, D)
    x1 = _merge(o_gla, o_dil, z2d, gate_col_block, x2d,
                w_branch_gla[0].astype(BF16), w_branch_dil[0].astype(BF16), w_out[0].astype(BF16),
                b_gate[0][None, :], 256)
    out = _ffn(x1, ffn_norm[0][None, :], w_ffn_in[0].astype(BF16), w_ffn_out[0].astype(BF16),
               final_norm[None, :], 512, 512)
    return out.reshape(B, S, D)


def _expand_matrix():
    e = np.zeros((LANES, DIL_W), np.float32)
    for h in range(DIL_SLOTS):
        e[h, h * DIL_HEAD_DIM:(h + 1) * DIL_HEAD_DIM] = 1.0
    return e
```

```python
import functools

import numpy as np
import jax
import jax.numpy as jnp
from jax import lax
from jax.experimental import pallas as pl
from jax.experimental.pallas import tpu as pltpu

F32 = jnp.float32
BF16 = jnp.bfloat16

D_MODEL = 2048
GLA_HEADS = 4
GLA_DK = 256
GLA_DV = 512
GLA_QK_W = GLA_HEADS * GLA_DK
GLA_V_W = GLA_HEADS * GLA_DV
GLA_LOWRANK = 16
GLA_GATE_NORM = 16.0
GLA_BLOCK = 128
GLA_STEP_ROWS = 256
DIL_DILATIONS = (1, 4, 16)
DIL_SLOTS = 8
DIL_HEAD_DIM = 128
DIL_W = DIL_SLOTS * DIL_HEAD_DIM
DIL_STEPS = 128
DIL_BLOCK = 128
REL_BUCKETS = 32
REL_MAX_DIST = 2048
D_FF = 5632
RMS_EPS = 1e-6
NEG_INF = -1e30

LANES = 128
PROJ_ROWS = 1024
VMEM_LIMIT = 56 * 1024 * 1024


def _params(n_axes, vmem=VMEM_LIMIT):
    return pltpu.CompilerParams(dimension_semantics=("arbitrary",) * n_axes, vmem_limit_bytes=vmem)


def _dot(a, b):
    return jnp.dot(a, b, preferred_element_type=F32)


def _dot_t(a, b, dims):
    return lax.dot_general(a, b, (dims, ((), ())), preferred_element_type=F32)


def _split_bf16(x):
    hi = x.astype(BF16)
    lo = (x - hi.astype(F32)).astype(BF16)
    return hi, lo


def _in_proj_kernel(x_ref, g_ref, w_ref, o_ref, h_ref, *scratch, dilation):
    rows = x_ref.shape[0]
    bn = w_ref.shape[1]

    @pl.when(pl.program_id(2) == 0)
    def _():
        x = x_ref[...]
        ms = jnp.mean(x * x, axis=-1, keepdims=True)
        h_ref[...] = (x * lax.rsqrt(ms + RMS_EPS) * g_ref[...]).astype(BF16)

    res = _dot(h_ref[...], w_ref[...])
    if dilation == 1:
        o_ref[0] = res.astype(o_ref.dtype)
    else:
        slab_ref = scratch[0]
        for c in range(bn // LANES):
            slab_ref[c] = res[:, c * LANES:(c + 1) * LANES]
        passes, rest = [], dilation
        while rest > 1:
            passes.append(min(rest, 4))
            rest //= passes[-1]
        src_ref, n_groups = slab_ref, 1
        for p, stride in enumerate(passes):
            group_rows = rows // n_groups
            last = p == len(passes) - 1
            for g in range(n_groups):
                for r in range(stride):
                    for c in range(bn // LANES):
                        part = src_ref[c, pl.ds(g * group_rows + r, group_rows // stride, stride=stride), :]
                        if last:
                            o_ref[g + n_groups * r, :, c * LANES:(c + 1) * LANES] = part.astype(o_ref.dtype)
                        else:
                            dst = (g + n_groups * r) * (group_rows // stride)
                            scratch[1][c, dst:dst + group_rows // stride, :] = part
            if not last:
                src_ref, n_groups = scratch[1], n_groups * stride


def _in_proj(x, norm_w, w, col0, n_cols, dilation, bn):
    B, S, D = x.shape
    rows = PROJ_ROWS
    assert col0 % bn == 0 and n_cols % bn == 0
    scratch = [pltpu.VMEM((rows, D), BF16)]
    if dilation > 1:
        scratch.append(pltpu.VMEM((bn // LANES, rows, LANES), F32))
    if dilation > 4:
        scratch.append(pltpu.VMEM((bn // LANES, rows, LANES), F32))
    return pl.pallas_call(
        functools.partial(_in_proj_kernel, dilation=dilation),
        grid=(B, S // rows, n_cols // bn),
        in_specs=[
            pl.BlockSpec((None, rows, D), lambda b, t, j: (b, t, 0)),
            pl.BlockSpec((1, D), lambda b, t, j: (0, 0)),
            pl.BlockSpec((D, bn), lambda b, t, j: (0, col0 // bn + j)),
        ],
        out_specs=pl.BlockSpec((None, dilation, rows // dilation, bn), lambda b, t, j: (b, 0, t, j)),
        out_shape=jax.ShapeDtypeStruct((B, dilation, S // dilation, n_cols), BF16),
        scratch_shapes=scratch,
        compiler_params=_params(3),
        name=f"in_proj_d{dilation}",
    )(x, norm_w, w)


def _pack_w_in_kernel(w_ref, o_ref, *, segments, zero_range):
    o_ref[:, zero_range[0]:zero_range[0] + zero_range[1]] = jnp.zeros((o_ref.shape[0], zero_range[1]), o_ref.dtype)
    for dst, src, width in segments:
        o_ref[:, dst:dst + width] = w_ref[:, src:src + width].astype(o_ref.dtype)


def _pack_w_in(w, segments, zero_range, n_out, rows=128):
    D, N = w.shape
    return pl.pallas_call(
        functools.partial(_pack_w_in_kernel, segments=segments, zero_range=zero_range),
        grid=(D // rows,),
        in_specs=[pl.BlockSpec((rows, N), lambda i: (i, 0))],
        out_specs=pl.BlockSpec((rows, n_out), lambda i: (i, 0)),
        out_shape=jax.ShapeDtypeStruct((D, n_out), BF16),
        compiler_params=_params(1),
        name="pack_w_in",
    )(w)


def _gla_kernel(q_ref, k_ref, v_ref, g_ref, lr_ref, wup_ref, bgk_ref, nw_ref, o_ref, s_ref):
    C = GLA_BLOCK

    @pl.when(pl.program_id(1) == 0)
    def _():
        s_ref[...] = jnp.zeros_like(s_ref)

    row = lax.broadcasted_iota(jnp.int32, (C, C), 0)
    col = lax.broadcasted_iota(jnp.int32, (C, C), 1)
    causal = row >= col
    tri = jnp.where(causal, 1.0, 0.0).astype(BF16)
    ones = jnp.ones((C, LANES), BF16)
    scale = GLA_DK ** -0.5
    for blk, h in [(blk, h) for blk in range(q_ref.shape[0] // C) for h in range(GLA_HEADS)]:
        rs = slice(blk * C, (blk + 1) * C)
        ks = slice(h * GLA_DK, (h + 1) * GLA_DK)
        vs = slice(h * GLA_DV, (h + 1) * GLA_DV)
        pre = _dot(lr_ref[rs, :], wup_ref[:, ks]) + bgk_ref[:, ks]
        log_sig = jnp.minimum(pre, 0.0) - jnp.log(1.0 + jnp.exp(-jnp.abs(pre)))
        gk_hi, gk_lo = _split_bf16(log_sig * (1.0 / GLA_GATE_NORM))
        b = _dot(tri, gk_hi) + _dot(tri, gk_lo)
        tot_col = _dot_t(gk_hi, ones, ((0,), (0,))) + _dot_t(gk_lo, ones, ((0,), (0,)))
        b_last = b[C - 1:C, :]
        b_mid = b[C // 2 - 1:C // 2, :]
        q = q_ref[rs, ks].astype(F32)
        k = k_ref[rs, ks].astype(F32)
        v = v_ref[rs, vs]
        q_dec = q * jnp.exp(b - b_mid)
        k_dec = k * jnp.exp(b_mid - b)
        att = _dot_t((q_dec * scale).astype(BF16), k_dec.astype(BF16), ((1,), (1,)))
        att = jnp.where(causal, att, 0.0).astype(BF16)
        s_old = s_ref[h]
        q_in = (q_dec * (jnp.exp(b_mid) * scale)).astype(BF16)
        o = _dot(att, v) + _dot(q_in, s_old.astype(BF16))
        k_end = (k_dec * jnp.exp(b_last - b_mid)).astype(BF16)
        decay = jnp.exp(tot_col)
        s_ref[h] = jnp.concatenate([decay] * (GLA_DV // LANES), axis=1) * s_old + _dot_t(k_end, v, ((0,), (0,)))
        ms = jnp.mean(o * o, axis=-1, keepdims=True)
        g = g_ref[rs, vs].astype(F32)
        y = (o * lax.rsqrt(ms + RMS_EPS) * nw_ref[...]) * (g * jax.nn.sigmoid(g))
        o_ref[rs, vs] = y.astype(o_ref.dtype)


def _gla(z, lr_col_block, wup, bgk, norm_w, B, S):
    C = GLA_STEP_ROWS
    nc = S // C
    return pl.pallas_call(
        _gla_kernel,
        grid=(B, nc),
        in_specs=[
            pl.BlockSpec((C, GLA_QK_W), lambda b, c: (b * nc + c, 0)),
            pl.BlockSpec((C, GLA_QK_W), lambda b, c: (b * nc + c, 1)),
            pl.BlockSpec((C, GLA_V_W), lambda b, c: (b * nc + c, 1)),
            pl.BlockSpec((C, GLA_V_W), lambda b, c: (b * nc + c, 2)),
            pl.BlockSpec((C, LANES), lambda b, c: (b * nc + c, lr_col_block)),
            pl.BlockSpec((LANES, GLA_QK_W), lambda b, c: (0, 0)),
            pl.BlockSpec((1, GLA_QK_W), lambda b, c: (0, 0)),
            pl.BlockSpec((1, GLA_DV), lambda b, c: (0, 0)),
        ],
        out_specs=pl.BlockSpec((C, GLA_V_W), lambda b, c: (b * nc + c, 0)),
        out_shape=jax.ShapeDtypeStruct((B * S, GLA_V_W), BF16),
        scratch_shapes=[pltpu.VMEM((GLA_HEADS, GLA_DK, GLA_DV), F32)],
        compiler_params=_params(2),
        name="gla",
    )(z, z, z, z, z, wup, bgk, norm_w)


def _dil_attn_kernel(q_ref, k_ref, v_ref, bucket_ref, table_ref, o_ref, lse_ref, bias_ref, *,
                     n_cls, n_blk, buckets, head0):
    scale = DIL_HEAD_DIM ** -0.5
    lane = lax.broadcasted_iota(jnp.int32, (DIL_BLOCK, LANES), 1)

    @pl.when((pl.program_id(0) == 0) & (pl.program_id(1) == 0))
    def _():
        bucket = bucket_ref[...]
        for h in range(DIL_SLOTS):
            bias = jnp.full(bucket.shape, NEG_INF, F32)
            for k in buckets:
                bias = jnp.where(bucket == k, table_ref[k, head0 + h], bias)
            bias_ref[h] = bias

    def block(c, start, first):
        lse_tile = jnp.zeros((DIL_BLOCK, LANES), F32)
        rows_q = pl.ds(start, DIL_BLOCK)
        rows_k = rows_q if first else pl.ds(start - DIL_BLOCK, 2 * DIL_BLOCK)
        n_keys = DIL_BLOCK if first else 2 * DIL_BLOCK
        ones = jnp.ones((n_keys, DIL_HEAD_DIM), BF16)
        for h in range(DIL_SLOTS):
            cq = slice(h * DIL_HEAD_DIM, (h + 1) * DIL_HEAD_DIM)
            q = q_ref[c, rows_q, cq]
            kw = k_ref[c, rows_k, cq]
            vw = v_ref[c, rows_k, cq]
            bias = bias_ref[h, :, DIL_BLOCK:] if first else bias_ref[h]
            s = _dot_t(q, kw, ((1,), (1,))) * scale + bias
            m = jnp.max(s, axis=-1, keepdims=True)
            p = jnp.exp(s - m).astype(BF16)
            num_l = _dot(p, jnp.concatenate([vw, ones], axis=1))
            l = num_l[:, DIL_HEAD_DIM:]
            o_ref[c, rows_q, cq] = (num_l[:, :DIL_HEAD_DIM] / l).astype(o_ref.dtype)
            lse_tile = jnp.where(lane == h, m + jnp.log(l), lse_tile)
        lse_ref[c, rows_q, :] = lse_tile

    def one_class(c, carry):
        block(c, 0, True)
        if n_blk > 1:
            def body(n, inner):
                block(c, pl.multiple_of(n * DIL_BLOCK, DIL_BLOCK), False)
                return inner
            lax.fori_loop(1, n_blk, body, 0)
        return carry

    if n_cls == 1:
        one_class(0, 0)
    else:
        lax.fori_loop(0, n_cls, one_class, 0)


def _band_buckets(dilation):
    a_idx = np.arange(DIL_BLOCK)[:, None]
    c_idx = np.arange(2 * DIL_BLOCK)[None, :]
    steps = DIL_BLOCK + a_idx - c_idx
    in_band = (steps >= 0) & (steps <= DIL_STEPS)
    bucket = _t5_bucket(np.clip(steps, 0, None) * dilation)
    return np.where(in_band, bucket, -1).astype(np.int32)


def _dil_attn(qkv, col_block, rel_bias, group):
    B, d, L, _ = qkv.shape
    n_cls = d
    bucket = _band_buckets(d)
    buckets = tuple(int(k) for k in np.unique(bucket) if k >= 0)

    def part(p):
        return pl.BlockSpec((None, n_cls, L, DIL_W), lambda b, r: (b, r, 0, col_block + p))

    return pl.pallas_call(
        functools.partial(_dil_attn_kernel, n_cls=n_cls, n_blk=L // DIL_BLOCK, buckets=buckets,
                          head0=group * DIL_SLOTS),
        grid=(B, d // n_cls),
        in_specs=[
            part(0), part(1), part(2),
            pl.BlockSpec((DIL_BLOCK, 2 * DIL_BLOCK), lambda b, r: (0, 0)),
            pl.BlockSpec(memory_space=pltpu.SMEM),
        ],
        scratch_shapes=[pltpu.VMEM((DIL_SLOTS, DIL_BLOCK, 2 * DIL_BLOCK), F32)],
        out_specs=[
            pl.BlockSpec((None, n_cls, L, DIL_W), lambda b, r: (b, r, 0, 0)),
            pl.BlockSpec((None, n_cls, L, LANES), lambda b, r: (b, r, 0, 0)),
        ],
        out_shape=[
            jax.ShapeDtypeStruct((B, d, L, DIL_W), BF16),
            jax.ShapeDtypeStruct((B, d, L, LANES), F32),
        ],
        compiler_params=_params(2),
        name=f"dil_attn_d{d}",
    )(qkv, qkv, qkv, jnp.asarray(bucket), rel_bias)


COMBINE_ROWS = 1024
COMBINE_CHUNK = 256


def _dil_combine_kernel(o1_ref, o4_ref, o16_ref, l1_ref, l4_ref, l16_ref, e_ref, out_ref,
                        s4_ref, s16_ref, sl4_ref, sl16_ref):
    T = o1_ref.shape[0]
    n_slab = DIL_W // LANES
    for d, o_ref, l_ref, s_ref, sl_ref in ((4, o4_ref, l4_ref, s4_ref, sl4_ref),
                                            (16, o16_ref, l16_ref, s16_ref, sl16_ref)):
        for r in range(d):
            rows = pl.ds(r, T // d, stride=d)
            sl_ref[rows, :] = l_ref[r]
            for c in range(n_slab):
                s_ref[c, rows, :] = o_ref[r, :, c * LANES:(c + 1) * LANES].astype(F32)

    def chunk(i, carry):
        rows = pl.ds(pl.multiple_of(i * COMBINE_CHUNK, COMBINE_CHUNK), COMBINE_CHUNK)
        lses = (l1_ref[rows, :], sl4_ref[rows, :], sl16_ref[rows, :])
        m = jnp.maximum(jnp.maximum(lses[0], lses[1]), lses[2])
        es = [jnp.exp(l - m) for l in lses]
        den = es[0] + es[1] + es[2]
        wide = []
        for e in es:
            w_hi, w_lo = _split_bf16(e / den)
            wide.append(_dot(w_hi, e_ref[...]) + _dot(w_lo, e_ref[...]))
        for c in range(n_slab):
            cs = slice(c * LANES, (c + 1) * LANES)
            acc = (wide[0][:, cs] * o1_ref[rows, cs].astype(F32)
                   + wide[1][:, cs] * s4_ref[c, rows, :]
                   + wide[2][:, cs] * s16_ref[c, rows, :])
            out_ref[rows, cs] = acc.astype(out_ref.dtype)
        return carry

    lax.fori_loop(0, T // COMBINE_CHUNK, chunk, 0)


def _dil_combine(o1, o4, o16, l1, l4, l16, expand):
    B, _, S, _ = o1.shape
    T = COMBINE_ROWS
    nt = S // T
    n_slab = DIL_W // LANES
    return pl.pallas_call(
        _dil_combine_kernel,
        grid=(B, nt),
        in_specs=[
            pl.BlockSpec((None, None, T, DIL_W), lambda b, t: (b, 0, t, 0)),
            pl.BlockSpec((None, 4, T // 4, DIL_W), lambda b, t: (b, 0, t, 0)),
            pl.BlockSpec((None, 16, T // 16, DIL_W), lambda b, t: (b, 0, t, 0)),
            pl.BlockSpec((None, None, T, LANES), lambda b, t: (b, 0, t, 0)),
            pl.BlockSpec((None, 4, T // 4, LANES), lambda b, t: (b, 0, t, 0)),
            pl.BlockSpec((None, 16, T // 16, LANES), lambda b, t: (b, 0, t, 0)),
            pl.BlockSpec((LANES, DIL_W), lambda b, t: (0, 0)),
        ],
        out_specs=pl.BlockSpec((T, DIL_W), lambda b, t: (b * nt + t, 0)),
        out_shape=jax.ShapeDtypeStruct((B * S, DIL_W), BF16),
        scratch_shapes=[
            pltpu.VMEM((n_slab, T, LANES), F32), pltpu.VMEM((n_slab, T, LANES), F32),
            pltpu.VMEM((T, LANES), F32), pltpu.VMEM((T, LANES), F32),
        ],
        compiler_params=_params(2),
        name="dil_combine",
    )(o1, o4, o16, l1, l4, l16, expand)


def _t5_bucket(dist):
    max_exact = REL_BUCKETS // 2
    d = np.maximum(dist, 1).astype(np.float64)
    large = max_exact + (np.log(d / max_exact) / np.log(REL_MAX_DIST / max_exact)
                         * (REL_BUCKETS - max_exact)).astype(np.int64)
    large = np.minimum(large, REL_BUCKETS - 1)
    return np.where(dist < max_exact, dist, large).astype(np.int32)


def _merge_kernel(og_ref, od_ref, ga_ref, gb_ref, x_ref, wa_ref, wb_ref, wo_ref, bg_ref, o_ref):
    a = _dot(og_ref[...], wa_ref[...])
    bm = _dot(od_ref[...], wb_ref[...])
    gate_a = jax.nn.sigmoid(ga_ref[...].astype(F32) + bg_ref[:, :D_MODEL])
    gate_b = jax.nn.sigmoid(gb_ref[...].astype(F32) + bg_ref[:, D_MODEL:])
    merged = (gate_a * a + gate_b * bm).astype(BF16)
    o_ref[...] = x_ref[...] + _dot(merged, wo_ref[...])


def _merge(o_gla, o_dil, z, gate_col_block, x2d, wa, wb, wo, b_gate, bm):
    M = x2d.shape[0]
    const = lambda i: (0, 0)
    resident = dict(pipeline_mode=pl.Buffered(1))
    return pl.pallas_call(
        _merge_kernel,
        grid=(M // bm,),
        in_specs=[
            pl.BlockSpec((bm, GLA_V_W), lambda i: (i, 0)),
            pl.BlockSpec((bm, DIL_W), lambda i: (i, 0)),
            pl.BlockSpec((bm, D_MODEL), lambda i: (i, gate_col_block)),
            pl.BlockSpec((bm, D_MODEL), lambda i: (i, gate_col_block + 1)),
            pl.BlockSpec((bm, D_MODEL), lambda i: (i, 0)),
            pl.BlockSpec((GLA_V_W, D_MODEL), const, **resident),
            pl.BlockSpec((DIL_W, D_MODEL), const, **resident),
            pl.BlockSpec((D_MODEL, D_MODEL), const, **resident),
            pl.BlockSpec((1, 2 * D_MODEL), const),
        ],
        out_specs=pl.BlockSpec((bm, D_MODEL), lambda i: (i, 0)),
        out_shape=jax.ShapeDtypeStruct((M, D_MODEL), F32),
        compiler_params=_params(1),
        name="merge_out_proj",
    )(o_gla, o_dil, z, z, x2d, wa, wb, wo, b_gate)


def _ffn_kernel(x_ref, nw_ref, wg_ref, wu_ref, wd_ref, fw_ref, o_ref, h_ref):
    f = pl.program_id(1)

    @pl.when(f == 0)
    def _():
        x = x_ref[...]
        ms = jnp.mean(x * x, axis=-1, keepdims=True)
        h_ref[...] = (x * lax.rsqrt(ms + RMS_EPS) * nw_ref[...]).astype(BF16)
        o_ref[...] = x

    h = h_ref[...]
    gate = _dot(h, wg_ref[...])
    up = _dot(h, wu_ref[...])
    act = (gate * jax.nn.sigmoid(gate) * up).astype(BF16)
    o_ref[...] += _dot(act, wd_ref[...])

    @pl.when(f == pl.num_programs(1) - 1)
    def _():
        y = o_ref[...]
        ms = jnp.mean(y * y, axis=-1, keepdims=True)
        o_ref[...] = y * lax.rsqrt(ms + RMS_EPS) * fw_ref[...]


def _ffn(x1, norm_w, w_in, w_out, final_w, bm, bf):
    M = x1.shape[0]
    nf = D_FF // bf
    return pl.pallas_call(
        _ffn_kernel,
        grid=(M // bm, nf),
        in_specs=[
            pl.BlockSpec((bm, D_MODEL), lambda i, f: (i, 0)),
            pl.BlockSpec((1, D_MODEL), lambda i, f: (0, 0)),
            pl.BlockSpec((D_MODEL, bf), lambda i, f: (0, f)),
            pl.BlockSpec((D_MODEL, bf), lambda i, f: (0, f + nf)),
            pl.BlockSpec((bf, D_MODEL), lambda i, f: (f, 0)),
            pl.BlockSpec((1, D_MODEL), lambda i, f: (0, 0)),
        ],
        out_specs=pl.BlockSpec((bm, D_MODEL), lambda i, f: (i, 0)),
        out_shape=jax.ShapeDtypeStruct((M, D_MODEL), F32),
        scratch_shapes=[pltpu.VMEM((bm, D_MODEL), BF16)],
        compiler_params=_params(2),
        name="swiglu_final_norm",
    )(x1, norm_w, w_in, w_in, w_out, final_w)


def kernel(x, attn_norm, w_in, w_gk_up, b_gk, gla_norm, b_gate, w_branch_gla, w_branch_dil,
           w_out, ffn_norm, w_ffn_in, w_ffn_out, rel_bias, final_norm):
    B, S, D = x.shape
    assert D == D_MODEL and S % (16 * DIL_BLOCK) == 0 and attn_norm.shape[0] == 1

    o_lr = 2 * GLA_QK_W + 2 * GLA_V_W
    o_dil = o_lr + GLA_LOWRANK
    o_gate = o_dil + 9 * DIL_W
    wi = w_in[0]
    n_nat = o_lr + 2 * D_MODEL + 3 * DIL_W + LANES
    bn_nat, bn_dil = n_nat // 7, DIL_W
    col_d4 = -(-n_nat // bn_dil) * bn_dil
    col_d16 = col_d4 + 3 * DIL_W
    gate_col_block = o_lr // D_MODEL
    d1_col = o_lr + 2 * D_MODEL
    lr_col = d1_col + 3 * DIL_W
    lr_col_block = lr_col // LANES
    w_all = _pack_w_in(
        wi,
        segments=((0, 0, o_lr), (o_lr, o_gate, 2 * D_MODEL), (d1_col, o_dil, 3 * DIL_W),
                  (lr_col, o_lr, GLA_LOWRANK), (col_d4, o_dil + 3 * DIL_W, 6 * DIL_W)),
        zero_range=(lr_col, col_d4 - lr_col), n_out=col_d16 + 3 * DIL_W)

    an = attn_norm[0][None, :]
    z_nat = _in_proj(x, an, w_all, 0, n_nat, 1, bn_nat)
    z_d4 = _in_proj(x, an, w_all, col_d4, 3 * DIL_W, 4, bn_dil)
    z_d16 = _in_proj(x, an, w_all, col_d16, 3 * DIL_W, 16, bn_dil)
    z2d = z_nat.reshape(B * S, n_nat)

    wup = jnp.concatenate([w_gk_up[0], jnp.zeros((LANES - GLA_LOWRANK, GLA_QK_W), F32)], axis=0).astype(BF16)
    o_gla = _gla(z2d, lr_col_block, wup, b_gk[0][None, :], gla_norm[0][None, :], B, S)

    groups = []
    for gi, (zg, col_block) in enumerate(((z_nat, d1_col // DIL_W), (z_d4, 0), (z_d16, 0))):
        groups.append(_dil_attn(zg, col_block, rel_bias, gi))
    o_dil = _dil_combine(groups[0][0], groups[1][0], groups[2][0],
                         groups[0][1], groups[1][1], groups[2][1], jnp.asarray(_expand_matrix(), BF16))

    x2d = x.reshape(B * S, D)
    x1 = _merge(o_gla, o_dil, z2d, gate_col_block, x2d,
                w_branch_gla[0].astype(BF16), w_branch_dil[0].astype(BF16), w_out[0].astype(BF16),
                b_gate[0][None, :], 256)
    out = _ffn(x1, ffn_norm[0][None, :], w_ffn_in[0].astype(BF16), w_ffn_out[0].astype(BF16),
               final_norm[None, :], 512, 512)
    return out.reshape(B, S, D)


def _expand_matrix():
    e = np.zeros((LANES, DIL_W), np.float32)
    for h in range(DIL_SLOTS):
        e[h, h * DIL_HEAD_DIM:(h + 1) * DIL_HEAD_DIM] = 1.0
    return e
```

```python
import functools

import numpy as np
import jax
import jax.numpy as jnp
from jax import lax
from jax.experimental import pallas as pl
from jax.experimental.pallas import tpu as pltpu

F32 = jnp.float32
BF16 = jnp.bfloat16

D_MODEL = 2048
GLA_HEADS = 4
GLA_DK = 256
GLA_DV = 512
GLA_QK_W = GLA_HEADS * GLA_DK
GLA_V_W = GLA_HEADS * GLA_DV
GLA_LOWRANK = 16
GLA_GATE_NORM = 16.0
GLA_BLOCK = 128
GLA_STEP_ROWS = 256
DIL_DILATIONS = (1, 4, 16)
DIL_SLOTS = 8
DIL_HEAD_DIM = 128
DIL_W = DIL_SLOTS * DIL_HEAD_DIM
DIL_STEPS = 128
DIL_BLOCK = 128
REL_BUCKETS = 32
REL_MAX_DIST = 2048
D_FF = 5632
RMS_EPS = 1e-6
NEG_INF = -1e30

LANES = 128
SUBLANES = 8
PROJ_ROWS = 1024
PROJ_CHUNK = 256
VMEM_LIMIT = 56 * 1024 * 1024


def _params(n_axes, vmem=VMEM_LIMIT):
    return pltpu.CompilerParams(dimension_semantics=("arbitrary",) * n_axes, vmem_limit_bytes=vmem)


def _dot(a, b):
    return jnp.dot(a, b, preferred_element_type=F32)


def _dot_t(a, b, dims):
    return lax.dot_general(a, b, (dims, ((), ())), preferred_element_type=F32)


def _split_bf16(x):
    hi = x.astype(BF16)
    lo = (x - hi.astype(F32)).astype(BF16)
    return hi, lo


def _in_proj_norm_kernel(x_ref, g_ref, w_ref, o_ref, h_ref):
    @pl.when(pl.program_id(2) == 0)
    def _():
        x = x_ref[...]
        ms = jnp.mean(x * x, axis=-1, keepdims=True)
        h_ref[...] = (x * lax.rsqrt(ms + RMS_EPS) * g_ref[...]).astype(h_ref.dtype)

    o_ref[...] = _dot(h_ref[...], w_ref[...]).astype(o_ref.dtype)


def _in_proj_norm(x, norm_w, w, n_cols, bn):
    B, S, D = x.shape
    rows = PROJ_ROWS
    return pl.pallas_call(
        _in_proj_norm_kernel,
        grid=(B, S // rows, n_cols // bn),
        in_specs=[
            pl.BlockSpec((None, rows, D), lambda b, t, j: (b, t, 0)),
            pl.BlockSpec((1, D), lambda b, t, j: (0, 0)),
            pl.BlockSpec((D, bn), lambda b, t, j: (0, j)),
        ],
        out_specs=[
            pl.BlockSpec((None, rows, bn), lambda b, t, j: (b, t, j)),
            pl.BlockSpec((None, rows, D), lambda b, t, j: (b, t, 0)),
        ],
        out_shape=[jax.ShapeDtypeStruct((B, S, n_cols), BF16), jax.ShapeDtypeStruct((B, S, D), BF16)],
        compiler_params=_params(3),
        name="in_proj_norm",
    )(x, norm_w, w)


def _in_proj_dil_kernel(h_ref, w_ref, o_ref, *scratch, dilation):
    rows = h_ref.shape[0]
    bn = w_ref.shape[1]

    passes, rest = [], dilation
    while rest > 1:
        passes.append(min(rest, 4))
        rest //= passes[-1]

    def deinterleave(res, slab0):
        slabs = range(slab0, slab0 + res.shape[1] // LANES)
        for i, c in enumerate(slabs):
            scratch[0][c] = res[:, i * LANES:(i + 1) * LANES]
        src_ref, n_groups = scratch[0], 1
        for p, stride in enumerate(passes):
            group_rows = rows // n_groups
            last = p == len(passes) - 1
            for g in range(n_groups):
                for r in range(stride):
                    for c in slabs:
                        part = src_ref[c, pl.ds(g * group_rows + r, group_rows // stride, stride=stride), :]
                        if last:
                            o_ref[g + n_groups * r, :, c * LANES:(c + 1) * LANES] = part.astype(o_ref.dtype)
                        else:
                            dst = (g + n_groups * r) * (group_rows // stride)
                            scratch[1][c, dst:dst + group_rows // stride, :] = part
            if not last:
                src_ref, n_groups = scratch[1], n_groups * stride

    n_chunks = bn // PROJ_CHUNK
    res = _dot(h_ref[...], w_ref[:, :PROJ_CHUNK])
    for i in range(n_chunks):
        nxt = _dot(h_ref[...], w_ref[:, (i + 1) * PROJ_CHUNK:(i + 2) * PROJ_CHUNK]) if i + 1 < n_chunks else None
        deinterleave(res, i * PROJ_CHUNK // LANES)
        res = nxt


def _in_proj_dil(h, w, col0, n_cols, dilation, bn):
    B, S, D = h.shape
    rows = PROJ_ROWS
    assert col0 % bn == 0 and n_cols % bn == 0 and bn % PROJ_CHUNK == 0
    scratch = [pltpu.VMEM((bn // LANES, rows, LANES), F32)]
    if dilation > 4:
        scratch.append(pltpu.VMEM((bn // LANES, rows, LANES), F32))
    return pl.pallas_call(
        functools.partial(_in_proj_dil_kernel, dilation=dilation),
        grid=(B, S // rows, n_cols // bn),
        in_specs=[
            pl.BlockSpec((None, rows, D), lambda b, t, j: (b, t, 0)),
            pl.BlockSpec((D, bn), lambda b, t, j: (0, col0 // bn + j)),
        ],
        out_specs=pl.BlockSpec((None, dilation, rows // dilation, bn), lambda b, t, j: (b, 0, t, j)),
        out_shape=jax.ShapeDtypeStruct((B, dilation, S // dilation, n_cols), BF16),
        scratch_shapes=scratch,
        compiler_params=_params(3),
        name=f"in_proj_d{dilation}",
    )(h, w)


PACK_COLS = 512


def _pack_w_in_kernel(src_ref, valid_ref, wt_ref, o_ref):
    x = wt_ref[...]
    row = lax.broadcasted_iota(jnp.int32, x.shape, 0)
    x = jnp.where(row < valid_ref[pl.program_id(0)], x, 0.0)
    o_ref[...] = x.T.astype(o_ref.dtype)


def _pack_w_in(wt, segments, n_out):
    N, D = wt.shape
    n_blocks = n_out // PACK_COLS
    src = np.zeros((n_blocks,), np.int32)
    valid = np.zeros((n_blocks,), np.int32)
    for dst, s, width in segments:
        assert dst % PACK_COLS == 0
        for i in range(dst // PACK_COLS, -(-(dst + width) // PACK_COLS)):
            start = s + i * PACK_COLS - dst
            assert start % SUBLANES == 0 and start + PACK_COLS <= N
            src[i] = start // SUBLANES
            valid[i] = min(PACK_COLS, dst + width - i * PACK_COLS)
    return pl.pallas_call(
        _pack_w_in_kernel,
        grid_spec=pltpu.PrefetchScalarGridSpec(
            num_scalar_prefetch=2,
            grid=(n_blocks,),
            in_specs=[pl.BlockSpec((pl.Element(PACK_COLS), pl.Element(D)),
                                   lambda i, src, valid: (src[i] * SUBLANES, 0))],
            out_specs=pl.BlockSpec((D, PACK_COLS), lambda i, src, valid: (0, i)),
        ),
        out_shape=jax.ShapeDtypeStruct((D, n_out), BF16),
        compiler_params=_params(1),
        name="pack_w_in",
    )(jnp.asarray(src), jnp.asarray(valid), wt)


def _gla_kernel(q_ref, k_ref, v_ref, g_ref, lr_ref, wup_ref, bgk_ref, nw_ref, o_ref, s_ref):
    C = GLA_BLOCK

    @pl.when(pl.program_id(1) == 0)
    def _():
        s_ref[...] = jnp.zeros_like(s_ref)

    row = lax.broadcasted_iota(jnp.int32, (C, C), 0)
    col = lax.broadcasted_iota(jnp.int32, (C, C), 1)
    causal = row >= col
    tri = jnp.where(causal, 1.0, 0.0).astype(BF16)
    ones = jnp.ones((C, LANES), BF16)
    scale = GLA_DK ** -0.5
    n_blk = q_ref.shape[0] // C
    heads = range(GLA_HEADS)
    ks = [slice(h * GLA_DK, (h + 1) * GLA_DK) for h in heads]
    vs = [slice(h * GLA_DV, (h + 1) * GLA_DV) for h in heads]
    units = [(blk, h) for blk in range(n_blk) for h in heads]
    rows = {u: slice(u[0] * C, (u[0] + 1) * C) for u in units}
    gks, b_cum, tot, att, q_dec, k_dec = {}, {}, {}, {}, {}, {}
    o_intra, q_in, k_end, decay = {}, {}, {}, {}
    for u in units:
        pre = _dot(lr_ref[rows[u], :], wup_ref[:, ks[u[1]]]) + bgk_ref[:, ks[u[1]]]
        log_sig = jnp.minimum(pre, 0.0) - jnp.log(1.0 + jnp.exp(-jnp.abs(pre)))
        gks[u] = _split_bf16(log_sig * (1.0 / GLA_GATE_NORM))
    for u in units:
        gk_hi, gk_lo = gks[u]
        b_cum[u] = _dot(tri, gk_hi) + _dot(tri, gk_lo)
        tot[u] = _dot_t(gk_hi, ones, ((0,), (0,))) + _dot_t(gk_lo, ones, ((0,), (0,)))
    for u in units:
        b = b_cum[u]
        b_mid = b[C // 2 - 1:C // 2, :]
        q_dec[u] = q_ref[rows[u], ks[u[1]]].astype(F32) * jnp.exp(b - b_mid)
        k_dec[u] = k_ref[rows[u], ks[u[1]]].astype(F32) * jnp.exp(b_mid - b)
        att[u] = _dot_t((q_dec[u] * scale).astype(BF16), k_dec[u].astype(BF16), ((1,), (1,)))
    for u in units:
        b = b_cum[u]
        b_last = b[C - 1:C, :]
        b_mid = b[C // 2 - 1:C // 2, :]
        o_intra[u] = _dot(jnp.where(causal, att[u], 0.0).astype(BF16), v_ref[rows[u], vs[u[1]]])
        q_in[u] = (q_dec[u] * (jnp.exp(b_mid) * scale)).astype(BF16)
        k_end[u] = (k_dec[u] * jnp.exp(b_last - b_mid)).astype(BF16)
        decay[u] = jnp.concatenate([jnp.exp(tot[u])] * (GLA_DV // LANES), axis=1)
    for blk in range(n_blk):
        rs = slice(blk * C, (blk + 1) * C)
        os = []
        for h in heads:
            s_old = s_ref[h]
            os.append(o_intra[blk, h] + _dot(q_in[blk, h], s_old.astype(BF16)))
            s_ref[h] = decay[blk, h] * s_old + _dot_t(k_end[blk, h], v_ref[rs, vs[h]], ((0,), (0,)))
        ys = []
        for h, o in zip(heads, os):
            ms = jnp.mean(o * o, axis=-1, keepdims=True)
            g = g_ref[rs, vs[h]].astype(F32)
            y = (o * lax.rsqrt(ms + RMS_EPS) * nw_ref[...]) * (g * jax.nn.sigmoid(g))
            ys.append(y.astype(o_ref.dtype))
        o_ref[rs, :] = jnp.concatenate(ys, axis=1)


def _gla(z, lr_col_block, wup, bgk, norm_w, B, S):
    C = GLA_STEP_ROWS
    nc = S // C
    return pl.pallas_call(
        _gla_kernel,
        grid=(B, nc),
        in_specs=[
            pl.BlockSpec((C, GLA_QK_W), lambda b, c: (b * nc + c, 0)),
            pl.BlockSpec((C, GLA_QK_W), lambda b, c: (b * nc + c, 1)),
            pl.BlockSpec((C, GLA_V_W), lambda b, c: (b * nc + c, 1)),
            pl.BlockSpec((C, GLA_V_W), lambda b, c: (b * nc + c, 2)),
            pl.BlockSpec((C, LANES), lambda b, c: (b * nc + c, lr_col_block)),
            pl.BlockSpec((LANES, GLA_QK_W), lambda b, c: (0, 0)),
            pl.BlockSpec((1, GLA_QK_W), lambda b, c: (0, 0)),
            pl.BlockSpec((1, GLA_DV), lambda b, c: (0, 0)),
        ],
        out_specs=pl.BlockSpec((C, GLA_V_W), lambda b, c: (b * nc + c, 0)),
        out_shape=jax.ShapeDtypeStruct((B * S, GLA_V_W), BF16),
        scratch_shapes=[pltpu.VMEM((GLA_HEADS, GLA_DK, GLA_DV), F32)],
        compiler_params=_params(2),
        name="gla",
    )(z, z, z, z, z, wup, bgk, norm_w)


def _dil_attn_kernel(q_ref, k_ref, v_ref, bucket_ref, table_ref, o_ref, lse_ref, bias_ref, *,
                     n_cls, n_blk, buckets, head0):
    scale = DIL_HEAD_DIM ** -0.5
    lane = lax.broadcasted_iota(jnp.int32, (DIL_BLOCK, LANES), 1)

    @pl.when((pl.program_id(0) == 0) & (pl.program_id(1) == 0))
    def _():
        bucket = bucket_ref[...]
        for h in range(DIL_SLOTS):
            bias = jnp.full(bucket.shape, NEG_INF, F32)
            for k in buckets:
                bias = jnp.where(bucket == k, table_ref[k, head0 + h], bias)
            bias_ref[h] = bias

    def block(c, start, first):
        lse_tile = jnp.zeros((DIL_BLOCK, LANES), F32)
        rows_q = pl.ds(start, DIL_BLOCK)
        rows_k = rows_q if first else pl.ds(start - DIL_BLOCK, 2 * DIL_BLOCK)
        n_keys = DIL_BLOCK if first else 2 * DIL_BLOCK
        ones = jnp.ones((n_keys, DIL_HEAD_DIM), BF16)
        cols = [slice(h * DIL_HEAD_DIM, (h + 1) * DIL_HEAD_DIM) for h in range(DIL_SLOTS)]
        scores = [
            _dot_t(q_ref[c, rows_q, cq], k_ref[c, rows_k, cq], ((1,), (1,))) * scale
            + (bias_ref[h, :, DIL_BLOCK:] if first else bias_ref[h])
            for h, cq in enumerate(cols)]
        maxes = [jnp.max(s, axis=-1, keepdims=True) for s in scores]
        probs = [jnp.exp(s - m).astype(BF16) for s, m in zip(scores, maxes)]
        num_ls = [_dot(p, jnp.concatenate([v_ref[c, rows_k, cq], ones], axis=1)) for p, cq in zip(probs, cols)]
        outs = []
        for h, (num_l, m) in enumerate(zip(num_ls, maxes)):
            l = num_l[:, DIL_HEAD_DIM:]
            outs.append((num_l[:, :DIL_HEAD_DIM] / l).astype(o_ref.dtype))
            lse_tile = jnp.where(lane == h, m + jnp.log(l), lse_tile)
        o_ref[c, rows_q, :] = jnp.concatenate(outs, axis=1)
        lse_ref[c, rows_q, :] = lse_tile

    def one_class(c, carry):
        block(c, 0, True)
        if n_blk > 1:
            def body(n, inner):
                block(c, pl.multiple_of(n * DIL_BLOCK, DIL_BLOCK), False)
                return inner
            lax.fori_loop(1, n_blk, body, 0)
        return carry

    if n_cls == 1:
        one_class(0, 0)
    else:
        lax.fori_loop(0, n_cls, one_class, 0)


def _band_buckets(dilation):
    a_idx = np.arange(DIL_BLOCK)[:, None]
    c_idx = np.arange(2 * DIL_BLOCK)[None, :]
    steps = DIL_BLOCK + a_idx - c_idx
    in_band = (steps >= 0) & (steps <= DIL_STEPS)
    bucket = _t5_bucket(np.clip(steps, 0, None) * dilation)
    return np.where(in_band, bucket, -1).astype(np.int32)


def _dil_attn(qkv, col_block, rel_bias, group):
    B, d, L, _ = qkv.shape
    n_cls = d
    bucket = _band_buckets(d)
    buckets = tuple(int(k) for k in np.unique(bucket) if k >= 0)

    def part(p):
        return pl.BlockSpec((None, n_cls, L, DIL_W), lambda b, r: (b, r, 0, col_block + p))

    return pl.pallas_call(
        functools.partial(_dil_attn_kernel, n_cls=n_cls, n_blk=L // DIL_BLOCK, buckets=buckets,
                          head0=group * DIL_SLOTS),
        grid=(B, d // n_cls),
        in_specs=[
            part(0), part(1), part(2),
            pl.BlockSpec((DIL_BLOCK, 2 * DIL_BLOCK), lambda b, r: (0, 0)),
            pl.BlockSpec(memory_space=pltpu.SMEM),
        ],
        scratch_shapes=[pltpu.VMEM((DIL_SLOTS, DIL_BLOCK, 2 * DIL_BLOCK), F32)],
        out_specs=[
            pl.BlockSpec((None, n_cls, L, DIL_W), lambda b, r: (b, r, 0, 0)),
            pl.BlockSpec((None, n_cls, L, LANES), lambda b, r: (b, r, 0, 0)),
        ],
        out_shape=[
            jax.ShapeDtypeStruct((B, d, L, DIL_W), BF16),
            jax.ShapeDtypeStruct((B, d, L, LANES), F32),
        ],
        compiler_params=_params(2),
        name=f"dil_attn_d{d}",
    )(qkv, qkv, qkv, jnp.asarray(bucket), rel_bias)


COMBINE_ROWS = 1024
COMBINE_CHUNK = 256


def _dil_combine_kernel(o1_ref, o4_ref, o16_ref, l1_ref, l4_ref, l16_ref, e_ref, out_ref,
                        s4_ref, s16_ref, sl4_ref, sl16_ref):
    T = o1_ref.shape[0]
    n_slab = DIL_W // LANES
    for d, o_ref, l_ref, s_ref, sl_ref in ((4, o4_ref, l4_ref, s4_ref, sl4_ref),
                                            (16, o16_ref, l16_ref, s16_ref, sl16_ref)):
        for r in range(d):
            rows = pl.ds(r, T // d, stride=d)
            sl_ref[rows, :] = l_ref[r]
            for c in range(n_slab):
                s_ref[c, rows, :] = o_ref[r, :, c * LANES:(c + 1) * LANES].astype(F32)

    def chunk(i, carry):
        rows = pl.ds(pl.multiple_of(i * COMBINE_CHUNK, COMBINE_CHUNK), COMBINE_CHUNK)
        lses = (l1_ref[rows, :], sl4_ref[rows, :], sl16_ref[rows, :])
        m = jnp.maximum(jnp.maximum(lses[0], lses[1]), lses[2])
        es = [jnp.exp(l - m) for l in lses]
        den = es[0] + es[1] + es[2]
        wide = []
        for e in es:
            w_hi, w_lo = _split_bf16(e / den)
            wide.append(_dot(w_hi, e_ref[...]) + _dot(w_lo, e_ref[...]))
        for c in range(n_slab):
            cs = slice(c * LANES, (c + 1) * LANES)
            acc = (wide[0][:, cs] * o1_ref[rows, cs].astype(F32)
                   + wide[1][:, cs] * s4_ref[c, rows, :]
                   + wide[2][:, cs] * s16_ref[c, rows, :])
            out_ref[rows, cs] = acc.astype(out_ref.dtype)
        return carry

    lax.fori_loop(0, T // COMBINE_CHUNK, chunk, 0)


def _dil_combine(o1, o4, o16, l1, l4, l16, expand):
    B, _, S, _ = o1.shape
    T = COMBINE_ROWS
    nt = S // T
    n_slab = DIL_W // LANES
    return pl.pallas_call(
        _dil_combine_kernel,
        grid=(B, nt),
        in_specs=[
            pl.BlockSpec((None, None, T, DIL_W), lambda b, t: (b, 0, t, 0)),
            pl.BlockSpec((None, 4, T // 4, DIL_W), lambda b, t: (b, 0, t, 0)),
            pl.BlockSpec((None, 16, T // 16, DIL_W), lambda b, t: (b, 0, t, 0)),
            pl.BlockSpec((None, None, T, LANES), lambda b, t: (b, 0, t, 0)),
            pl.BlockSpec((None, 4, T // 4, LANES), lambda b, t: (b, 0, t, 0)),
            pl.BlockSpec((None, 16, T // 16, LANES), lambda b, t: (b, 0, t, 0)),
            pl.BlockSpec((LANES, DIL_W), lambda b, t: (0, 0)),
        ],
        out_specs=pl.BlockSpec((T, DIL_W), lambda b, t: (b * nt + t, 0)),
        out_shape=jax.ShapeDtypeStruct((B * S, DIL_W), BF16),
        scratch_shapes=[
            pltpu.VMEM((n_slab, T, LANES), F32), pltpu.VMEM((n_slab, T, LANES), F32),
            pltpu.VMEM((T, LANES), F32), pltpu.VMEM((T, LANES), F32),
        ],
        compiler_params=_params(2),
        name="dil_combine",
    )(o1, o4, o16, l1, l4, l16, expand)


def _t5_bucket(dist):
    max_exact = REL_BUCKETS // 2
    d = np.maximum(dist, 1).astype(np.float64)
    large = max_exact + (np.log(d / max_exact) / np.log(REL_MAX_DIST / max_exact)
                         * (REL_BUCKETS - max_exact)).astype(np.int64)
    large = np.minimum(large, REL_BUCKETS - 1)
    return np.where(dist < max_exact, dist, large).astype(np.int32)


def _merge_kernel(og_ref, od_ref, ga_ref, gb_ref, x_ref, wa_ref, wb_ref, wo_ref, bg_ref, o_ref):
    a = _dot(og_ref[...], wa_ref[...])
    bm = _dot(od_ref[...], wb_ref[...])
    gate_a = jax.nn.sigmoid(ga_ref[...].astype(F32) + bg_ref[:, :D_MODEL])
    gate_b = jax.nn.sigmoid(gb_ref[...].astype(F32) + bg_ref[:, D_MODEL:])
    merged = (gate_a * a + gate_b * bm).astype(BF16)
    o_ref[...] = x_ref[...] + _dot(merged, wo_ref[...])


def _merge(o_gla, o_dil, z, gate_col_block, x2d, wa, wb, wo, b_gate, bm):
    M = x2d.shape[0]
    const = lambda i: (0, 0)
    resident = dict(pipeline_mode=pl.Buffered(1))
    return pl.pallas_call(
        _merge_kernel,
        grid=(M // bm,),
        in_specs=[
            pl.BlockSpec((bm, GLA_V_W), lambda i: (i, 0)),
            pl.BlockSpec((bm, DIL_W), lambda i: (i, 0)),
            pl.BlockSpec((bm, D_MODEL), lambda i: (i, gate_col_block)),
            pl.BlockSpec((bm, D_MODEL), lambda i: (i, gate_col_block + 1)),
            pl.BlockSpec((bm, D_MODEL), lambda i: (i, 0)),
            pl.BlockSpec((GLA_V_W, D_MODEL), const, **resident),
            pl.BlockSpec((DIL_W, D_MODEL), const, **resident),
            pl.BlockSpec((D_MODEL, D_MODEL), const, **resident),
            pl.BlockSpec((1, 2 * D_MODEL), const),
        ],
        out_specs=pl.BlockSpec((bm, D_MODEL), lambda i: (i, 0)),
        out_shape=jax.ShapeDtypeStruct((M, D_MODEL), F32),
        compiler_params=_params(1),
        name="merge_out_proj",
    )(o_gla, o_dil, z, z, x2d, wa, wb, wo, b_gate)


def _ffn_kernel(x_ref, nw_ref, wg_ref, wu_ref, wd_ref, fw_ref, o_ref, h_ref):
    f = pl.program_id(1)

    @pl.when(f == 0)
    def _():
        x = x_ref[...]
        ms = jnp.mean(x * x, axis=-1, keepdims=True)
        h_ref[...] = (x * lax.rsqrt(ms + RMS_EPS) * nw_ref[...]).astype(BF16)
        o_ref[...] = x

    h = h_ref[...]
    gate = _dot(h, wg_ref[...])
    up = _dot(h, wu_ref[...])
    act = (gate * jax.nn.sigmoid(gate) * up).astype(BF16)
    o_ref[...] += _dot(act, wd_ref[...])

    @pl.when(f == pl.num_programs(1) - 1)
    def _():
        y = o_ref[...]
        ms = jnp.mean(y * y, axis=-1, keepdims=True)
        o_ref[...] = y * lax.rsqrt(ms + RMS_EPS) * fw_ref[...]


def _ffn(x1, norm_w, w_in, w_out, final_w, bm, bf):
    M = x1.shape[0]
    nf = D_FF // bf
    return pl.pallas_call(
        _ffn_kernel,
        grid=(M // bm, nf),
        in_specs=[
            pl.BlockSpec((bm, D_MODEL), lambda i, f: (i, 0)),
            pl.BlockSpec((1, D_MODEL), lambda i, f: (0, 0)),
            pl.BlockSpec((D_MODEL, bf), lambda i, f: (0, f)),
            pl.BlockSpec((D_MODEL, bf), lambda i, f: (0, f + nf)),
            pl.BlockSpec((bf, D_MODEL), lambda i, f: (f, 0)),
            pl.BlockSpec((1, D_MODEL), lambda i, f: (0, 0)),
        ],
        out_specs=pl.BlockSpec((bm, D_MODEL), lambda i, f: (i, 0)),
        out_shape=jax.ShapeDtypeStruct((M, D_MODEL), F32),
        scratch_shapes=[pltpu.VMEM((bm, D_MODEL), BF16)],
        compiler_params=_params(2),
        name="swiglu_final_norm",
    )(x1, norm_w, w_in, w_in, w_out, final_w)


def kernel(x, attn_norm, w_in, w_gk_up, b_gk, gla_norm, b_gate, w_branch_gla, w_branch_dil,
           w_out, ffn_norm, w_ffn_in, w_ffn_out, rel_bias, final_norm):
    B, S, D = x.shape
    assert D == D_MODEL and S % (16 * DIL_BLOCK) == 0 and attn_norm.shape[0] == 1

    o_lr = 2 * GLA_QK_W + 2 * GLA_V_W
    o_dil = o_lr + GLA_LOWRANK
    o_gate = o_dil + 9 * DIL_W
    wi = w_in[0]
    n_nat = o_lr + 2 * D_MODEL + 3 * DIL_W + LANES
    bn_nat, bn_dil = n_nat // 7, DIL_W
    col_d4 = -(-n_nat // bn_dil) * bn_dil
    col_d16 = col_d4 + 3 * DIL_W
    gate_col_block = o_lr // D_MODEL
    d1_col = o_lr + 2 * D_MODEL
    lr_col = d1_col + 3 * DIL_W
    lr_col_block = lr_col // LANES
    w_all = _pack_w_in(
        jnp.swapaxes(w_in, 1, 2)[0],
        segments=((0, 0, o_lr), (o_lr, o_gate, 2 * D_MODEL), (d1_col, o_dil, 3 * DIL_W),
                  (lr_col, o_lr, GLA_LOWRANK), (col_d4, o_dil + 3 * DIL_W, 6 * DIL_W)),
        n_out=col_d16 + 3 * DIL_W)

    an = attn_norm[0][None, :]
    z_nat, h = _in_proj_norm(x, an, w_all, n_nat, bn_nat)
    z_d4 = _in_proj_dil(h, w_all, col_d4, 3 * DIL_W, 4, bn_dil)
    z_d16 = _in_proj_dil(h, w_all, col_d16, 3 * DIL_W, 16, bn_dil)
    z2d = z_nat.reshape(B * S, n_nat)
    z_nat = z_nat.reshape(B, 1, S, n_nat)

    wup = jnp.concatenate([w_gk_up[0], jnp.zeros((LANES - GLA_LOWRANK, GLA_QK_W), F32)], axis=0).astype(BF16)
    o_gla = _gla(z2d, lr_col_block, wup, b_gk[0][None, :], gla_norm[0][None, :], B, S)

    groups = []
    for gi, (zg, col_block) in enumerate(((z_nat, d1_col // DIL_W), (z_d4, 0), (z_d16, 0))):
        groups.append(_dil_attn(zg, col_block, rel_bias, gi))
    o_dil = _dil_combine(groups[0][0], groups[1][0], groups[2][0],
                         groups[0][1], groups[1][1], groups[2][1], jnp.asarray(_expand_matrix(), BF16))

    x2d = x.reshape(B * S, D)
    x1 = _merge(o_gla, o_dil, z2d, gate_col_block, x2d,
                w_branch_gla[0].astype(BF16), w_branch_dil[0].astype(BF16), w_out[0].astype(BF16),
                b_gate[0][None, :], 256)
    out = _ffn(x1, ffn_norm[0][None, :], w_ffn_in[0].astype(BF16), w_ffn_out[0].astype(BF16),
               final_norm[None, :], 512, 512)
    return out.reshape(B, S, D)


def _expand_matrix():
    e = np.zeros((LANES, DIL_W), np.float32)
    for h in range(DIL_SLOTS):
        e[h, h * DIL_HEAD_DIM:(h + 1) * DIL_HEAD_DIM] = 1.0
    return e
```

```python
import functools

import numpy as np
import jax
import jax.numpy as jnp
from jax import lax
from jax.experimental import pallas as pl
from jax.experimental.pallas import tpu as pltpu

F32 = jnp.float32
BF16 = jnp.bfloat16

D_MODEL = 2048
GLA_HEADS = 4
GLA_DK = 256
GLA_DV = 512
GLA_QK_W = GLA_HEADS * GLA_DK
GLA_V_W = GLA_HEADS * GLA_DV
GLA_LOWRANK = 16
GLA_GATE_NORM = 16.0
GLA_BLOCK = 128
GLA_STEP_ROWS = 256
DIL_DILATIONS = (1, 4, 16)
DIL_SLOTS = 8
DIL_HEAD_DIM = 128
DIL_W = DIL_SLOTS * DIL_HEAD_DIM
DIL_STEPS = 128
DIL_BLOCK = 128
REL_BUCKETS = 32
REL_MAX_DIST = 2048
D_FF = 5632
RMS_EPS = 1e-6
NEG_INF = -1e30

LANES = 128
SUBLANES = 8
PROJ_ROWS = 1024
PROJ_CHUNK = 256
VMEM_LIMIT = 56 * 1024 * 1024


def _params(n_axes, vmem=VMEM_LIMIT):
    return pltpu.CompilerParams(dimension_semantics=("arbitrary",) * n_axes, vmem_limit_bytes=vmem)


def _dot(a, b):
    return jnp.dot(a, b, preferred_element_type=F32)


def _dot_t(a, b, dims):
    return lax.dot_general(a, b, (dims, ((), ())), preferred_element_type=F32)


def _split_bf16(x):
    hi = x.astype(BF16)
    lo = (x - hi.astype(F32)).astype(BF16)
    return hi, lo


def _in_proj_norm_kernel(x_ref, g_ref, w_ref, w_lr_ref, o_ref, h_ref, lr_ref):
    @pl.when(pl.program_id(2) == 0)
    def _():
        x = x_ref[...]
        ms = jnp.mean(x * x, axis=-1, keepdims=True)
        h = (x * lax.rsqrt(ms + RMS_EPS) * g_ref[...]).astype(h_ref.dtype)
        h_ref[...] = h
        lr_ref[...] = _dot(h, w_lr_ref[...]).astype(lr_ref.dtype)

    o_ref[...] = _dot(h_ref[...], w_ref[...]).astype(o_ref.dtype)


def _in_proj_norm(x, norm_w, w, n_cols, bn, lr_col_block):
    B, S, D = x.shape
    rows = PROJ_ROWS
    return pl.pallas_call(
        _in_proj_norm_kernel,
        grid=(B, S // rows, n_cols // bn),
        in_specs=[
            pl.BlockSpec((None, rows, D), lambda b, t, j: (b, t, 0)),
            pl.BlockSpec((1, D), lambda b, t, j: (0, 0)),
            pl.BlockSpec((D, bn), lambda b, t, j: (0, j)),
            pl.BlockSpec((D, LANES), lambda b, t, j: (0, lr_col_block)),
        ],
        out_specs=[
            pl.BlockSpec((None, rows, bn), lambda b, t, j: (b, t, j)),
            pl.BlockSpec((None, rows, D), lambda b, t, j: (b, t, 0)),
            pl.BlockSpec((None, rows, LANES), lambda b, t, j: (b, t, 0)),
        ],
        out_shape=[jax.ShapeDtypeStruct((B, S, n_cols), BF16), jax.ShapeDtypeStruct((B, S, D), BF16),
                   jax.ShapeDtypeStruct((B, S, LANES), BF16)],
        compiler_params=_params(3),
        name="in_proj_norm",
    )(x, norm_w, w, w)


def _in_proj_dil_kernel(h_ref, w_ref, o_ref, *scratch, dilation):
    rows = h_ref.shape[0]
    bn = w_ref.shape[1]

    passes, rest = [], dilation
    while rest > 1:
        passes.append(min(rest, 4))
        rest //= passes[-1]

    def deinterleave(res, slab0):
        slabs = range(slab0, slab0 + res.shape[1] // LANES)
        for i, c in enumerate(slabs):
            scratch[0][c] = res[:, i * LANES:(i + 1) * LANES]
        src_ref, n_groups = scratch[0], 1
        for p, stride in enumerate(passes):
            group_rows = rows // n_groups
            last = p == len(passes) - 1
            for g in range(n_groups):
                for r in range(stride):
                    for c in slabs:
                        part = src_ref[c, pl.ds(g * group_rows + r, group_rows // stride, stride=stride), :]
                        if last:
                            o_ref[g + n_groups * r, :, c * LANES:(c + 1) * LANES] = part.astype(o_ref.dtype)
                        else:
                            dst = (g + n_groups * r) * (group_rows // stride)
                            scratch[1][c, dst:dst + group_rows // stride, :] = part
            if not last:
                src_ref, n_groups = scratch[1], n_groups * stride

    n_chunks = bn // PROJ_CHUNK
    res = _dot(h_ref[...], w_ref[:, :PROJ_CHUNK])
    for i in range(n_chunks):
        nxt = _dot(h_ref[...], w_ref[:, (i + 1) * PROJ_CHUNK:(i + 2) * PROJ_CHUNK]) if i + 1 < n_chunks else None
        deinterleave(res, i * PROJ_CHUNK // LANES)
        res = nxt


def _in_proj_dil(h, w, col0, n_cols, dilation, bn):
    B, S, D = h.shape
    rows = PROJ_ROWS
    assert col0 % bn == 0 and n_cols % bn == 0 and bn % PROJ_CHUNK == 0
    scratch = [pltpu.VMEM((bn // LANES, rows, LANES), F32)]
    if dilation > 4:
        scratch.append(pltpu.VMEM((bn // LANES, rows, LANES), F32))
    return pl.pallas_call(
        functools.partial(_in_proj_dil_kernel, dilation=dilation),
        grid=(B, S // rows, n_cols // bn),
        in_specs=[
            pl.BlockSpec((None, rows, D), lambda b, t, j: (b, t, 0)),
            pl.BlockSpec((D, bn), lambda b, t, j: (0, col0 // bn + j)),
        ],
        out_specs=pl.BlockSpec((None, dilation, rows // dilation, bn), lambda b, t, j: (b, 0, t, j)),
        out_shape=jax.ShapeDtypeStruct((B, dilation, S // dilation, n_cols), BF16),
        scratch_shapes=scratch,
        compiler_params=_params(3),
        name=f"in_proj_d{dilation}",
    )(h, w)


PACK_COLS = 512


def _pack_w_in_kernel(src_ref, valid_ref, wt_ref, o_ref):
    x = wt_ref[...]
    row = lax.broadcasted_iota(jnp.int32, x.shape, 0)
    x = jnp.where(row < valid_ref[pl.program_id(0)], x, 0.0)
    o_ref[...] = x.T.astype(o_ref.dtype)


def _pack_w_in(wt, segments, n_out):
    N, D = wt.shape
    n_blocks = n_out // PACK_COLS
    src = np.zeros((n_blocks,), np.int32)
    valid = np.zeros((n_blocks,), np.int32)
    for dst, s, width in segments:
        assert dst % PACK_COLS == 0
        for i in range(dst // PACK_COLS, -(-(dst + width) // PACK_COLS)):
            start = s + i * PACK_COLS - dst
            assert start % SUBLANES == 0 and start + PACK_COLS <= N
            src[i] = start // SUBLANES
            valid[i] = min(PACK_COLS, dst + width - i * PACK_COLS)
    return pl.pallas_call(
        _pack_w_in_kernel,
        grid_spec=pltpu.PrefetchScalarGridSpec(
            num_scalar_prefetch=2,
            grid=(n_blocks,),
            in_specs=[pl.BlockSpec((pl.Element(PACK_COLS), pl.Element(D)),
                                   lambda i, src, valid: (src[i] * SUBLANES, 0))],
            out_specs=pl.BlockSpec((D, PACK_COLS), lambda i, src, valid: (0, i)),
        ),
        out_shape=jax.ShapeDtypeStruct((D, n_out), BF16),
        compiler_params=_params(1),
        name="pack_w_in",
    )(jnp.asarray(src), jnp.asarray(valid), wt)


def _gla_kernel(q_ref, k_ref, v_ref, g_ref, lr_ref, wup_ref, bgk_ref, nw_ref, o_ref, s_ref):
    C = GLA_BLOCK

    @pl.when(pl.program_id(1) == 0)
    def _():
        s_ref[...] = jnp.zeros_like(s_ref)

    row = lax.broadcasted_iota(jnp.int32, (C, C), 0)
    col = lax.broadcasted_iota(jnp.int32, (C, C), 1)
    causal = row >= col
    tri = jnp.where(causal, 1.0, 0.0).astype(BF16)
    ones = jnp.ones((C, LANES), BF16)
    scale = GLA_DK ** -0.5
    n_blk = q_ref.shape[0] // C
    heads = range(GLA_HEADS)
    ks = [slice(h * GLA_DK, (h + 1) * GLA_DK) for h in heads]
    vs = [slice(h * GLA_DV, (h + 1) * GLA_DV) for h in heads]
    units = [(blk, h) for blk in range(n_blk) for h in heads]
    rows = {u: slice(u[0] * C, (u[0] + 1) * C) for u in units}
    gks, b_cum, tot, att, q_dec, k_dec = {}, {}, {}, {}, {}, {}
    o_intra, q_in, k_end, decay = {}, {}, {}, {}
    for u in units:
        pre = _dot(lr_ref[rows[u], :], wup_ref[:, ks[u[1]]]) + bgk_ref[:, ks[u[1]]]
        log_sig = jnp.minimum(pre, 0.0) - jnp.log(1.0 + jnp.exp(-jnp.abs(pre)))
        gks[u] = _split_bf16(log_sig * (1.0 / GLA_GATE_NORM))
    for u in units:
        gk_hi, gk_lo = gks[u]
        b_cum[u] = _dot(tri, gk_hi) + _dot(tri, gk_lo)
        tot[u] = _dot_t(gk_hi, ones, ((0,), (0,))) + _dot_t(gk_lo, ones, ((0,), (0,)))
    for u in units:
        b = b_cum[u]
        b_mid = b[C // 2 - 1:C // 2, :]
        q_dec[u] = q_ref[rows[u], ks[u[1]]].astype(F32) * jnp.exp(b - b_mid)
        k_dec[u] = k_ref[rows[u], ks[u[1]]].astype(F32) * jnp.exp(b_mid - b)
        att[u] = _dot_t((q_dec[u] * scale).astype(BF16), k_dec[u].astype(BF16), ((1,), (1,)))
    for u in units:
        b = b_cum[u]
        b_last = b[C - 1:C, :]
        b_mid = b[C // 2 - 1:C // 2, :]
        o_intra[u] = _dot(jnp.where(causal, att[u], 0.0).astype(BF16), v_ref[rows[u], vs[u[1]]])
        q_in[u] = (q_dec[u] * (jnp.exp(b_mid) * scale)).astype(BF16)
        k_end[u] = (k_dec[u] * jnp.exp(b_last - b_mid)).astype(BF16)
        decay[u] = jnp.concatenate([jnp.exp(tot[u])] * (GLA_DV // LANES), axis=1)
    for blk in range(n_blk):
        rs = slice(blk * C, (blk + 1) * C)
        os = []
        for h in heads:
            s_old = s_ref[h]
            os.append(o_intra[blk, h] + _dot(q_in[blk, h], s_old.astype(BF16)))
            s_ref[h] = decay[blk, h] * s_old + _dot_t(k_end[blk, h], v_ref[rs, vs[h]], ((0,), (0,)))
        ys = []
        for h, o in zip(heads, os):
            ms = jnp.mean(o * o, axis=-1, keepdims=True)
            g = g_ref[rs, vs[h]].astype(F32)
            y = (o * lax.rsqrt(ms + RMS_EPS) * nw_ref[...]) * (g * jax.nn.sigmoid(g))
            ys.append(y.astype(o_ref.dtype))
        o_ref[rs, :] = jnp.concatenate(ys, axis=1)


def _gla(z, lr, wup, bgk, norm_w, B, S):
    C = GLA_STEP_ROWS
    nc = S // C
    return pl.pallas_call(
        _gla_kernel,
        grid=(B, nc),
        in_specs=[
            pl.BlockSpec((C, GLA_QK_W), lambda b, c: (b * nc + c, 0)),
            pl.BlockSpec((C, GLA_QK_W), lambda b, c: (b * nc + c, 1)),
            pl.BlockSpec((C, GLA_V_W), lambda b, c: (b * nc + c, 1)),
            pl.BlockSpec((C, GLA_V_W), lambda b, c: (b * nc + c, 2)),
            pl.BlockSpec((C, LANES), lambda b, c: (b * nc + c, 0)),
            pl.BlockSpec((LANES, GLA_QK_W), lambda b, c: (0, 0)),
            pl.BlockSpec((1, GLA_QK_W), lambda b, c: (0, 0)),
            pl.BlockSpec((1, GLA_DV), lambda b, c: (0, 0)),
        ],
        out_specs=pl.BlockSpec((C, GLA_V_W), lambda b, c: (b * nc + c, 0)),
        out_shape=jax.ShapeDtypeStruct((B * S, GLA_V_W), BF16),
        scratch_shapes=[pltpu.VMEM((GLA_HEADS, GLA_DK, GLA_DV), F32)],
        compiler_params=_params(2),
        name="gla",
    )(z, z, z, z, lr, wup, bgk, norm_w)


def _dil_attn_kernel(q_ref, k_ref, v_ref, bucket_ref, table_ref, o_ref, lse_ref, bias_ref, *,
                     n_cls, n_blk, buckets, head0):
    scale = DIL_HEAD_DIM ** -0.5
    lane = lax.broadcasted_iota(jnp.int32, (DIL_BLOCK, LANES), 1)

    @pl.when((pl.program_id(0) == 0) & (pl.program_id(1) == 0))
    def _():
        bucket = bucket_ref[...]
        for h in range(DIL_SLOTS):
            bias = jnp.full(bucket.shape, NEG_INF, F32)
            for k in buckets:
                bias = jnp.where(bucket == k, table_ref[k, head0 + h], bias)
            bias_ref[h] = bias

    def block(c, start, first):
        lse_tile = jnp.zeros((DIL_BLOCK, LANES), F32)
        rows_q = pl.ds(start, DIL_BLOCK)
        rows_k = rows_q if first else pl.ds(start - DIL_BLOCK, 2 * DIL_BLOCK)
        n_keys = DIL_BLOCK if first else 2 * DIL_BLOCK
        ones = jnp.ones((n_keys, DIL_HEAD_DIM), BF16)
        cols = [slice(h * DIL_HEAD_DIM, (h + 1) * DIL_HEAD_DIM) for h in range(DIL_SLOTS)]
        scores = [
            _dot_t(q_ref[c, rows_q, cq], k_ref[c, rows_k, cq], ((1,), (1,))) * scale
            + (bias_ref[h, :, DIL_BLOCK:] if first else bias_ref[h])
            for h, cq in enumerate(cols)]
        maxes = [jnp.max(s, axis=-1, keepdims=True) for s in scores]
        probs = [jnp.exp(s - m).astype(BF16) for s, m in zip(scores, maxes)]
        num_ls = [_dot(p, jnp.concatenate([v_ref[c, rows_k, cq], ones], axis=1)) for p, cq in zip(probs, cols)]
        outs = []
        for h, (num_l, m) in enumerate(zip(num_ls, maxes)):
            l = num_l[:, DIL_HEAD_DIM:]
            outs.append((num_l[:, :DIL_HEAD_DIM] / l).astype(o_ref.dtype))
            lse_tile = jnp.where(lane == h, m + jnp.log(l), lse_tile)
        o_ref[c, rows_q, :] = jnp.concatenate(outs, axis=1)
        lse_ref[c, rows_q, :] = lse_tile

    def one_class(c, carry):
        block(c, 0, True)
        if n_blk > 1:
            def body(n, inner):
                block(c, pl.multiple_of(n * DIL_BLOCK, DIL_BLOCK), False)
                return inner
            lax.fori_loop(1, n_blk, body, 0)
        return carry

    if n_cls == 1:
        one_class(0, 0)
    else:
        lax.fori_loop(0, n_cls, one_class, 0)


def _band_buckets(dilation):
    a_idx = np.arange(DIL_BLOCK)[:, None]
    c_idx = np.arange(2 * DIL_BLOCK)[None, :]
    steps = DIL_BLOCK + a_idx - c_idx
    in_band = (steps >= 0) & (steps <= DIL_STEPS)
    bucket = _t5_bucket(np.clip(steps, 0, None) * dilation)
    return np.where(in_band, bucket, -1).astype(np.int32)


def _dil_attn(qkv, col_block, rel_bias, group):
    B, d, L, _ = qkv.shape
    n_cls = d
    bucket = _band_buckets(d)
    buckets = tuple(int(k) for k in np.unique(bucket) if k >= 0)

    def part(p):
        return pl.BlockSpec((None, n_cls, L, DIL_W), lambda b, r: (b, r, 0, col_block + p))

    return pl.pallas_call(
        functools.partial(_dil_attn_kernel, n_cls=n_cls, n_blk=L // DIL_BLOCK, buckets=buckets,
                          head0=group * DIL_SLOTS),
        grid=(B, d // n_cls),
        in_specs=[
            part(0), part(1), part(2),
            pl.BlockSpec((DIL_BLOCK, 2 * DIL_BLOCK), lambda b, r: (0, 0)),
            pl.BlockSpec(memory_space=pltpu.SMEM),
        ],
        scratch_shapes=[pltpu.VMEM((DIL_SLOTS, DIL_BLOCK, 2 * DIL_BLOCK), F32)],
        out_specs=[
            pl.BlockSpec((None, n_cls, L, DIL_W), lambda b, r: (b, r, 0, 0)),
            pl.BlockSpec((None, n_cls, L, LANES), lambda b, r: (b, r, 0, 0)),
        ],
        out_shape=[
            jax.ShapeDtypeStruct((B, d, L, DIL_W), BF16),
            jax.ShapeDtypeStruct((B, d, L, LANES), F32),
        ],
        compiler_params=_params(2),
        name=f"dil_attn_d{d}",
    )(qkv, qkv, qkv, jnp.asarray(bucket), rel_bias)


def _t5_bucket(dist):
    max_exact = REL_BUCKETS // 2
    d = np.maximum(dist, 1).astype(np.float64)
    large = max_exact + (np.log(d / max_exact) / np.log(REL_MAX_DIST / max_exact)
                         * (REL_BUCKETS - max_exact)).astype(np.int64)
    large = np.minimum(large, REL_BUCKETS - 1)
    return np.where(dist < max_exact, dist, large).astype(np.int32)


def _merge_kernel(og_ref, o1_ref, o4_ref, o16_ref, l1_ref, l4_ref, l16_ref, ga_ref, gb_ref, x_ref,
                  wa_ref, wb_ref, wo_ref, bg_ref, o_ref, s4_ref, s16_ref, sl4_ref, sl16_ref):
    T = og_ref.shape[0]
    a = _dot(og_ref[...], wa_ref[...])
    for d, grp_ref, lse_ref, s_ref, sl_ref in ((4, o4_ref, l4_ref, s4_ref, sl4_ref),
                                                (16, o16_ref, l16_ref, s16_ref, sl16_ref)):
        for r in range(d):
            rows = pl.ds(r, T // d, stride=d)
            sl_ref[rows, :] = lse_ref[r]
            for c in range(DIL_SLOTS):
                s_ref[c, rows, :] = grp_ref[r, :, c * LANES:(c + 1) * LANES].astype(F32)
    lses = (l1_ref[...], sl4_ref[...], sl16_ref[...])
    m = jnp.maximum(jnp.maximum(lses[0], lses[1]), lses[2])
    es = [jnp.exp(l - m) for l in lses]
    inv = 1.0 / (es[0] + es[1] + es[2])
    ws = [e * inv for e in es]
    heads = []
    for c in range(DIL_SLOTS):
        cs = slice(c * DIL_HEAD_DIM, (c + 1) * DIL_HEAD_DIM)
        mix = (ws[0][:, c:c + 1] * o1_ref[:, cs].astype(F32)
               + ws[1][:, c:c + 1] * s4_ref[c]
               + ws[2][:, c:c + 1] * s16_ref[c])
        heads.append(mix.astype(BF16))
    bm = _dot(jnp.concatenate(heads, axis=1), wb_ref[...])
    gate_a = jax.nn.sigmoid(ga_ref[...].astype(F32) + bg_ref[:, :D_MODEL])
    gate_b = jax.nn.sigmoid(gb_ref[...].astype(F32) + bg_ref[:, D_MODEL:])
    merged = (gate_a * a + gate_b * bm).astype(BF16)
    o_ref[...] = x_ref[...] + _dot(merged, wo_ref[...])


def _merge(o_gla, groups, z, gate_col_block, x, wa, wb, wo, b_gate, bm):
    B, S, D = x.shape
    nt = S // bm
    (o1, l1), (o4, l4), (o16, l16) = groups
    const = lambda i: (0, 0)
    resident = dict(pipeline_mode=pl.Buffered(1))

    def natural(width):
        return pl.BlockSpec((None, None, bm, width), lambda i: (i // nt, 0, i % nt, 0))

    def class_major(d, width):
        return pl.BlockSpec((None, d, bm // d, width), lambda i: (i // nt, 0, i % nt, 0))

    out = pl.pallas_call(
        _merge_kernel,
        grid=(B * nt,),
        in_specs=[
            pl.BlockSpec((bm, GLA_V_W), lambda i: (i, 0)),
            natural(DIL_W), class_major(4, DIL_W), class_major(16, DIL_W),
            natural(LANES), class_major(4, LANES), class_major(16, LANES),
            pl.BlockSpec((bm, D_MODEL), lambda i: (i, gate_col_block)),
            pl.BlockSpec((bm, D_MODEL), lambda i: (i, gate_col_block + 1)),
            pl.BlockSpec((bm, D_MODEL), lambda i: (i, 0)),
            pl.BlockSpec((GLA_V_W, D_MODEL), const, **resident),
            pl.BlockSpec((DIL_W, D_MODEL), const, **resident),
            pl.BlockSpec((D_MODEL, D_MODEL), const, **resident),
            pl.BlockSpec((1, 2 * D_MODEL), const),
        ],
        out_specs=pl.BlockSpec((bm, D_MODEL), lambda i: (i, 0)),
        out_shape=jax.ShapeDtypeStruct((B * S, D_MODEL), F32),
        scratch_shapes=[
            pltpu.VMEM((DIL_SLOTS, bm, LANES), F32), pltpu.VMEM((DIL_SLOTS, bm, LANES), F32),
            pltpu.VMEM((bm, LANES), F32), pltpu.VMEM((bm, LANES), F32),
        ],
        compiler_params=_params(1),
        name="merge_out_proj",
    )(o_gla, o1, o4, o16, l1, l4, l16, z, z, x.reshape(B * S, D), wa, wb, wo, b_gate)
    return out


def _ffn_kernel(x_ref, nw_ref, wg_ref, wu_ref, wd_ref, fw_ref, o_ref, h_ref):
    f = pl.program_id(1)

    @pl.when(f == 0)
    def _():
        x = x_ref[...]
        ms = jnp.mean(x * x, axis=-1, keepdims=True)
        h_ref[...] = (x * lax.rsqrt(ms + RMS_EPS) * nw_ref[...]).astype(BF16)
        o_ref[...] = x

    h = h_ref[...]
    gate = _dot(h, wg_ref[...])
    up = _dot(h, wu_ref[...])
    act = (gate * jax.nn.sigmoid(gate) * up).astype(BF16)
    for c in range(0, D_MODEL, FFN_OUT_CHUNK):
        o_ref[:, c:c + FFN_OUT_CHUNK] += _dot(act, wd_ref[:, c:c + FFN_OUT_CHUNK])

    @pl.when(f == pl.num_programs(1) - 1)
    def _():
        y = o_ref[...]
        ms = jnp.mean(y * y, axis=-1, keepdims=True)
        o_ref[...] = y * lax.rsqrt(ms + RMS_EPS) * fw_ref[...]


FFN_OUT_CHUNK = 512


def _ffn(x1, norm_w, w_in, w_out, final_w, bm, bf):
    M = x1.shape[0]
    nf = D_FF // bf
    return pl.pallas_call(
        _ffn_kernel,
        grid=(M // bm, nf),
        in_specs=[
            pl.BlockSpec((bm, D_MODEL), lambda i, f: (i, 0)),
            pl.BlockSpec((1, D_MODEL), lambda i, f: (0, 0)),
            pl.BlockSpec((D_MODEL, bf), lambda i, f: (0, f)),
            pl.BlockSpec((D_MODEL, bf), lambda i, f: (0, f + nf)),
            pl.BlockSpec((bf, D_MODEL), lambda i, f: (f, 0)),
            pl.BlockSpec((1, D_MODEL), lambda i, f: (0, 0)),
        ],
        out_specs=pl.BlockSpec((bm, D_MODEL), lambda i, f: (i, 0)),
        out_shape=jax.ShapeDtypeStruct((M, D_MODEL), F32),
        scratch_shapes=[pltpu.VMEM((bm, D_MODEL), BF16)],
        compiler_params=_params(2),
        name="swiglu_final_norm",
    )(x1, norm_w, w_in, w_in, w_out, final_w)


def kernel(x, attn_norm, w_in, w_gk_up, b_gk, gla_norm, b_gate, w_branch_gla, w_branch_dil,
           w_out, ffn_norm, w_ffn_in, w_ffn_out, rel_bias, final_norm):
    B, S, D = x.shape
    assert D == D_MODEL and S % (16 * DIL_BLOCK) == 0 and attn_norm.shape[0] == 1

    o_lr = 2 * GLA_QK_W + 2 * GLA_V_W
    o_dil = o_lr + GLA_LOWRANK
    o_gate = o_dil + 9 * DIL_W
    n_nat = o_lr + 2 * D_MODEL + 3 * DIL_W
    bn_nat = bn_dil = DIL_W
    lr_col = n_nat
    col_d4 = n_nat + bn_dil
    col_d16 = col_d4 + 3 * DIL_W
    gate_col_block = o_lr // D_MODEL
    d1_col = o_lr + 2 * D_MODEL
    w_all = _pack_w_in(
        jnp.swapaxes(w_in, 1, 2)[0],
        segments=((0, 0, o_lr), (o_lr, o_gate, 2 * D_MODEL), (d1_col, o_dil, 3 * DIL_W),
                  (lr_col, o_lr, GLA_LOWRANK), (col_d4, o_dil + 3 * DIL_W, 6 * DIL_W)),
        n_out=col_d16 + 3 * DIL_W)

    an = attn_norm[0][None, :]
    z_nat, h, lr = _in_proj_norm(x, an, w_all, n_nat, bn_nat, lr_col // LANES)
    z_d4 = _in_proj_dil(h, w_all, col_d4, 3 * DIL_W, 4, bn_dil)
    z_d16 = _in_proj_dil(h, w_all, col_d16, 3 * DIL_W, 16, bn_dil)
    z2d = z_nat.reshape(B * S, n_nat)
    z_nat = z_nat.reshape(B, 1, S, n_nat)

    wup = jnp.concatenate([w_gk_up[0], jnp.zeros((LANES - GLA_LOWRANK, GLA_QK_W), F32)], axis=0).astype(BF16)
    o_gla = _gla(z2d, lr.reshape(B * S, LANES), wup, b_gk[0][None, :], gla_norm[0][None, :], B, S)

    groups = []
    for gi, (zg, col_block) in enumerate(((z_nat, d1_col // DIL_W), (z_d4, 0), (z_d16, 0))):
        groups.append(_dil_attn(zg, col_block, rel_bias, gi))
    x1 = _merge(o_gla, groups, z2d, gate_col_block, x,
                w_branch_gla[0].astype(BF16), w_branch_dil[0].astype(BF16), w_out[0].astype(BF16),
                b_gate[0][None, :], 256)
    out = _ffn(x1, ffn_norm[0][None, :], w_ffn_in[0].astype(BF16), w_ffn_out[0].astype(BF16),
               final_norm[None, :], 1024, 512)
    return out.reshape(B, S, D)
```

```python
import functools

import numpy as np
import jax
import jax.numpy as jnp
from jax import lax
from jax.experimental import pallas as pl
from jax.experimental.pallas import tpu as pltpu

F32 = jnp.float32
BF16 = jnp.bfloat16

D_MODEL = 2048
GLA_HEADS = 4
GLA_DK = 256
GLA_DV = 512
GLA_QK_W = GLA_HEADS * GLA_DK
GLA_V_W = GLA_HEADS * GLA_DV
GLA_LOWRANK = 16
GLA_GATE_NORM = 16.0
GLA_BLOCK = 128
GLA_STEP_ROWS = 256
DIL_DILATIONS = (1, 4, 16)
DIL_SLOTS = 8
DIL_HEAD_DIM = 128
DIL_W = DIL_SLOTS * DIL_HEAD_DIM
DIL_STEPS = 128
DIL_BLOCK = 128
REL_BUCKETS = 32
REL_MAX_DIST = 2048
D_FF = 5632
RMS_EPS = 1e-6
NEG_INF = -1e30

LANES = 128
SUBLANES = 8
BF16_ROWS = 16
PROJ_ROWS = 1024
PROJ_CHUNK = 256
PROJ_NORM_CHUNK = 256
VMEM_LIMIT = 56 * 1024 * 1024


def _params(n_axes, vmem=VMEM_LIMIT):
    return pltpu.CompilerParams(dimension_semantics=("arbitrary",) * n_axes, vmem_limit_bytes=vmem)


def _dot(a, b):
    return jnp.dot(a, b, preferred_element_type=F32)


def _dot_t(a, b, dims):
    return lax.dot_general(a, b, (dims, ((), ())), preferred_element_type=F32)


def _split_bf16(x):
    hi = x.astype(BF16)
    lo = (x - hi.astype(F32)).astype(BF16)
    return hi, lo


def _in_proj_norm_kernel(x_ref, g_ref, w_ref, w_lr_ref, o_ref, h_ref, lr_ref):
    @pl.when(pl.program_id(2) == 0)
    def _():
        x = x_ref[...]
        ms = jnp.mean(x * x, axis=-1, keepdims=True)
        h = (x * lax.rsqrt(ms + RMS_EPS) * g_ref[...]).astype(h_ref.dtype)
        h_ref[...] = h
        lr_ref[...] = _dot(h, w_lr_ref[...]).astype(lr_ref.dtype)

    for c in range(0, w_ref.shape[1], PROJ_NORM_CHUNK):
        o_ref[:, c:c + PROJ_NORM_CHUNK] = _dot(h_ref[...], w_ref[:, c:c + PROJ_NORM_CHUNK]).astype(o_ref.dtype)


def _in_proj_norm(x, norm_w, w, n_cols, bn, lr_col_block, rows):
    B, S, D = x.shape
    return pl.pallas_call(
        _in_proj_norm_kernel,
        grid=(B, S // rows, n_cols // bn),
        in_specs=[
            pl.BlockSpec((None, rows, D), lambda b, t, j: (b, t, 0)),
            pl.BlockSpec((1, D), lambda b, t, j: (0, 0)),
            pl.BlockSpec((D, bn), lambda b, t, j: (0, j)),
            pl.BlockSpec((D, LANES), lambda b, t, j: (0, lr_col_block)),
        ],
        out_specs=[
            pl.BlockSpec((None, rows, bn), lambda b, t, j: (b, t, j)),
            pl.BlockSpec((None, rows, D), lambda b, t, j: (b, t, 0)),
            pl.BlockSpec((None, rows, LANES), lambda b, t, j: (b, t, 0)),
        ],
        out_shape=[jax.ShapeDtypeStruct((B, S, n_cols), BF16), jax.ShapeDtypeStruct((B, S, D), BF16),
                   jax.ShapeDtypeStruct((B, S, LANES), BF16)],
        compiler_params=_params(3),
        name="in_proj_norm",
    )(x, norm_w, w, w)


def _in_proj_dil_kernel(h_ref, w_ref, cast_ref, o_ref, cast_out_ref, *scratch, dilation):
    rows = h_ref.shape[0]
    bn = w_ref.shape[1]

    @pl.when(pl.program_id(2) == 0)
    def _():
        cast_out_ref[...] = cast_ref[...].astype(cast_out_ref.dtype)

    passes, rest = [], dilation
    while rest > 1:
        passes.append(min(rest, 4))
        rest //= passes[-1]

    def deinterleave(res, slab0):
        slabs = range(slab0, slab0 + res.shape[1] // LANES)
        for i, c in enumerate(slabs):
            scratch[0][c] = res[:, i * LANES:(i + 1) * LANES]
        src_ref, n_groups = scratch[0], 1
        for p, stride in enumerate(passes):
            group_rows = rows // n_groups
            last = p == len(passes) - 1
            for g in range(n_groups):
                for r in range(stride):
                    for c in slabs:
                        part = src_ref[c, pl.ds(g * group_rows + r, group_rows // stride, stride=stride), :]
                        if last:
                            o_ref[g + n_groups * r, :, c * LANES:(c + 1) * LANES] = part.astype(o_ref.dtype)
                        else:
                            dst = (g + n_groups * r) * (group_rows // stride)
                            scratch[1][c, dst:dst + group_rows // stride, :] = part
            if not last:
                src_ref, n_groups = scratch[1], n_groups * stride

    n_chunks = bn // PROJ_CHUNK
    res = _dot(h_ref[...], w_ref[:, :PROJ_CHUNK])
    for i in range(n_chunks):
        nxt = _dot(h_ref[...], w_ref[:, (i + 1) * PROJ_CHUNK:(i + 2) * PROJ_CHUNK]) if i + 1 < n_chunks else None
        deinterleave(res, i * PROJ_CHUNK // LANES)
        res = nxt


def _in_proj_dil(h, w, col0, n_cols, dilation, bn, to_cast):
    B, S, D = h.shape
    rows = PROJ_ROWS
    nt = S // rows
    assert col0 % bn == 0 and n_cols % bn == 0 and bn % PROJ_CHUNK == 0
    cast_rows = to_cast.shape[0] // (B * nt)
    assert cast_rows * B * nt == to_cast.shape[0] and cast_rows % BF16_ROWS == 0
    cast_spec = pl.BlockSpec((cast_rows, to_cast.shape[1]), lambda b, t, j: (b * nt + t, 0))
    scratch = [pltpu.VMEM((bn // LANES, rows, LANES), F32)]
    if dilation > 4:
        scratch.append(pltpu.VMEM((bn // LANES, rows, LANES), F32))
    return pl.pallas_call(
        functools.partial(_in_proj_dil_kernel, dilation=dilation),
        grid=(B, nt, n_cols // bn),
        in_specs=[
            pl.BlockSpec((None, rows, D), lambda b, t, j: (b, t, 0)),
            pl.BlockSpec((D, bn), lambda b, t, j: (0, col0 // bn + j)),
            cast_spec,
        ],
        out_specs=[
            pl.BlockSpec((None, dilation, rows // dilation, bn), lambda b, t, j: (b, 0, t, j)),
            cast_spec,
        ],
        out_shape=[jax.ShapeDtypeStruct((B, dilation, S // dilation, n_cols), BF16),
                   jax.ShapeDtypeStruct(to_cast.shape, BF16)],
        scratch_shapes=scratch,
        compiler_params=_params(3),
        name=f"in_proj_d{dilation}",
    )(h, w, to_cast)


PACK_COLS = 512


def _pack_w_in_kernel(src_ref, valid_ref, wt_ref, o_ref):
    x = wt_ref[...]
    row = lax.broadcasted_iota(jnp.int32, x.shape, 0)
    x = jnp.where(row < valid_ref[pl.program_id(0)], x, 0.0)
    o_ref[...] = x.T.astype(o_ref.dtype)


def _pack_w_in(wt, segments, n_out):
    N, D = wt.shape
    n_blocks = n_out // PACK_COLS
    src = np.zeros((n_blocks,), np.int32)
    valid = np.zeros((n_blocks,), np.int32)
    for dst, s, width in segments:
        assert dst % PACK_COLS == 0
        for i in range(dst // PACK_COLS, -(-(dst + width) // PACK_COLS)):
            start = s + i * PACK_COLS - dst
            assert start % SUBLANES == 0 and start + PACK_COLS <= N
            src[i] = start // SUBLANES
            valid[i] = min(PACK_COLS, dst + width - i * PACK_COLS)
    return pl.pallas_call(
        _pack_w_in_kernel,
        grid_spec=pltpu.PrefetchScalarGridSpec(
            num_scalar_prefetch=2,
            grid=(n_blocks,),
            in_specs=[pl.BlockSpec((pl.Element(PACK_COLS), pl.Element(D)),
                                   lambda i, src, valid: (src[i] * SUBLANES, 0))],
            out_specs=pl.BlockSpec((D, PACK_COLS), lambda i, src, valid: (0, i)),
        ),
        out_shape=jax.ShapeDtypeStruct((D, n_out), BF16),
        compiler_params=_params(1),
        name="pack_w_in",
    )(jnp.asarray(src), jnp.asarray(valid), wt)


def _gla_kernel(q_ref, k_ref, v_ref, g_ref, lr_ref, wup_ref, bgk_ref, nw_ref, o_ref, s_ref):
    C = GLA_BLOCK

    @pl.when(pl.program_id(1) == 0)
    def _():
        s_ref[...] = jnp.zeros_like(s_ref)

    row = lax.broadcasted_iota(jnp.int32, (C, C), 0)
    col = lax.broadcasted_iota(jnp.int32, (C, C), 1)
    causal = row >= col
    tri = jnp.where(causal, 1.0, 0.0).astype(BF16)
    ones = jnp.ones((C, LANES), BF16)
    scale = GLA_DK ** -0.5
    n_blk = q_ref.shape[0] // C
    heads = range(GLA_HEADS)
    ks = [slice(h * GLA_DK, (h + 1) * GLA_DK) for h in heads]
    vs = [slice(h * GLA_DV, (h + 1) * GLA_DV) for h in heads]
    units = [(blk, h) for blk in range(n_blk) for h in heads]
    rows = {u: slice(u[0] * C, (u[0] + 1) * C) for u in units}
    gks, b_cum, tot, att, q_dec, k_dec = {}, {}, {}, {}, {}, {}
    o_intra, q_in, k_end, decay = {}, {}, {}, {}
    for u in units:
        pre = _dot(lr_ref[rows[u], :], wup_ref[:, ks[u[1]]]) + bgk_ref[:, ks[u[1]]]
        log_sig = jnp.minimum(pre, 0.0) - jnp.log(1.0 + jnp.exp(-jnp.abs(pre)))
        gks[u] = _split_bf16(log_sig * (1.0 / GLA_GATE_NORM))
    for u in units:
        gk_hi, gk_lo = gks[u]
        b_cum[u] = _dot(tri, gk_hi) + _dot(tri, gk_lo)
        tot[u] = _dot_t(gk_hi, ones, ((0,), (0,))) + _dot_t(gk_lo, ones, ((0,), (0,)))
    for u in units:
        b = b_cum[u]
        b_mid = b[C // 2 - 1:C // 2, :]
        q_dec[u] = q_ref[rows[u], ks[u[1]]].astype(F32) * jnp.exp(b - b_mid)
        k_dec[u] = k_ref[rows[u], ks[u[1]]].astype(F32) * jnp.exp(b_mid - b)
        att[u] = _dot_t((q_dec[u] * scale).astype(BF16), k_dec[u].astype(BF16), ((1,), (1,)))
    for u in units:
        b = b_cum[u]
        b_last = b[C - 1:C, :]
        b_mid = b[C // 2 - 1:C // 2, :]
        o_intra[u] = _dot(jnp.where(causal, att[u], 0.0).astype(BF16), v_ref[rows[u], vs[u[1]]])
        q_in[u] = (q_dec[u] * (jnp.exp(b_mid) * scale)).astype(BF16)
        k_end[u] = (k_dec[u] * jnp.exp(b_last - b_mid)).astype(BF16)
        decay[u] = jnp.concatenate([jnp.exp(tot[u])] * (GLA_DV // LANES), axis=1)
    for blk in range(n_blk):
        rs = slice(blk * C, (blk + 1) * C)
        os = []
        for h in heads:
            s_old = s_ref[h]
            os.append(o_intra[blk, h] + _dot(q_in[blk, h], s_old.astype(BF16)))
            s_ref[h] = decay[blk, h] * s_old + _dot_t(k_end[blk, h], v_ref[rs, vs[h]], ((0,), (0,)))
        ys = []
        for h, o in zip(heads, os):
            ms = jnp.mean(o * o, axis=-1, keepdims=True)
            g = g_ref[rs, vs[h]].astype(F32)
            y = (o * lax.rsqrt(ms + RMS_EPS) * nw_ref[...]) * (g * jax.nn.sigmoid(g))
            ys.append(y.astype(o_ref.dtype))
        o_ref[rs, :] = jnp.concatenate(ys, axis=1)


def _gla(z, lr, wup, bgk, norm_w, B, S):
    C = GLA_STEP_ROWS
    nc = S // C
    return pl.pallas_call(
        _gla_kernel,
        grid=(B, nc),
        in_specs=[
            pl.BlockSpec((C, GLA_QK_W), lambda b, c: (b * nc + c, 0)),
            pl.BlockSpec((C, GLA_QK_W), lambda b, c: (b * nc + c, 1)),
            pl.BlockSpec((C, GLA_V_W), lambda b, c: (b * nc + c, 1)),
            pl.BlockSpec((C, GLA_V_W), lambda b, c: (b * nc + c, 2)),
            pl.BlockSpec((C, LANES), lambda b, c: (b * nc + c, 0)),
            pl.BlockSpec((LANES, GLA_QK_W), lambda b, c: (0, 0)),
            pl.BlockSpec((1, GLA_QK_W), lambda b, c: (0, 0)),
            pl.BlockSpec((1, GLA_DV), lambda b, c: (0, 0)),
        ],
        out_specs=pl.BlockSpec((C, GLA_V_W), lambda b, c: (b * nc + c, 0)),
        out_shape=jax.ShapeDtypeStruct((B * S, GLA_V_W), BF16),
        scratch_shapes=[pltpu.VMEM((GLA_HEADS, GLA_DK, GLA_DV), F32)],
        compiler_params=_params(2),
        name="gla",
    )(z, z, z, z, lr, wup, bgk, norm_w)


def _dil_attn_kernel(q_ref, k_ref, v_ref, bucket_ref, table_ref, o_ref, lse_ref, bias_ref, *,
                     n_cls, n_blk, buckets, head0):
    scale = DIL_HEAD_DIM ** -0.5
    lane = lax.broadcasted_iota(jnp.int32, (DIL_BLOCK, LANES), 1)

    @pl.when((pl.program_id(0) == 0) & (pl.program_id(1) == 0))
    def _():
        bucket = bucket_ref[...]
        for h in range(DIL_SLOTS):
            bias = jnp.full(bucket.shape, NEG_INF, F32)
            for k in buckets:
                bias = jnp.where(bucket == k, table_ref[k, head0 + h], bias)
            bias_ref[h] = bias

    def block(c, start, first):
        lse_tile = jnp.zeros((DIL_BLOCK, LANES), F32)
        rows_q = pl.ds(start, DIL_BLOCK)
        rows_k = rows_q if first else pl.ds(start - DIL_BLOCK, 2 * DIL_BLOCK)
        n_keys = DIL_BLOCK if first else 2 * DIL_BLOCK
        ones = jnp.ones((n_keys, DIL_HEAD_DIM), BF16)
        cols = [slice(h * DIL_HEAD_DIM, (h + 1) * DIL_HEAD_DIM) for h in range(DIL_SLOTS)]
        scores = [
            _dot_t(q_ref[c, rows_q, cq], k_ref[c, rows_k, cq], ((1,), (1,))) * scale
            + (bias_ref[h, :, DIL_BLOCK:] if first else bias_ref[h])
            for h, cq in enumerate(cols)]
        maxes = [jnp.max(s, axis=-1, keepdims=True) for s in scores]
        probs = [jnp.exp(s - m).astype(BF16) for s, m in zip(scores, maxes)]
        num_ls = [_dot(p, jnp.concatenate([v_ref[c, rows_k, cq], ones], axis=1)) for p, cq in zip(probs, cols)]
        outs = []
        for h, (num_l, m) in enumerate(zip(num_ls, maxes)):
            l = num_l[:, DIL_HEAD_DIM:]
            outs.append((num_l[:, :DIL_HEAD_DIM] / l).astype(o_ref.dtype))
            lse_tile = jnp.where(lane == h, m + jnp.log(l), lse_tile)
        o_ref[c, rows_q, :] = jnp.concatenate(outs, axis=1)
        lse_ref[c, rows_q, :] = lse_tile

    def one_class(c, carry):
        block(c, 0, True)
        if n_blk > 1:
            def body(n, inner):
                block(c, pl.multiple_of(n * DIL_BLOCK, DIL_BLOCK), False)
                return inner
            lax.fori_loop(1, n_blk, body, 0)
        return carry

    if n_cls == 1:
        one_class(0, 0)
    else:
        lax.fori_loop(0, n_cls, one_class, 0)


def _band_buckets(dilation):
    a_idx = np.arange(DIL_BLOCK)[:, None]
    c_idx = np.arange(2 * DIL_BLOCK)[None, :]
    steps = DIL_BLOCK + a_idx - c_idx
    in_band = (steps >= 0) & (steps <= DIL_STEPS)
    bucket = _t5_bucket(np.clip(steps, 0, None) * dilation)
    return np.where(in_band, bucket, -1).astype(np.int32)


def _dil_attn(qkv, col_block, rel_bias, group):
    B, d, L, _ = qkv.shape
    n_cls = d
    bucket = _band_buckets(d)
    buckets = tuple(int(k) for k in np.unique(bucket) if k >= 0)

    def part(p):
        return pl.BlockSpec((None, n_cls, L, DIL_W), lambda b, r: (b, r, 0, col_block + p))

    return pl.pallas_call(
        functools.partial(_dil_attn_kernel, n_cls=n_cls, n_blk=L // DIL_BLOCK, buckets=buckets,
                          head0=group * DIL_SLOTS),
        grid=(B, d // n_cls),
        in_specs=[
            part(0), part(1), part(2),
            pl.BlockSpec((DIL_BLOCK, 2 * DIL_BLOCK), lambda b, r: (0, 0)),
            pl.BlockSpec(memory_space=pltpu.SMEM),
        ],
        scratch_shapes=[pltpu.VMEM((DIL_SLOTS, DIL_BLOCK, 2 * DIL_BLOCK), F32)],
        out_specs=[
            pl.BlockSpec((None, n_cls, L, DIL_W), lambda b, r: (b, r, 0, 0)),
            pl.BlockSpec((None, n_cls, L, LANES), lambda b, r: (b, r, 0, 0)),
        ],
        out_shape=[
            jax.ShapeDtypeStruct((B, d, L, DIL_W), BF16),
            jax.ShapeDtypeStruct((B, d, L, LANES), F32),
        ],
        compiler_params=_params(2),
        name=f"dil_attn_d{d}",
    )(qkv, qkv, qkv, jnp.asarray(bucket), rel_bias)


def _t5_bucket(dist):
    max_exact = REL_BUCKETS // 2
    d = np.maximum(dist, 1).astype(np.float64)
    large = max_exact + (np.log(d / max_exact) / np.log(REL_MAX_DIST / max_exact)
                         * (REL_BUCKETS - max_exact)).astype(np.int64)
    large = np.minimum(large, REL_BUCKETS - 1)
    return np.where(dist < max_exact, dist, large).astype(np.int32)


def _merge_kernel(og_ref, o1_ref, o4_ref, o16_ref, l1_ref, l4_ref, l16_ref, ga_ref, gb_ref, x_ref,
                  wa_ref, wb_ref, wo_ref, bg_ref, o_ref, s4_ref, s16_ref, sl4_ref, sl16_ref):
    T = og_ref.shape[0]
    a = _dot(og_ref[...], wa_ref[...])
    for d, grp_ref, lse_ref, s_ref, sl_ref in ((4, o4_ref, l4_ref, s4_ref, sl4_ref),
                                                (16, o16_ref, l16_ref, s16_ref, sl16_ref)):
        for r in range(d):
            rows = pl.ds(r, T // d, stride=d)
            sl_ref[rows, :] = lse_ref[r]
            for c in range(DIL_SLOTS):
                s_ref[c, rows, :] = grp_ref[r, :, c * LANES:(c + 1) * LANES].astype(F32)
    lses = (l1_ref[...], sl4_ref[...], sl16_ref[...])
    m = jnp.maximum(jnp.maximum(lses[0], lses[1]), lses[2])
    es = [jnp.exp(l - m) for l in lses]
    inv = 1.0 / (es[0] + es[1] + es[2])
    ws = [e * inv for e in es]
    heads = []
    for c in range(DIL_SLOTS):
        cs = slice(c * DIL_HEAD_DIM, (c + 1) * DIL_HEAD_DIM)
        mix = (ws[0][:, c:c + 1] * o1_ref[:, cs].astype(F32)
               + ws[1][:, c:c + 1] * s4_ref[c]
               + ws[2][:, c:c + 1] * s16_ref[c])
        heads.append(mix.astype(BF16))
    bm = _dot(jnp.concatenate(heads, axis=1), wb_ref[...])
    gate_a = jax.nn.sigmoid(ga_ref[...].astype(F32) + bg_ref[:, :D_MODEL])
    gate_b = jax.nn.sigmoid(gb_ref[...].astype(F32) + bg_ref[:, D_MODEL:])
    merged = (gate_a * a + gate_b * bm).astype(BF16)
    o_ref[...] = x_ref[...] + _dot(merged, wo_ref[...])


def _merge(o_gla, groups, z, gate_col_block, x, wa, wb, wo, b_gate, bm):
    B, S, D = x.shape
    nt = S // bm
    (o1, l1), (o4, l4), (o16, l16) = groups
    const = lambda i: (0, 0)
    resident = dict(pipeline_mode=pl.Buffered(1))

    def natural(width):
        return pl.BlockSpec((None, None, bm, width), lambda i: (i // nt, 0, i % nt, 0))

    def class_major(d, width):
        return pl.BlockSpec((None, d, bm // d, width), lambda i: (i // nt, 0, i % nt, 0))

    out = pl.pallas_call(
        _merge_kernel,
        grid=(B * nt,),
        in_specs=[
            pl.BlockSpec((bm, GLA_V_W), lambda i: (i, 0)),
            natural(DIL_W), class_major(4, DIL_W), class_major(16, DIL_W),
            natural(LANES), class_major(4, LANES), class_major(16, LANES),
            pl.BlockSpec((bm, D_MODEL), lambda i: (i, gate_col_block)),
            pl.BlockSpec((bm, D_MODEL), lambda i: (i, gate_col_block + 1)),
            pl.BlockSpec((bm, D_MODEL), lambda i: (i, 0)),
            pl.BlockSpec((GLA_V_W, D_MODEL), const, **resident),
            pl.BlockSpec((DIL_W, D_MODEL), const, **resident),
            pl.BlockSpec((D_MODEL, D_MODEL), const, **resident),
            pl.BlockSpec((1, 2 * D_MODEL), const),
        ],
        out_specs=pl.BlockSpec((bm, D_MODEL), lambda i: (i, 0)),
        out_shape=jax.ShapeDtypeStruct((B * S, D_MODEL), F32),
        scratch_shapes=[
            pltpu.VMEM((DIL_SLOTS, bm, LANES), F32), pltpu.VMEM((DIL_SLOTS, bm, LANES), F32),
            pltpu.VMEM((bm, LANES), F32), pltpu.VMEM((bm, LANES), F32),
        ],
        compiler_params=_params(1),
        name="merge_out_proj",
    )(o_gla, o1, o4, o16, l1, l4, l16, z, z, x.reshape(B * S, D), wa, wb, wo, b_gate)
    return out


def _ffn_kernel(x_ref, nw_ref, wg_ref, wu_ref, wd_ref, fw_ref, o_ref, h_ref):
    f = pl.program_id(1)

    @pl.when(f == 0)
    def _():
        x = x_ref[...]
        ms = jnp.mean(x * x, axis=-1, keepdims=True)
        h_ref[...] = (x * lax.rsqrt(ms + RMS_EPS) * nw_ref[...]).astype(BF16)
        o_ref[...] = x

    h = h_ref[...]
    gate = _dot(h, wg_ref[...])
    up = _dot(h, wu_ref[...])
    act = (gate * jax.nn.sigmoid(gate) * up).astype(BF16)
    for c in range(0, D_MODEL, FFN_OUT_CHUNK):
        o_ref[:, c:c + FFN_OUT_CHUNK] += _dot(act, wd_ref[:, c:c + FFN_OUT_CHUNK])

    @pl.when(f == pl.num_programs(1) - 1)
    def _():
        y = o_ref[...]
        ms = jnp.mean(y * y, axis=-1, keepdims=True)
        o_ref[...] = y * lax.rsqrt(ms + RMS_EPS) * fw_ref[...]


FFN_OUT_CHUNK = 512


def _ffn(x1, norm_w, w_in, w_out, final_w, bm, bf):
    M = x1.shape[0]
    nf = D_FF // bf
    return pl.pallas_call(
        _ffn_kernel,
        grid=(M // bm, nf),
        in_specs=[
            pl.BlockSpec((bm, D_MODEL), lambda i, f: (i, 0)),
            pl.BlockSpec((1, D_MODEL), lambda i, f: (0, 0)),
            pl.BlockSpec((D_MODEL, bf), lambda i, f: (0, f)),
            pl.BlockSpec((D_MODEL, bf), lambda i, f: (0, f + nf)),
            pl.BlockSpec((bf, D_MODEL), lambda i, f: (f, 0)),
            pl.BlockSpec((1, D_MODEL), lambda i, f: (0, 0)),
        ],
        out_specs=pl.BlockSpec((bm, D_MODEL), lambda i, f: (i, 0)),
        out_shape=jax.ShapeDtypeStruct((M, D_MODEL), F32),
        scratch_shapes=[pltpu.VMEM((bm, D_MODEL), BF16)],
        compiler_params=_params(2),
        name="swiglu_final_norm",
    )(x1, norm_w, w_in, w_in, w_out, final_w)


def kernel(x, attn_norm, w_in, w_gk_up, b_gk, gla_norm, b_gate, w_branch_gla, w_branch_dil,
           w_out, ffn_norm, w_ffn_in, w_ffn_out, rel_bias, final_norm):
    B, S, D = x.shape
    assert D == D_MODEL and S % (16 * DIL_BLOCK) == 0 and attn_norm.shape[0] == 1

    o_lr = 2 * GLA_QK_W + 2 * GLA_V_W
    o_dil = o_lr + GLA_LOWRANK
    o_gate = o_dil + 9 * DIL_W
    n_nat = o_lr + 2 * D_MODEL + 3 * DIL_W
    bn_nat = bn_dil = DIL_W
    lr_col = n_nat
    col_d4 = n_nat + bn_dil
    col_d16 = col_d4 + 3 * DIL_W
    gate_col_block = o_lr // D_MODEL
    d1_col = o_lr + 2 * D_MODEL
    w_all = _pack_w_in(
        jnp.swapaxes(w_in, 1, 2)[0],
        segments=((0, 0, o_lr), (o_lr, o_gate, 2 * D_MODEL), (d1_col, o_dil, 3 * DIL_W),
                  (lr_col, o_lr, GLA_LOWRANK), (col_d4, o_dil + 3 * DIL_W, 6 * DIL_W)),
        n_out=col_d16 + 3 * DIL_W)

    an = attn_norm[0][None, :]
    z_nat, h, lr = _in_proj_norm(x, an, w_all, n_nat, n_nat // 4, lr_col // LANES, rows=512)
    z_d4, w_ffn_in_bf = _in_proj_dil(h, w_all, col_d4, 3 * DIL_W, 4, bn_dil, w_ffn_in[0])
    z_d16, w_ffn_out_bf = _in_proj_dil(h, w_all, col_d16, 3 * DIL_W, 16, bn_dil, w_ffn_out[0])
    z2d = z_nat.reshape(B * S, n_nat)
    z_nat = z_nat.reshape(B, 1, S, n_nat)

    wup = jnp.concatenate([w_gk_up[0], jnp.zeros((LANES - GLA_LOWRANK, GLA_QK_W), F32)], axis=0).astype(BF16)
    o_gla = _gla(z2d, lr.reshape(B * S, LANES), wup, b_gk[0][None, :], gla_norm[0][None, :], B, S)

    groups = []
    for gi, (zg, col_block) in enumerate(((z_nat, d1_col // DIL_W), (z_d4, 0), (z_d16, 0))):
        groups.append(_dil_attn(zg, col_block, rel_bias, gi))
    x1 = _merge(o_gla, groups, z2d, gate_col_block, x,
                w_branch_gla[0].astype(BF16), w_branch_dil[0].astype(BF16), w_out[0].astype(BF16),
                b_gate[0][None, :], 256)
    out = _ffn(x1, ffn_norm[0][None, :], w_ffn_in_bf, w_ffn_out_bf, final_norm[None, :], 1024, 512)
    return out.reshape(B, S, D)
```

```python
import functools

import numpy as np
import jax
import jax.numpy as jnp
from jax import lax
from jax.experimental import pallas as pl
from jax.experimental.pallas import tpu as pltpu

F32 = jnp.float32
BF16 = jnp.bfloat16

D_MODEL = 2048
GLA_HEADS = 4
GLA_DK = 256
GLA_DV = 512
GLA_QK_W = GLA_HEADS * GLA_DK
GLA_V_W = GLA_HEADS * GLA_DV
GLA_LOWRANK = 16
GLA_GATE_NORM = 16.0
GLA_BLOCK = 128
GLA_STEP_ROWS = 256
DIL_DILATIONS = (1, 4, 16)
DIL_SLOTS = 8
DIL_HEAD_DIM = 128
DIL_W = DIL_SLOTS * DIL_HEAD_DIM
DIL_STEPS = 128
DIL_BLOCK = 128
REL_BUCKETS = 32
REL_MAX_DIST = 2048
D_FF = 5632
RMS_EPS = 1e-6
NEG_INF = -1e30
LOG2E = 1.4426950408889634
LN2 = 0.6931471805599453

LANES = 128
SUBLANES = 8
BF16_ROWS = 16
PROJ_ROWS = 1024
PROJ_CHUNK = 256
PROJ_NORM_CHUNK = 256
VMEM_LIMIT = 56 * 1024 * 1024


def _params(n_axes, vmem=VMEM_LIMIT):
    return pltpu.CompilerParams(dimension_semantics=("arbitrary",) * n_axes, vmem_limit_bytes=vmem)


def _dot(a, b):
    return jnp.dot(a, b, preferred_element_type=F32)


def _dot_t(a, b, dims):
    return lax.dot_general(a, b, (dims, ((), ())), preferred_element_type=F32)


def _split_bf16(x):
    hi = x.astype(BF16)
    lo = (x - hi.astype(F32)).astype(BF16)
    return hi, lo


def _in_proj_norm_kernel(x_ref, g_ref, w_ref, w_lr_ref, o_ref, h_ref, lr_ref):
    @pl.when(pl.program_id(2) == 0)
    def _():
        x = x_ref[...]
        ms = jnp.mean(x * x, axis=-1, keepdims=True)
        h = (x * lax.rsqrt(ms + RMS_EPS) * g_ref[...]).astype(h_ref.dtype)
        h_ref[...] = h
        lr_ref[...] = _dot(h, w_lr_ref[...]).astype(lr_ref.dtype)

    for c in range(0, w_ref.shape[1], PROJ_NORM_CHUNK):
        o_ref[:, c:c + PROJ_NORM_CHUNK] = _dot(h_ref[...], w_ref[:, c:c + PROJ_NORM_CHUNK]).astype(o_ref.dtype)


def _in_proj_norm(x, norm_w, w, n_cols, bn, lr_col_block, rows):
    B, S, D = x.shape
    return pl.pallas_call(
        _in_proj_norm_kernel,
        grid=(B, S // rows, n_cols // bn),
        in_specs=[
            pl.BlockSpec((None, rows, D), lambda b, t, j: (b, t, 0)),
            pl.BlockSpec((1, D), lambda b, t, j: (0, 0)),
            pl.BlockSpec((D, bn), lambda b, t, j: (0, j)),
            pl.BlockSpec((D, LANES), lambda b, t, j: (0, lr_col_block)),
        ],
        out_specs=[
            pl.BlockSpec((None, rows, bn), lambda b, t, j: (b, t, j)),
            pl.BlockSpec((None, rows, D), lambda b, t, j: (b, t, 0)),
            pl.BlockSpec((None, rows, LANES), lambda b, t, j: (b, t, 0)),
        ],
        out_shape=[jax.ShapeDtypeStruct((B, S, n_cols), BF16), jax.ShapeDtypeStruct((B, S, D), BF16),
                   jax.ShapeDtypeStruct((B, S, LANES), BF16)],
        compiler_params=_params(3),
        name="in_proj_norm",
    )(x, norm_w, w, w)


def _in_proj_dil_kernel(h_ref, w_ref, cast_ref, o_ref, cast_out_ref, *scratch, dilation, cast_moves):
    rows = h_ref.shape[0]
    bn = w_ref.shape[1]

    @pl.when(pl.program_id(2) == 0)
    def _():
        for dst, src, width in cast_moves:
            cast_out_ref[:, dst:dst + width] = cast_ref[:, src:src + width].astype(cast_out_ref.dtype)

    passes, rest = [], dilation
    while rest > 1:
        passes.append(min(rest, 4))
        rest //= passes[-1]

    def deinterleave(res, slab0):
        slabs = range(slab0, slab0 + res.shape[1] // LANES)
        for i, c in enumerate(slabs):
            scratch[0][c] = res[:, i * LANES:(i + 1) * LANES]
        src_ref, n_groups = scratch[0], 1
        for p, stride in enumerate(passes):
            group_rows = rows // n_groups
            last = p == len(passes) - 1
            for g in range(n_groups):
                for r in range(stride):
                    for c in slabs:
                        part = src_ref[c, pl.ds(g * group_rows + r, group_rows // stride, stride=stride), :]
                        if last:
                            o_ref[g + n_groups * r, :, c * LANES:(c + 1) * LANES] = part.astype(o_ref.dtype)
                        else:
                            dst = (g + n_groups * r) * (group_rows // stride)
                            scratch[1][c, dst:dst + group_rows // stride, :] = part
            if not last:
                src_ref, n_groups = scratch[1], n_groups * stride

    n_chunks = bn // PROJ_CHUNK
    res = _dot(h_ref[...], w_ref[:, :PROJ_CHUNK])
    for i in range(n_chunks):
        nxt = _dot(h_ref[...], w_ref[:, (i + 1) * PROJ_CHUNK:(i + 2) * PROJ_CHUNK]) if i + 1 < n_chunks else None
        deinterleave(res, i * PROJ_CHUNK // LANES)
        res = nxt


def _in_proj_dil(h, w, col0, n_cols, dilation, bn, to_cast, cast_moves=None):
    B, S, D = h.shape
    rows = PROJ_ROWS
    nt = S // rows
    assert col0 % bn == 0 and n_cols % bn == 0 and bn % PROJ_CHUNK == 0
    cast_rows = to_cast.shape[0] // (B * nt)
    assert cast_rows * B * nt == to_cast.shape[0] and cast_rows % BF16_ROWS == 0
    if cast_moves is None:
        cast_moves = ((0, 0, to_cast.shape[1]),)
    assert sum(m[2] for m in cast_moves) == to_cast.shape[1]
    cast_spec = pl.BlockSpec((cast_rows, to_cast.shape[1]), lambda b, t, j: (b * nt + t, 0))
    scratch = [pltpu.VMEM((bn // LANES, rows, LANES), F32)]
    if dilation > 4:
        scratch.append(pltpu.VMEM((bn // LANES, rows, LANES), F32))
    return pl.pallas_call(
        functools.partial(_in_proj_dil_kernel, dilation=dilation, cast_moves=tuple(cast_moves)),
        grid=(B, nt, n_cols // bn),
        in_specs=[
            pl.BlockSpec((None, rows, D), lambda b, t, j: (b, t, 0)),
            pl.BlockSpec((D, bn), lambda b, t, j: (0, col0 // bn + j)),
            cast_spec,
        ],
        out_specs=[
            pl.BlockSpec((None, dilation, rows // dilation, bn), lambda b, t, j: (b, 0, t, j)),
            cast_spec,
        ],
        out_shape=[jax.ShapeDtypeStruct((B, dilation, S // dilation, n_cols), BF16),
                   jax.ShapeDtypeStruct(to_cast.shape, BF16)],
        scratch_shapes=scratch,
        compiler_params=_params(3),
        name=f"in_proj_d{dilation}",
    )(h, w, to_cast)


PACK_COLS = 512


def _pack_w_in_kernel(src_ref, valid_ref, wt_ref, o_ref):
    x = wt_ref[...]
    row = lax.broadcasted_iota(jnp.int32, x.shape, 0)
    x = jnp.where(row < valid_ref[pl.program_id(0)], x, 0.0)
    o_ref[...] = x.T.astype(o_ref.dtype)


def _pack_w_in(wt, segments, n_out):
    N, D = wt.shape
    n_blocks = n_out // PACK_COLS
    src = np.zeros((n_blocks,), np.int32)
    valid = np.zeros((n_blocks,), np.int32)
    for dst, s, width in segments:
        assert dst % PACK_COLS == 0
        for i in range(dst // PACK_COLS, -(-(dst + width) // PACK_COLS)):
            start = s + i * PACK_COLS - dst
            assert start % SUBLANES == 0 and start + PACK_COLS <= N
            src[i] = start // SUBLANES
            valid[i] = min(PACK_COLS, dst + width - i * PACK_COLS)
    return pl.pallas_call(
        _pack_w_in_kernel,
        grid_spec=pltpu.PrefetchScalarGridSpec(
            num_scalar_prefetch=2,
            grid=(n_blocks,),
            in_specs=[pl.BlockSpec((pl.Element(PACK_COLS), pl.Element(D)),
                                   lambda i, src, valid: (src[i] * SUBLANES, 0))],
            out_specs=pl.BlockSpec((D, PACK_COLS), lambda i, src, valid: (0, i)),
        ),
        out_shape=jax.ShapeDtypeStruct((D, n_out), BF16),
        compiler_params=_params(1),
        name="pack_w_in",
    )(jnp.asarray(src), jnp.asarray(valid), wt)


def _gla_kernel(q_ref, k_ref, v_ref, g_ref, lr_ref, wup_ref, bgk_ref, nw_ref, o_ref, s_ref):
    C = GLA_BLOCK

    @pl.when(pl.program_id(1) == 0)
    def _():
        s_ref[...] = jnp.zeros_like(s_ref)

    row = lax.broadcasted_iota(jnp.int32, (C, C), 0)
    col = lax.broadcasted_iota(jnp.int32, (C, C), 1)
    causal = row >= col
    tri = jnp.where(causal, 1.0, 0.0).astype(BF16)
    ones = jnp.ones((C, LANES), BF16)
    scale = GLA_DK ** -0.5
    n_blk = q_ref.shape[0] // C
    heads = range(GLA_HEADS)
    ks = [slice(h * GLA_DK, (h + 1) * GLA_DK) for h in heads]
    vs = [slice(h * GLA_DV, (h + 1) * GLA_DV) for h in heads]
    units = [(blk, h) for blk in range(n_blk) for h in heads]
    rows = {u: slice(u[0] * C, (u[0] + 1) * C) for u in units}
    gks, b_cum, tot, att, q_dec, k_dec = {}, {}, {}, {}, {}, {}
    o_intra, q_in, k_end, decay = {}, {}, {}, {}
    for u in units:
        pre = _dot(lr_ref[rows[u], :], wup_ref[:, ks[u[1]]]) + bgk_ref[:, ks[u[1]]]
        log_sig = jnp.minimum(pre, 0.0) - jnp.log(1.0 + jnp.exp(-jnp.abs(pre)))
        gks[u] = _split_bf16(log_sig * (1.0 / GLA_GATE_NORM))
    for u in units:
        gk_hi, gk_lo = gks[u]
        b_cum[u] = _dot(tri, gk_hi) + _dot(tri, gk_lo)
        tot[u] = _dot_t(gk_hi, ones, ((0,), (0,))) + _dot_t(gk_lo, ones, ((0,), (0,)))
    for u in units:
        b = b_cum[u]
        b_mid = b[C // 2 - 1:C // 2, :]
        q_dec[u] = q_ref[rows[u], ks[u[1]]].astype(F32) * jnp.exp(b - b_mid)
        k_dec[u] = k_ref[rows[u], ks[u[1]]].astype(F32) * jnp.exp(b_mid - b)
        att[u] = _dot_t((q_dec[u] * scale).astype(BF16), k_dec[u].astype(BF16), ((1,), (1,)))
    for u in units:
        b = b_cum[u]
        b_last = b[C - 1:C, :]
        b_mid = b[C // 2 - 1:C // 2, :]
        o_intra[u] = _dot(jnp.where(causal, att[u], 0.0).astype(BF16), v_ref[rows[u], vs[u[1]]])
        q_in[u] = (q_dec[u] * (jnp.exp(b_mid) * scale)).astype(BF16)
        k_end[u] = (k_dec[u] * jnp.exp(b_last - b_mid)).astype(BF16)
        decay[u] = jnp.concatenate([jnp.exp(tot[u])] * (GLA_DV // LANES), axis=1)
    for blk in range(n_blk):
        rs = slice(blk * C, (blk + 1) * C)
        os = []
        for h in heads:
            s_old = s_ref[h]
            os.append(o_intra[blk, h] + _dot(q_in[blk, h], s_old.astype(BF16)))
            s_ref[h] = decay[blk, h] * s_old + _dot_t(k_end[blk, h], v_ref[rs, vs[h]], ((0,), (0,)))
        ys = []
        for h, o in zip(heads, os):
            ms = jnp.mean(o * o, axis=-1, keepdims=True)
            g = g_ref[rs, vs[h]].astype(F32)
            y = (o * lax.rsqrt(ms + RMS_EPS) * nw_ref[...]) * (g * jax.nn.sigmoid(g))
            ys.append(y.astype(o_ref.dtype))
        o_ref[rs, :] = jnp.concatenate(ys, axis=1)


def _gla(z, lr, wup, bgk, norm_w, B, S):
    C = GLA_STEP_ROWS
    nc = S // C
    return pl.pallas_call(
        _gla_kernel,
        grid=(B, nc),
        in_specs=[
            pl.BlockSpec((C, GLA_QK_W), lambda b, c: (b * nc + c, 0)),
            pl.BlockSpec((C, GLA_QK_W), lambda b, c: (b * nc + c, 1)),
            pl.BlockSpec((C, GLA_V_W), lambda b, c: (b * nc + c, 1)),
            pl.BlockSpec((C, GLA_V_W), lambda b, c: (b * nc + c, 2)),
            pl.BlockSpec((C, LANES), lambda b, c: (b * nc + c, 0)),
            pl.BlockSpec((LANES, GLA_QK_W), lambda b, c: (0, 0)),
            pl.BlockSpec((1, GLA_QK_W), lambda b, c: (0, 0)),
            pl.BlockSpec((1, GLA_DV), lambda b, c: (0, 0)),
        ],
        out_specs=pl.BlockSpec((C, GLA_V_W), lambda b, c: (b * nc + c, 0)),
        out_shape=jax.ShapeDtypeStruct((B * S, GLA_V_W), BF16),
        scratch_shapes=[pltpu.VMEM((GLA_HEADS, GLA_DK, GLA_DV), F32)],
        compiler_params=_params(2),
        name="gla",
    )(z, z, z, z, lr, wup, bgk, norm_w)


def _dil_attn_kernel(q_ref, k_ref, v_ref, bucket_ref, table_ref, cast_ref, o_ref, lse_ref, cast_out_ref,
                     bias_ref, *, n_cls, n_blk, buckets, head0):
    scale = DIL_HEAD_DIM ** -0.5
    lane = lax.broadcasted_iota(jnp.int32, (DIL_BLOCK, LANES), 1)
    cast_out_ref[...] = cast_ref[...].astype(cast_out_ref.dtype)

    @pl.when((pl.program_id(0) == 0) & (pl.program_id(1) == 0))
    def _():
        bucket = bucket_ref[...]
        for h in range(DIL_SLOTS):
            bias = jnp.full(bucket.shape, NEG_INF, F32)
            for k in buckets:
                bias = jnp.where(bucket == k, table_ref[k, head0 + h] * LOG2E, bias)
            bias_ref[h] = bias

    def block(c, start, first):
        lse_tile = jnp.zeros((DIL_BLOCK, LANES), F32)
        rows_q = pl.ds(start, DIL_BLOCK)
        rows_k = rows_q if first else pl.ds(start - DIL_BLOCK, 2 * DIL_BLOCK)
        n_keys = DIL_BLOCK if first else 2 * DIL_BLOCK
        ones = jnp.ones((n_keys, DIL_HEAD_DIM), BF16)
        cols = [slice(h * DIL_HEAD_DIM, (h + 1) * DIL_HEAD_DIM) for h in range(DIL_SLOTS)]
        scores = [
            _dot_t(q_ref[c, rows_q, cq], k_ref[c, rows_k, cq], ((1,), (1,))) * (scale * LOG2E)
            + (bias_ref[h, :, DIL_BLOCK:] if first else bias_ref[h])
            for h, cq in enumerate(cols)]
        maxes = [jnp.max(s, axis=-1, keepdims=True) for s in scores]
        probs = [jnp.exp2(s - m).astype(BF16) for s, m in zip(scores, maxes)]
        num_ls = [_dot(p, jnp.concatenate([v_ref[c, rows_k, cq], ones], axis=1)) for p, cq in zip(probs, cols)]
        outs = []
        for h, (num_l, m) in enumerate(zip(num_ls, maxes)):
            l = num_l[:, DIL_HEAD_DIM:]
            outs.append((num_l[:, :DIL_HEAD_DIM] / l).astype(o_ref.dtype))
            lse_tile = jnp.where(lane == h, m * LN2 + jnp.log(l), lse_tile)
        o_ref[c, rows_q, :] = jnp.concatenate(outs, axis=1)
        lse_ref[c, rows_q, :] = lse_tile

    def one_class(c, carry):
        block(c, 0, True)
        if n_blk > 1:
            def body(n, inner):
                block(c, pl.multiple_of(n * DIL_BLOCK, DIL_BLOCK), False)
                return inner
            lax.fori_loop(1, n_blk, body, 0)
        return carry

    if n_cls == 1:
        one_class(0, 0)
    else:
        lax.fori_loop(0, n_cls, one_class, 0)


def _band_buckets(dilation):
    a_idx = np.arange(DIL_BLOCK)[:, None]
    c_idx = np.arange(2 * DIL_BLOCK)[None, :]
    steps = DIL_BLOCK + a_idx - c_idx
    in_band = (steps >= 0) & (steps <= DIL_STEPS)
    bucket = _t5_bucket(np.clip(steps, 0, None) * dilation)
    return np.where(in_band, bucket, -1).astype(np.int32)


def _dil_attn(qkv, col_block, rel_bias, group, to_cast):
    B, d, L, _ = qkv.shape
    n_cls = d
    bucket = _band_buckets(d)
    buckets = tuple(int(k) for k in np.unique(bucket) if k >= 0)
    cast_rows = to_cast.shape[0] // B
    assert cast_rows * B == to_cast.shape[0] and cast_rows % BF16_ROWS == 0
    cast_spec = pl.BlockSpec((cast_rows, to_cast.shape[1]), lambda b, r: (b, 0))

    def part(p):
        return pl.BlockSpec((None, n_cls, L, DIL_W), lambda b, r: (b, r, 0, col_block + p))

    o, lse, cast = pl.pallas_call(
        functools.partial(_dil_attn_kernel, n_cls=n_cls, n_blk=L // DIL_BLOCK, buckets=buckets,
                          head0=group * DIL_SLOTS),
        grid=(B, d // n_cls),
        in_specs=[
            part(0), part(1), part(2),
            pl.BlockSpec((DIL_BLOCK, 2 * DIL_BLOCK), lambda b, r: (0, 0)),
            pl.BlockSpec(memory_space=pltpu.SMEM),
            cast_spec,
        ],
        scratch_shapes=[pltpu.VMEM((DIL_SLOTS, DIL_BLOCK, 2 * DIL_BLOCK), F32)],
        out_specs=[
            pl.BlockSpec((None, n_cls, L, DIL_W), lambda b, r: (b, r, 0, 0)),
            pl.BlockSpec((None, n_cls, L, LANES), lambda b, r: (b, r, 0, 0)),
            cast_spec,
        ],
        out_shape=[
            jax.ShapeDtypeStruct((B, d, L, DIL_W), BF16),
            jax.ShapeDtypeStruct((B, d, L, LANES), F32),
            jax.ShapeDtypeStruct(to_cast.shape, BF16),
        ],
        compiler_params=_params(2),
        name=f"dil_attn_d{d}",
    )(qkv, qkv, qkv, jnp.asarray(bucket), rel_bias, to_cast)
    return (o, lse), cast


def _t5_bucket(dist):
    max_exact = REL_BUCKETS // 2
    d = np.maximum(dist, 1).astype(np.float64)
    large = max_exact + (np.log(d / max_exact) / np.log(REL_MAX_DIST / max_exact)
                         * (REL_BUCKETS - max_exact)).astype(np.int64)
    large = np.minimum(large, REL_BUCKETS - 1)
    return np.where(dist < max_exact, dist, large).astype(np.int32)


def _merge_kernel(og_ref, o1_ref, o4_ref, o16_ref, l1_ref, l4_ref, l16_ref, ga_ref, gb_ref, x_ref,
                  wa_ref, wb_ref, wo_ref, bg_ref, o_ref, s4_ref, s16_ref, sl4_ref, sl16_ref):
    T = og_ref.shape[0]
    a = _dot(og_ref[...], wa_ref[...])
    for d, grp_ref, lse_ref, s_ref, sl_ref in ((4, o4_ref, l4_ref, s4_ref, sl4_ref),
                                                (16, o16_ref, l16_ref, s16_ref, sl16_ref)):
        for r in range(d):
            rows = pl.ds(r, T // d, stride=d)
            sl_ref[rows, :] = lse_ref[r]
            for c in range(DIL_SLOTS):
                s_ref[c, rows, :] = grp_ref[r, :, c * LANES:(c + 1) * LANES].astype(F32)
    lses = (l1_ref[...], sl4_ref[...], sl16_ref[...])
    m = jnp.maximum(jnp.maximum(lses[0], lses[1]), lses[2])
    es = [jnp.exp(l - m) for l in lses]
    inv = 1.0 / (es[0] + es[1] + es[2])
    ws = [e * inv for e in es]
    heads = []
    for c in range(DIL_SLOTS):
        cs = slice(c * DIL_HEAD_DIM, (c + 1) * DIL_HEAD_DIM)
        mix = (ws[0][:, c:c + 1] * o1_ref[:, cs].astype(F32)
               + ws[1][:, c:c + 1] * s4_ref[c]
               + ws[2][:, c:c + 1] * s16_ref[c])
        heads.append(mix.astype(BF16))
    bm = _dot(jnp.concatenate(heads, axis=1), wb_ref[...])
    gate_a = jax.nn.sigmoid(ga_ref[...].astype(F32) + bg_ref[:, :D_MODEL])
    gate_b = jax.nn.sigmoid(gb_ref[...].astype(F32) + bg_ref[:, D_MODEL:])
    merged = (gate_a * a + gate_b * bm).astype(BF16)
    o_ref[...] = x_ref[...] + _dot(merged, wo_ref[...])


def _merge(o_gla, groups, z, gate_col_block, x, wa, wb, wo, b_gate, bm):
    B, S, D = x.shape
    nt = S // bm
    (o1, l1), (o4, l4), (o16, l16) = groups
    const = lambda i: (0, 0)
    resident = dict(pipeline_mode=pl.Buffered(1))

    def natural(width):
        return pl.BlockSpec((None, None, bm, width), lambda i: (i // nt, 0, i % nt, 0))

    def class_major(d, width):
        return pl.BlockSpec((None, d, bm // d, width), lambda i: (i // nt, 0, i % nt, 0))

    out = pl.pallas_call(
        _merge_kernel,
        grid=(B * nt,),
        in_specs=[
            pl.BlockSpec((bm, GLA_V_W), lambda i: (i, 0)),
            natural(DIL_W), class_major(4, DIL_W), class_major(16, DIL_W),
            natural(LANES), class_major(4, LANES), class_major(16, LANES),
            pl.BlockSpec((bm, D_MODEL), lambda i: (i, gate_col_block)),
            pl.BlockSpec((bm, D_MODEL), lambda i: (i, gate_col_block + 1)),
            pl.BlockSpec((bm, D_MODEL), lambda i: (i, 0)),
            pl.BlockSpec((GLA_V_W, D_MODEL), const, **resident),
            pl.BlockSpec((DIL_W, D_MODEL), const, **resident),
            pl.BlockSpec((D_MODEL, D_MODEL), const, **resident),
            pl.BlockSpec((1, 2 * D_MODEL), const),
        ],
        out_specs=pl.BlockSpec((bm, D_MODEL), lambda i: (i, 0)),
        out_shape=jax.ShapeDtypeStruct((B * S, D_MODEL), F32),
        scratch_shapes=[
            pltpu.VMEM((DIL_SLOTS, bm, LANES), F32), pltpu.VMEM((DIL_SLOTS, bm, LANES), F32),
            pltpu.VMEM((bm, LANES), F32), pltpu.VMEM((bm, LANES), F32),
        ],
        compiler_params=_params(1),
        name="merge_out_proj",
    )(o_gla, o1, o4, o16, l1, l4, l16, z, z, x.reshape(B * S, D), wa, wb, wo, b_gate)
    return out


def _ffn_kernel(x_ref, nw_ref, wgu_ref, wd_ref, fw_ref, o_ref, h_ref):
    f = pl.program_id(1)

    @pl.when(f == 0)
    def _():
        x = x_ref[...]
        ms = jnp.mean(x * x, axis=-1, keepdims=True)
        h_ref[...] = (x * lax.rsqrt(ms + RMS_EPS) * nw_ref[...]).astype(BF16)
        o_ref[...] = x

    bf = wd_ref.shape[0]
    gate_up = _dot(h_ref[...], wgu_ref[...])
    gate, up = gate_up[:, :bf], gate_up[:, bf:]
    act = (gate * jax.nn.sigmoid(gate) * up).astype(BF16)
    for c in range(0, D_MODEL, FFN_OUT_CHUNK):
        o_ref[:, c:c + FFN_OUT_CHUNK] += _dot(act, wd_ref[:, c:c + FFN_OUT_CHUNK])

    @pl.when(f == pl.num_programs(1) - 1)
    def _():
        y = o_ref[...]
        ms = jnp.mean(y * y, axis=-1, keepdims=True)
        o_ref[...] = y * lax.rsqrt(ms + RMS_EPS) * fw_ref[...]


FFN_OUT_CHUNK = 512
MERGE_ROWS = 256
FFN_ROWS = 1024
FFN_HIDDEN_BLOCK = 512


def _ffn_weight_moves(bf):
    moves = []
    for f in range(D_FF // bf):
        moves.append((2 * f * bf, f * bf, bf))
        moves.append(((2 * f + 1) * bf, D_FF + f * bf, bf))
    return moves


def _ffn(x1, norm_w, w_in, w_out, final_w, bm, bf):
    M = x1.shape[0]
    nf = D_FF // bf
    return pl.pallas_call(
        _ffn_kernel,
        grid=(M // bm, nf),
        in_specs=[
            pl.BlockSpec((bm, D_MODEL), lambda i, f: (i, 0)),
            pl.BlockSpec((1, D_MODEL), lambda i, f: (0, 0)),
            pl.BlockSpec((D_MODEL, 2 * bf), lambda i, f: (0, f)),
            pl.BlockSpec((bf, D_MODEL), lambda i, f: (f, 0)),
            pl.BlockSpec((1, D_MODEL), lambda i, f: (0, 0)),
        ],
        out_specs=pl.BlockSpec((bm, D_MODEL), lambda i, f: (i, 0)),
        out_shape=jax.ShapeDtypeStruct((M, D_MODEL), F32),
        scratch_shapes=[pltpu.VMEM((bm, D_MODEL), BF16)],
        compiler_params=_params(2),
        name="swiglu_final_norm",
    )(x1, norm_w, w_in, w_out, final_w)


def kernel(x, attn_norm, w_in, w_gk_up, b_gk, gla_norm, b_gate, w_branch_gla, w_branch_dil,
           w_out, ffn_norm, w_ffn_in, w_ffn_out, rel_bias, final_norm):
    B, S, D = x.shape
    assert D == D_MODEL and S % (16 * DIL_BLOCK) == 0 and attn_norm.shape[0] == 1

    o_lr = 2 * GLA_QK_W + 2 * GLA_V_W
    o_dil = o_lr + GLA_LOWRANK
    o_gate = o_dil + 9 * DIL_W
    n_nat = o_lr + 2 * D_MODEL + 3 * DIL_W
    bn_nat = bn_dil = DIL_W
    lr_col = n_nat
    col_d4 = n_nat + bn_dil
    col_d16 = col_d4 + 3 * DIL_W
    gate_col_block = o_lr // D_MODEL
    d1_col = o_lr + 2 * D_MODEL
    w_all = _pack_w_in(
        jnp.swapaxes(w_in, 1, 2)[0],
        segments=((0, 0, o_lr), (o_lr, o_gate, 2 * D_MODEL), (d1_col, o_dil, 3 * DIL_W),
                  (lr_col, o_lr, GLA_LOWRANK), (col_d4, o_dil + 3 * DIL_W, 6 * DIL_W)),
        n_out=col_d16 + 3 * DIL_W)

    an = attn_norm[0][None, :]
    z_nat, h, lr = _in_proj_norm(x, an, w_all, n_nat, n_nat // 4, lr_col // LANES, rows=512)
    z_d4, w_ffn_in_bf = _in_proj_dil(h, w_all, col_d4, 3 * DIL_W, 4, bn_dil, w_ffn_in[0],
                                     _ffn_weight_moves(FFN_HIDDEN_BLOCK))
    z_d16, w_ffn_out_bf = _in_proj_dil(h, w_all, col_d16, 3 * DIL_W, 16, bn_dil, w_ffn_out[0])
    z2d = z_nat.reshape(B * S, n_nat)
    z_nat = z_nat.reshape(B, 1, S, n_nat)

    wup = jnp.concatenate([w_gk_up[0], jnp.zeros((LANES - GLA_LOWRANK, GLA_QK_W), F32)], axis=0).astype(BF16)
    o_gla = _gla(z2d, lr.reshape(B * S, LANES), wup, b_gk[0][None, :], gla_norm[0][None, :], B, S)

    groups, merge_w = [], []
    for gi, (zg, col_block, w) in enumerate(((z_nat, d1_col // DIL_W, w_branch_gla[0]), (z_d4, 0, w_branch_dil[0]),
                                             (z_d16, 0, w_out[0]))):
        group, w_bf = _dil_attn(zg, col_block, rel_bias, gi, w)
        groups.append(group)
        merge_w.append(w_bf)
    x1 = _merge(o_gla, groups, z2d, gate_col_block, x, merge_w[0], merge_w[1], merge_w[2],
                b_gate[0][None, :], MERGE_ROWS)
    out = _ffn(x1, ffn_norm[0][None, :], w_ffn_in_bf, w_ffn_out_bf, final_norm[None, :],
               FFN_ROWS, FFN_HIDDEN_BLOCK)
    return out.reshape(B, S, D)
```

```python
import functools

import numpy as np
import jax
import jax.numpy as jnp
from jax import lax
from jax.experimental import pallas as pl
from jax.experimental.pallas import tpu as pltpu

F32 = jnp.float32
BF16 = jnp.bfloat16

D_MODEL = 2048
GLA_HEADS = 4
GLA_DK = 256
GLA_DV = 512
GLA_QK_W = GLA_HEADS * GLA_DK
GLA_V_W = GLA_HEADS * GLA_DV
GLA_LOWRANK = 16
GLA_GATE_NORM = 16.0
GLA_BLOCK = 128
GLA_STEP_ROWS = 256
DIL_DILATIONS = (1, 4, 16)
DIL_SLOTS = 8
DIL_HEAD_DIM = 128
DIL_W = DIL_SLOTS * DIL_HEAD_DIM
DIL_STEPS = 128
DIL_BLOCK = 128
REL_BUCKETS = 32
REL_MAX_DIST = 2048
D_FF = 5632
RMS_EPS = 1e-6
NEG_INF = -1e30
LOG2E = 1.4426950408889634
LN2 = 0.6931471805599453

LANES = 128
SUBLANES = 8
BF16_ROWS = 16
PROJ_ROWS = 1024
PROJ_CHUNK = 256
PROJ_NORM_CHUNK = 256
VMEM_LIMIT = 56 * 1024 * 1024


def _params(n_axes, vmem=VMEM_LIMIT):
    return pltpu.CompilerParams(dimension_semantics=("arbitrary",) * n_axes, vmem_limit_bytes=vmem)


def _dot(a, b):
    return jnp.dot(a, b, preferred_element_type=F32)


def _dot_t(a, b, dims):
    return lax.dot_general(a, b, (dims, ((), ())), preferred_element_type=F32)


def _split_bf16(x):
    hi = x.astype(BF16)
    lo = (x - hi.astype(F32)).astype(BF16)
    return hi, lo


def _in_proj_norm_kernel(x_ref, g_ref, w_ref, w_lr_ref, o_ref, h_ref, lr_ref):
    @pl.when(pl.program_id(2) == 0)
    def _():
        x = x_ref[...]
        ms = jnp.mean(x * x, axis=-1, keepdims=True)
        h = (x * lax.rsqrt(ms + RMS_EPS) * g_ref[...]).astype(h_ref.dtype)
        h_ref[...] = h
        lr_ref[...] = _dot(h, w_lr_ref[...]).astype(lr_ref.dtype)

    for c in range(0, w_ref.shape[1], PROJ_NORM_CHUNK):
        o_ref[:, c:c + PROJ_NORM_CHUNK] = _dot(h_ref[...], w_ref[:, c:c + PROJ_NORM_CHUNK]).astype(o_ref.dtype)


def _in_proj_norm(x, norm_w, w, n_cols, bn, lr_col_block, rows):
    B, S, D = x.shape
    return pl.pallas_call(
        _in_proj_norm_kernel,
        grid=(B, S // rows, n_cols // bn),
        in_specs=[
            pl.BlockSpec((None, rows, D), lambda b, t, j: (b, t, 0)),
            pl.BlockSpec((1, D), lambda b, t, j: (0, 0)),
            pl.BlockSpec((D, bn), lambda b, t, j: (0, j)),
            pl.BlockSpec((D, LANES), lambda b, t, j: (0, lr_col_block)),
        ],
        out_specs=[
            pl.BlockSpec((None, rows, bn), lambda b, t, j: (b, t, j)),
            pl.BlockSpec((None, rows, D), lambda b, t, j: (b, t, 0)),
            pl.BlockSpec((None, rows, LANES), lambda b, t, j: (b, t, 0)),
        ],
        out_shape=[jax.ShapeDtypeStruct((B, S, n_cols), BF16), jax.ShapeDtypeStruct((B, S, D), BF16),
                   jax.ShapeDtypeStruct((B, S, LANES), BF16)],
        compiler_params=_params(3),
        name="in_proj_norm",
    )(x, norm_w, w, w)


def _in_proj_dil_kernel(h_ref, w_ref, cast_ref, o_ref, cast_out_ref, *scratch, dilation, cast_moves):
    rows = h_ref.shape[0]
    chunk_slabs = PROJ_CHUNK // LANES

    for dst, src, width in cast_moves:
        cast_out_ref[:, dst:dst + width] = cast_ref[:, src:src + width].astype(cast_out_ref.dtype)

    passes, rest = [], dilation
    while rest > 1:
        passes.append(min(rest, 4))
        rest //= passes[-1]

    def deinterleave(res, chunk):
        base = (chunk % 2) * chunk_slabs
        for i in range(chunk_slabs):
            scratch[0][base + i] = res[:, i * LANES:(i + 1) * LANES]
        src_ref, n_groups = scratch[0], 1
        for p, stride in enumerate(passes):
            group_rows = rows // n_groups
            last = p == len(passes) - 1
            for g in range(n_groups):
                for r in range(stride):
                    for i in range(chunk_slabs):
                        part = src_ref[base + i, pl.ds(g * group_rows + r, group_rows // stride, stride=stride), :]
                        if last:
                            col = (chunk * chunk_slabs + i) * LANES
                            o_ref[g + n_groups * r, :, col:col + LANES] = part.astype(o_ref.dtype)
                        else:
                            dst = (g + n_groups * r) * (group_rows // stride)
                            scratch[1][base + i, dst:dst + group_rows // stride, :] = part
            if not last:
                src_ref, n_groups = scratch[1], n_groups * stride

    n_chunks = w_ref.shape[1] // PROJ_CHUNK
    res = _dot(h_ref[...], w_ref[:, :PROJ_CHUNK])
    for i in range(n_chunks):
        nxt = _dot(h_ref[...], w_ref[:, (i + 1) * PROJ_CHUNK:(i + 2) * PROJ_CHUNK]) if i + 1 < n_chunks else None
        deinterleave(res, i)
        res = nxt


def _in_proj_dil(h, w, col0, n_cols, dilation, to_cast, cast_moves=None):
    B, S, D = h.shape
    rows = PROJ_ROWS
    nt = S // rows
    assert col0 % n_cols == 0 and n_cols % PROJ_CHUNK == 0
    cast_rows = to_cast.shape[0] // (B * nt)
    assert cast_rows * B * nt == to_cast.shape[0] and cast_rows % BF16_ROWS == 0
    if cast_moves is None:
        cast_moves = ((0, 0, to_cast.shape[1]),)
    assert sum(m[2] for m in cast_moves) == to_cast.shape[1]
    cast_spec = pl.BlockSpec((cast_rows, to_cast.shape[1]), lambda b, t: (b * nt + t, 0))
    n_slabs = 2 * PROJ_CHUNK // LANES
    scratch = [pltpu.VMEM((n_slabs, rows, LANES), F32)]
    if dilation > 4:
        scratch.append(pltpu.VMEM((n_slabs, rows, LANES), F32))
    return pl.pallas_call(
        functools.partial(_in_proj_dil_kernel, dilation=dilation, cast_moves=tuple(cast_moves)),
        grid=(B, nt),
        in_specs=[
            pl.BlockSpec((None, rows, D), lambda b, t: (b, t, 0)),
            pl.BlockSpec((D, n_cols), lambda b, t: (0, col0 // n_cols), pipeline_mode=pl.Buffered(1)),
            cast_spec,
        ],
        out_specs=[
            pl.BlockSpec((None, dilation, rows // dilation, n_cols), lambda b, t: (b, 0, t, 0)),
            cast_spec,
        ],
        out_shape=[jax.ShapeDtypeStruct((B, dilation, S // dilation, n_cols), BF16),
                   jax.ShapeDtypeStruct(to_cast.shape, BF16)],
        scratch_shapes=scratch,
        compiler_params=_params(2),
        name=f"in_proj_d{dilation}",
    )(h, w, to_cast)


PACK_COLS = 512


def _pack_w_in_kernel(src_ref, valid_ref, wt_ref, o_ref):
    x = wt_ref[...]
    row = lax.broadcasted_iota(jnp.int32, x.shape, 0)
    x = jnp.where(row < valid_ref[pl.program_id(0)], x, 0.0)
    o_ref[...] = x.T.astype(o_ref.dtype)


def _pack_w_in(wt, segments, n_out):
    N, D = wt.shape
    n_blocks = n_out // PACK_COLS
    src = np.zeros((n_blocks,), np.int32)
    valid = np.zeros((n_blocks,), np.int32)
    for dst, s, width in segments:
        assert dst % PACK_COLS == 0
        for i in range(dst // PACK_COLS, -(-(dst + width) // PACK_COLS)):
            start = s + i * PACK_COLS - dst
            assert start % SUBLANES == 0 and start + PACK_COLS <= N
            src[i] = start // SUBLANES
            valid[i] = min(PACK_COLS, dst + width - i * PACK_COLS)
    return pl.pallas_call(
        _pack_w_in_kernel,
        grid_spec=pltpu.PrefetchScalarGridSpec(
            num_scalar_prefetch=2,
            grid=(n_blocks,),
            in_specs=[pl.BlockSpec((pl.Element(PACK_COLS), pl.Element(D)),
                                   lambda i, src, valid: (src[i] * SUBLANES, 0))],
            out_specs=pl.BlockSpec((D, PACK_COLS), lambda i, src, valid: (0, i)),
        ),
        out_shape=jax.ShapeDtypeStruct((D, n_out), BF16),
        compiler_params=_params(1),
        name="pack_w_in",
    )(jnp.asarray(src), jnp.asarray(valid), wt)


def _gla_kernel(q_ref, k_ref, v_ref, g_ref, lr_ref, wup_ref, bgk_ref, nw_ref, o_ref, s_ref):
    C = GLA_BLOCK

    @pl.when(pl.program_id(1) == 0)
    def _():
        s_ref[...] = jnp.zeros_like(s_ref)

    row = lax.broadcasted_iota(jnp.int32, (C, C), 0)
    col = lax.broadcasted_iota(jnp.int32, (C, C), 1)
    causal = row >= col
    tri = jnp.where(causal, 1.0, 0.0).astype(BF16)
    ones = jnp.ones((C, LANES), BF16)
    scale = GLA_DK ** -0.5
    n_blk = q_ref.shape[0] // C
    heads = range(GLA_HEADS)
    ks = [slice(h * GLA_DK, (h + 1) * GLA_DK) for h in heads]
    vs = [slice(h * GLA_DV, (h + 1) * GLA_DV) for h in heads]
    units = [(blk, h) for blk in range(n_blk) for h in heads]
    rows = {u: slice(u[0] * C, (u[0] + 1) * C) for u in units}
    gks, b_cum, tot, att, q_dec, k_dec = {}, {}, {}, {}, {}, {}
    o_intra, q_in, k_end, decay = {}, {}, {}, {}
    for u in units:
        pre = _dot(lr_ref[rows[u], :], wup_ref[:, ks[u[1]]]) + bgk_ref[:, ks[u[1]]]
        log_sig = jnp.minimum(pre, 0.0) - jnp.log(1.0 + jnp.exp(-jnp.abs(pre)))
        gks[u] = _split_bf16(log_sig * (1.0 / GLA_GATE_NORM))
    for u in units:
        gk_hi, gk_lo = gks[u]
        b_cum[u] = _dot(tri, gk_hi) + _dot(tri, gk_lo)
        tot[u] = _dot_t(gk_hi, ones, ((0,), (0,))) + _dot_t(gk_lo, ones, ((0,), (0,)))
    for u in units:
        b = b_cum[u]
        b_mid = b[C // 2 - 1:C // 2, :]
        q_dec[u] = q_ref[rows[u], ks[u[1]]].astype(F32) * jnp.exp(b - b_mid)
        k_dec[u] = k_ref[rows[u], ks[u[1]]].astype(F32) * jnp.exp(b_mid - b)
        att[u] = _dot_t((q_dec[u] * scale).astype(BF16), k_dec[u].astype(BF16), ((1,), (1,)))
    for u in units:
        b = b_cum[u]
        b_last = b[C - 1:C, :]
        b_mid = b[C // 2 - 1:C // 2, :]
        o_intra[u] = _dot(jnp.where(causal, att[u], 0.0).astype(BF16), v_ref[rows[u], vs[u[1]]])
        q_in[u] = (q_dec[u] * (jnp.exp(b_mid) * scale)).astype(BF16)
        k_end[u] = (k_dec[u] * jnp.exp(b_last - b_mid)).astype(BF16)
        decay[u] = jnp.concatenate([jnp.exp(tot[u])] * (GLA_DV // LANES), axis=1)
    for blk in range(n_blk):
        rs = slice(blk * C, (blk + 1) * C)
        os = []
        for h in heads:
            s_old = s_ref[h]
            os.append(o_intra[blk, h] + _dot(q_in[blk, h], s_old.astype(BF16)))
            s_ref[h] = decay[blk, h] * s_old + _dot_t(k_end[blk, h], v_ref[rs, vs[h]], ((0,), (0,)))
        ys = []
        for h, o in zip(heads, os):
            ms = jnp.mean(o * o, axis=-1, keepdims=True)
            g = g_ref[rs, vs[h]].astype(F32)
            y = (o * lax.rsqrt(ms + RMS_EPS) * nw_ref[...]) * (g * jax.nn.sigmoid(g))
            ys.append(y.astype(o_ref.dtype))
        o_ref[rs, :] = jnp.concatenate(ys, axis=1)


def _gla(z, lr, wup, bgk, norm_w, B, S):
    C = GLA_STEP_ROWS
    nc = S // C
    return pl.pallas_call(
        _gla_kernel,
        grid=(B, nc),
        in_specs=[
            pl.BlockSpec((C, GLA_QK_W), lambda b, c: (b * nc + c, 0)),
            pl.BlockSpec((C, GLA_QK_W), lambda b, c: (b * nc + c, 1)),
            pl.BlockSpec((C, GLA_V_W), lambda b, c: (b * nc + c, 1)),
            pl.BlockSpec((C, GLA_V_W), lambda b, c: (b * nc + c, 2)),
            pl.BlockSpec((C, LANES), lambda b, c: (b * nc + c, 0)),
            pl.BlockSpec((LANES, GLA_QK_W), lambda b, c: (0, 0)),
            pl.BlockSpec((1, GLA_QK_W), lambda b, c: (0, 0)),
            pl.BlockSpec((1, GLA_DV), lambda b, c: (0, 0)),
        ],
        out_specs=pl.BlockSpec((C, GLA_V_W), lambda b, c: (b * nc + c, 0)),
        out_shape=jax.ShapeDtypeStruct((B * S, GLA_V_W), BF16),
        scratch_shapes=[pltpu.VMEM((GLA_HEADS, GLA_DK, GLA_DV), F32)],
        compiler_params=_params(2),
        name="gla",
    )(z, z, z, z, lr, wup, bgk, norm_w)


def _dil_attn_kernel(q_ref, k_ref, v_ref, bucket_ref, table_ref, cast_ref, o_ref, lse_ref, cast_out_ref,
                     bias_ref, *, n_cls, n_blk, buckets, head0):
    scale = DIL_HEAD_DIM ** -0.5
    lane = lax.broadcasted_iota(jnp.int32, (DIL_BLOCK, LANES), 1)
    cast_out_ref[...] = cast_ref[...].astype(cast_out_ref.dtype)

    @pl.when((pl.program_id(0) == 0) & (pl.program_id(1) == 0))
    def _():
        bucket = bucket_ref[...]
        for h in range(DIL_SLOTS):
            bias = jnp.full(bucket.shape, NEG_INF, F32)
            for k in buckets:
                bias = jnp.where(bucket == k, table_ref[k, head0 + h] * LOG2E, bias)
            bias_ref[h] = bias

    def block(c, start, first):
        lse_tile = jnp.zeros((DIL_BLOCK, LANES), F32)
        rows_q = pl.ds(start, DIL_BLOCK)
        rows_k = rows_q if first else pl.ds(start - DIL_BLOCK, 2 * DIL_BLOCK)
        n_keys = DIL_BLOCK if first else 2 * DIL_BLOCK
        ones = jnp.ones((n_keys, DIL_HEAD_DIM), BF16)
        cols = [slice(h * DIL_HEAD_DIM, (h + 1) * DIL_HEAD_DIM) for h in range(DIL_SLOTS)]
        scores = [
            _dot_t(q_ref[c, rows_q, cq], k_ref[c, rows_k, cq], ((1,), (1,))) * (scale * LOG2E)
            + (bias_ref[h, :, DIL_BLOCK:] if first else bias_ref[h])
            for h, cq in enumerate(cols)]
        maxes = [jnp.max(s, axis=-1, keepdims=True) for s in scores]
        probs = [jnp.exp2(s - m).astype(BF16) for s, m in zip(scores, maxes)]
        num_ls = [_dot(p, jnp.concatenate([v_ref[c, rows_k, cq], ones], axis=1)) for p, cq in zip(probs, cols)]
        outs = []
        for h, (num_l, m) in enumerate(zip(num_ls, maxes)):
            l = num_l[:, DIL_HEAD_DIM:]
            outs.append((num_l[:, :DIL_HEAD_DIM] / l).astype(o_ref.dtype))
            lse_tile = jnp.where(lane == h, m * LN2 + jnp.log(l), lse_tile)
        o_ref[c, rows_q, :] = jnp.concatenate(outs, axis=1)
        lse_ref[c, rows_q, :] = lse_tile

    def one_class(c, carry):
        block(c, 0, True)
        if n_blk > 1:
            def body(n, inner):
                block(c, pl.multiple_of(n * DIL_BLOCK, DIL_BLOCK), False)
                return inner
            lax.fori_loop(1, n_blk, body, 0)
        return carry

    if n_cls == 1:
        one_class(0, 0)
    else:
        lax.fori_loop(0, n_cls, one_class, 0)


def _band_buckets(dilation):
    a_idx = np.arange(DIL_BLOCK)[:, None]
    c_idx = np.arange(2 * DIL_BLOCK)[None, :]
    steps = DIL_BLOCK + a_idx - c_idx
    in_band = (steps >= 0) & (steps <= DIL_STEPS)
    bucket = _t5_bucket(np.clip(steps, 0, None) * dilation)
    return np.where(in_band, bucket, -1).astype(np.int32)


def _dil_attn(qkv, col_block, rel_bias, group, to_cast):
    B, d, L, _ = qkv.shape
    n_cls = d
    bucket = _band_buckets(d)
    buckets = tuple(int(k) for k in np.unique(bucket) if k >= 0)
    cast_rows = to_cast.shape[0] // B
    assert cast_rows * B == to_cast.shape[0] and cast_rows % BF16_ROWS == 0
    cast_spec = pl.BlockSpec((cast_rows, to_cast.shape[1]), lambda b, r: (b, 0))

    def part(p):
        return pl.BlockSpec((None, n_cls, L, DIL_W), lambda b, r: (b, r, 0, col_block + p))

    o, lse, cast = pl.pallas_call(
        functools.partial(_dil_attn_kernel, n_cls=n_cls, n_blk=L // DIL_BLOCK, buckets=buckets,
                          head0=group * DIL_SLOTS),
        grid=(B, d // n_cls),
        in_specs=[
            part(0), part(1), part(2),
            pl.BlockSpec((DIL_BLOCK, 2 * DIL_BLOCK), lambda b, r: (0, 0)),
            pl.BlockSpec(memory_space=pltpu.SMEM),
            cast_spec,
        ],
        scratch_shapes=[pltpu.VMEM((DIL_SLOTS, DIL_BLOCK, 2 * DIL_BLOCK), F32)],
        out_specs=[
            pl.BlockSpec((None, n_cls, L, DIL_W), lambda b, r: (b, r, 0, 0)),
            pl.BlockSpec((None, n_cls, L, LANES), lambda b, r: (b, r, 0, 0)),
            cast_spec,
        ],
        out_shape=[
            jax.ShapeDtypeStruct((B, d, L, DIL_W), BF16),
            jax.ShapeDtypeStruct((B, d, L, LANES), F32),
            jax.ShapeDtypeStruct(to_cast.shape, BF16),
        ],
        compiler_params=_params(2),
        name=f"dil_attn_d{d}",
    )(qkv, qkv, qkv, jnp.asarray(bucket), rel_bias, to_cast)
    return (o, lse), cast


def _t5_bucket(dist):
    max_exact = REL_BUCKETS // 2
    d = np.maximum(dist, 1).astype(np.float64)
    large = max_exact + (np.log(d / max_exact) / np.log(REL_MAX_DIST / max_exact)
                         * (REL_BUCKETS - max_exact)).astype(np.int64)
    large = np.minimum(large, REL_BUCKETS - 1)
    return np.where(dist < max_exact, dist, large).astype(np.int32)


def _merge_kernel(og_ref, o1_ref, o4_ref, o16_ref, l1_ref, l4_ref, l16_ref, ga_ref, gb_ref, x_ref,
                  wa_ref, wb_ref, wo_ref, bg_ref, o_ref, s4_ref, s16_ref, sl4_ref, sl16_ref):
    T = og_ref.shape[0]
    a = _dot(og_ref[...], wa_ref[...])
    for d, grp_ref, lse_ref, s_ref, sl_ref in ((4, o4_ref, l4_ref, s4_ref, sl4_ref),
                                                (16, o16_ref, l16_ref, s16_ref, sl16_ref)):
        for r in range(d):
            rows = pl.ds(r, T // d, stride=d)
            sl_ref[rows, :] = lse_ref[r]
            for c in range(DIL_SLOTS):
                s_ref[c, rows, :] = grp_ref[r, :, c * LANES:(c + 1) * LANES].astype(F32)
    lses = (l1_ref[...], sl4_ref[...], sl16_ref[...])
    m = jnp.maximum(jnp.maximum(lses[0], lses[1]), lses[2])
    es = [jnp.exp(l - m) for l in lses]
    inv = 1.0 / (es[0] + es[1] + es[2])
    ws = [e * inv for e in es]
    heads = []
    for c in range(DIL_SLOTS):
        cs = slice(c * DIL_HEAD_DIM, (c + 1) * DIL_HEAD_DIM)
        mix = (ws[0][:, c:c + 1] * o1_ref[:, cs].astype(F32)
               + ws[1][:, c:c + 1] * s4_ref[c]
               + ws[2][:, c:c + 1] * s16_ref[c])
        heads.append(mix.astype(BF16))
    bm = _dot(jnp.concatenate(heads, axis=1), wb_ref[...])
    gate_a = jax.nn.sigmoid(ga_ref[...].astype(F32) + bg_ref[:, :D_MODEL])
    gate_b = jax.nn.sigmoid(gb_ref[...].astype(F32) + bg_ref[:, D_MODEL:])
    merged = (gate_a * a + gate_b * bm).astype(BF16)
    o_ref[...] = x_ref[...] + _dot(merged, wo_ref[...])


def _merge(o_gla, groups, z, gate_col_block, x, wa, wb, wo, b_gate, bm):
    B, S, D = x.shape
    nt = S // bm
    (o1, l1), (o4, l4), (o16, l16) = groups
    const = lambda i: (0, 0)
    resident = dict(pipeline_mode=pl.Buffered(1))

    def natural(width):
        return pl.BlockSpec((None, None, bm, width), lambda i: (i // nt, 0, i % nt, 0))

    def class_major(d, width):
        return pl.BlockSpec((None, d, bm // d, width), lambda i: (i // nt, 0, i % nt, 0))

    out = pl.pallas_call(
        _merge_kernel,
        grid=(B * nt,),
        in_specs=[
            pl.BlockSpec((bm, GLA_V_W), lambda i: (i, 0)),
            natural(DIL_W), class_major(4, DIL_W), class_major(16, DIL_W),
            natural(LANES), class_major(4, LANES), class_major(16, LANES),
            pl.BlockSpec((bm, D_MODEL), lambda i: (i, gate_col_block)),
            pl.BlockSpec((bm, D_MODEL), lambda i: (i, gate_col_block + 1)),
            pl.BlockSpec((bm, D_MODEL), lambda i: (i, 0)),
            pl.BlockSpec((GLA_V_W, D_MODEL), const, **resident),
            pl.BlockSpec((DIL_W, D_MODEL), const, **resident),
            pl.BlockSpec((D_MODEL, D_MODEL), const, **resident),
            pl.BlockSpec((1, 2 * D_MODEL), const),
        ],
        out_specs=pl.BlockSpec((bm, D_MODEL), lambda i: (i, 0)),
        out_shape=jax.ShapeDtypeStruct((B * S, D_MODEL), F32),
        scratch_shapes=[
            pltpu.VMEM((DIL_SLOTS, bm, LANES), F32), pltpu.VMEM((DIL_SLOTS, bm, LANES), F32),
            pltpu.VMEM((bm, LANES), F32), pltpu.VMEM((bm, LANES), F32),
        ],
        compiler_params=_params(1),
        name="merge_out_proj",
    )(o_gla, o1, o4, o16, l1, l4, l16, z, z, x.reshape(B * S, D), wa, wb, wo, b_gate)
    return out


def _ffn_kernel(x_ref, nw_ref, wgu_ref, wd_ref, fw_ref, o_ref, h_ref):
    f = pl.program_id(1)

    @pl.when(f == 0)
    def _():
        x = x_ref[...]
        ms = jnp.mean(x * x, axis=-1, keepdims=True)
        h_ref[...] = (x * lax.rsqrt(ms + RMS_EPS) * nw_ref[...]).astype(BF16)
        o_ref[...] = x

    bf = wd_ref.shape[0]
    gate_up = _dot(h_ref[...], wgu_ref[...])
    gate, up = gate_up[:, :bf], gate_up[:, bf:]
    act = (gate * jax.nn.sigmoid(gate) * up).astype(BF16)
    for c in range(0, D_MODEL, FFN_OUT_CHUNK):
        o_ref[:, c:c + FFN_OUT_CHUNK] += _dot(act, wd_ref[:, c:c + FFN_OUT_CHUNK])

    @pl.when(f == pl.num_programs(1) - 1)
    def _():
        y = o_ref[...]
        ms = jnp.mean(y * y, axis=-1, keepdims=True)
        o_ref[...] = y * lax.rsqrt(ms + RMS_EPS) * fw_ref[...]


FFN_OUT_CHUNK = 512
MERGE_ROWS = 256
FFN_ROWS = 1024
FFN_HIDDEN_BLOCK = 512


def _ffn_weight_moves(bf):
    moves = []
    for f in range(D_FF // bf):
        moves.append((2 * f * bf, f * bf, bf))
        moves.append(((2 * f + 1) * bf, D_FF + f * bf, bf))
    return moves


def _ffn(x1, norm_w, w_in, w_out, final_w, bm, bf):
    M = x1.shape[0]
    nf = D_FF // bf
    return pl.pallas_call(
        _ffn_kernel,
        grid=(M // bm, nf),
        in_specs=[
            pl.BlockSpec((bm, D_MODEL), lambda i, f: (i, 0)),
            pl.BlockSpec((1, D_MODEL), lambda i, f: (0, 0)),
            pl.BlockSpec((D_MODEL, 2 * bf), lambda i, f: (0, f)),
            pl.BlockSpec((bf, D_MODEL), lambda i, f: (f, 0)),
            pl.BlockSpec((1, D_MODEL), lambda i, f: (0, 0)),
        ],
        out_specs=pl.BlockSpec((bm, D_MODEL), lambda i, f: (i, 0)),
        out_shape=jax.ShapeDtypeStruct((M, D_MODEL), F32),
        scratch_shapes=[pltpu.VMEM((bm, D_MODEL), BF16)],
        compiler_params=_params(2),
        name="swiglu_final_norm",
    )(x1, norm_w, w_in, w_out, final_w)


def kernel(x, attn_norm, w_in, w_gk_up, b_gk, gla_norm, b_gate, w_branch_gla, w_branch_dil,
           w_out, ffn_norm, w_ffn_in, w_ffn_out, rel_bias, final_norm):
    B, S, D = x.shape
    assert D == D_MODEL and S % (16 * DIL_BLOCK) == 0 and attn_norm.shape[0] == 1

    o_lr = 2 * GLA_QK_W + 2 * GLA_V_W
    o_dil = o_lr + GLA_LOWRANK
    o_gate = o_dil + 9 * DIL_W
    n_nat = o_lr + 2 * D_MODEL + 3 * DIL_W
    n_dil = 3 * DIL_W
    lr_col = n_nat
    col_d4 = -(-(n_nat + LANES) // n_dil) * n_dil
    col_d16 = col_d4 + n_dil
    gate_col_block = o_lr // D_MODEL
    d1_col = o_lr + 2 * D_MODEL
    w_all = _pack_w_in(
        jnp.swapaxes(w_in, 1, 2)[0],
        segments=((0, 0, o_lr), (o_lr, o_gate, 2 * D_MODEL), (d1_col, o_dil, 3 * DIL_W),
                  (lr_col, o_lr, GLA_LOWRANK), (col_d4, o_dil + 3 * DIL_W, 6 * DIL_W)),
        n_out=col_d16 + 3 * DIL_W)

    an = attn_norm[0][None, :]
    z_nat, h, lr = _in_proj_norm(x, an, w_all, n_nat, n_nat // 4, lr_col // LANES, rows=512)
    z_d4, w_ffn_in_bf = _in_proj_dil(h, w_all, col_d4, n_dil, 4, w_ffn_in[0], _ffn_weight_moves(FFN_HIDDEN_BLOCK))
    z_d16, w_ffn_out_bf = _in_proj_dil(h, w_all, col_d16, n_dil, 16, w_ffn_out[0])
    z2d = z_nat.reshape(B * S, n_nat)
    z_nat = z_nat.reshape(B, 1, S, n_nat)

    wup = jnp.concatenate([w_gk_up[0], jnp.zeros((LANES - GLA_LOWRANK, GLA_QK_W), F32)], axis=0).astype(BF16)
    o_gla = _gla(z2d, lr.reshape(B * S, LANES), wup, b_gk[0][None, :], gla_norm[0][None, :], B, S)

    groups, merge_w = [], []
    for gi, (zg, col_block, w) in enumerate(((z_nat, d1_col // DIL_W, w_branch_gla[0]), (z_d4, 0, w_branch_dil[0]),
                                             (z_d16, 0, w_out[0]))):
        group, w_bf = _dil_attn(zg, col_block, rel_bias, gi, w)
        groups.append(group)
        merge_w.append(w_bf)
    x1 = _merge(o_gla, groups, z2d, gate_col_block, x, merge_w[0], merge_w[1], merge_w[2],
                b_gate[0][None, :], MERGE_ROWS)
    out = _ffn(x1, ffn_norm[0][None, :], w_ffn_in_bf, w_ffn_out_bf, final_norm[None, :],
               FFN_ROWS, FFN_HIDDEN_BLOCK)
    return out.reshape(B, S, D)
```

```python
import functools

import numpy as np
import jax
import jax.numpy as jnp
from jax import lax
from jax.experimental import pallas as pl
from jax.experimental.pallas import tpu as pltpu

F32 = jnp.float32
BF16 = jnp.bfloat16

D_MODEL = 2048
GLA_HEADS = 4
GLA_DK = 256
GLA_DV = 512
GLA_QK_W = GLA_HEADS * GLA_DK
GLA_V_W = GLA_HEADS * GLA_DV
GLA_LOWRANK = 16
GLA_GATE_NORM = 16.0
DIL_DILATIONS = (1, 4, 16)
DIL_SLOTS = 8
DIL_HEAD_DIM = 128
DIL_W = DIL_SLOTS * DIL_HEAD_DIM
DIL_STEPS = 128
DIL_BLOCK = 128
REL_BUCKETS = 32
REL_MAX_DIST = 2048
D_FF = 5632
RMS_EPS = 1e-6
NEG_INF = -1e30
LOG2E = 1.4426950408889634
LN2 = 0.6931471805599453

LANES = 128
SUBLANES = 8
BF16_ROWS = 16
MXU_WIDTH = 256
VMEM_LIMIT = 56 * 1024 * 1024

PACK_COLS = 512
PROJ_NORM_ROWS = 512
PROJ_NORM_COL_TILES = 4
PROJ_ROWS = 1024
PROJ_CHUNK = MXU_WIDTH
GLA_BLOCK = 128
GLA_STEP_ROWS = 512
MERGE_ROWS = 256
FFN_ROWS = 1024
FFN_HIDDEN_BLOCK = 512
FFN_OUT_CHUNK = 512


def _params(n_axes):
    return pltpu.CompilerParams(dimension_semantics=("arbitrary",) * n_axes, vmem_limit_bytes=VMEM_LIMIT)


def _dot(a, b):
    return jnp.dot(a, b, preferred_element_type=F32)


def _dot_t(a, b, dims):
    return lax.dot_general(a, b, (dims, ((), ())), preferred_element_type=F32)


def _split_bf16(x):
    hi = x.astype(BF16)
    lo = (x - hi.astype(F32)).astype(BF16)
    return hi, lo


def _in_proj_norm_kernel(x_ref, g_ref, w_ref, w_lr_ref, o_ref, h_ref, lr_ref):
    @pl.when(pl.program_id(2) == 0)
    def _():
        x = x_ref[...]
        ms = jnp.mean(x * x, axis=-1, keepdims=True)
        h = (x * lax.rsqrt(ms + RMS_EPS) * g_ref[...]).astype(h_ref.dtype)
        h_ref[...] = h
        lr_ref[...] = _dot(h, w_lr_ref[...]).astype(lr_ref.dtype)

    for c in range(0, w_ref.shape[1], PROJ_CHUNK):
        o_ref[:, c:c + PROJ_CHUNK] = _dot(h_ref[...], w_ref[:, c:c + PROJ_CHUNK]).astype(o_ref.dtype)


def _in_proj_norm(x, norm_w, w, n_cols, bn, lr_col_block, rows):
    B, S, D = x.shape
    return pl.pallas_call(
        _in_proj_norm_kernel,
        grid=(B, S // rows, n_cols // bn),
        in_specs=[
            pl.BlockSpec((None, rows, D), lambda b, t, j: (b, t, 0)),
            pl.BlockSpec((1, D), lambda b, t, j: (0, 0)),
            pl.BlockSpec((D, bn), lambda b, t, j: (0, j)),
            pl.BlockSpec((D, LANES), lambda b, t, j: (0, lr_col_block)),
        ],
        out_specs=[
            pl.BlockSpec((None, rows, bn), lambda b, t, j: (b, t, j)),
            pl.BlockSpec((None, rows, D), lambda b, t, j: (b, t, 0)),
            pl.BlockSpec((None, rows, LANES), lambda b, t, j: (b, t, 0)),
        ],
        out_shape=[jax.ShapeDtypeStruct((B, S, n_cols), BF16), jax.ShapeDtypeStruct((B, S, D), BF16),
                   jax.ShapeDtypeStruct((B, S, LANES), BF16)],
        compiler_params=_params(3),
        name="in_proj_norm",
    )(x, norm_w, w, w)


def _in_proj_dil_kernel(h_ref, w_ref, cast_ref, o_ref, cast_out_ref, *scratch, dilation, cast_moves):
    rows = h_ref.shape[0]
    chunk_slabs = PROJ_CHUNK // LANES

    for dst, src, width in cast_moves:
        cast_out_ref[:, dst:dst + width] = cast_ref[:, src:src + width].astype(cast_out_ref.dtype)

    passes, rest = [], dilation
    while rest > 1:
        passes.append(min(rest, 4))
        rest //= passes[-1]

    def deinterleave(res, chunk):
        base = (chunk % 2) * chunk_slabs
        for i in range(chunk_slabs):
            scratch[0][base + i] = res[:, i * LANES:(i + 1) * LANES]
        src_ref, n_groups = scratch[0], 1
        for p, stride in enumerate(passes):
            group_rows = rows // n_groups
            last = p == len(passes) - 1
            for g in range(n_groups):
                for r in range(stride):
                    for i in range(chunk_slabs):
                        part = src_ref[base + i, pl.ds(g * group_rows + r, group_rows // stride, stride=stride), :]
                        if last:
                            col = (chunk * chunk_slabs + i) * LANES
                            o_ref[g + n_groups * r, :, col:col + LANES] = part.astype(o_ref.dtype)
                        else:
                            dst = (g + n_groups * r) * (group_rows // stride)
                            scratch[1][base + i, dst:dst + group_rows // stride, :] = part
            if not last:
                src_ref, n_groups = scratch[1], n_groups * stride

    n_chunks = w_ref.shape[1] // PROJ_CHUNK
    res = _dot(h_ref[...], w_ref[:, :PROJ_CHUNK])
    for i in range(n_chunks):
        nxt = _dot(h_ref[...], w_ref[:, (i + 1) * PROJ_CHUNK:(i + 2) * PROJ_CHUNK]) if i + 1 < n_chunks else None
        deinterleave(res, i)
        res = nxt


def _in_proj_dil(h, w, col0, n_cols, dilation, to_cast, cast_moves=None):
    B, S, D = h.shape
    rows = PROJ_ROWS
    nt = S // rows
    assert col0 % n_cols == 0 and n_cols % PROJ_CHUNK == 0
    cast_rows = to_cast.shape[0] // (B * nt)
    assert cast_rows * B * nt == to_cast.shape[0] and cast_rows % BF16_ROWS == 0
    if cast_moves is None:
        cast_moves = ((0, 0, to_cast.shape[1]),)
    assert sum(m[2] for m in cast_moves) == to_cast.shape[1]
    cast_spec = pl.BlockSpec((cast_rows, to_cast.shape[1]), lambda b, t: (b * nt + t, 0))
    n_slabs = 2 * PROJ_CHUNK // LANES
    scratch = [pltpu.VMEM((n_slabs, rows, LANES), F32)]
    if dilation > 4:
        scratch.append(pltpu.VMEM((n_slabs, rows, LANES), F32))
    return pl.pallas_call(
        functools.partial(_in_proj_dil_kernel, dilation=dilation, cast_moves=tuple(cast_moves)),
        grid=(B, nt),
        in_specs=[
            pl.BlockSpec((None, rows, D), lambda b, t: (b, t, 0)),
            pl.BlockSpec((D, n_cols), lambda b, t: (0, col0 // n_cols), pipeline_mode=pl.Buffered(1)),
            cast_spec,
        ],
        out_specs=[
            pl.BlockSpec((None, dilation, rows // dilation, n_cols), lambda b, t: (b, 0, t, 0)),
            cast_spec,
        ],
        out_shape=[jax.ShapeDtypeStruct((B, dilation, S // dilation, n_cols), BF16),
                   jax.ShapeDtypeStruct(to_cast.shape, BF16)],
        scratch_shapes=scratch,
        compiler_params=_params(2),
        name=f"in_proj_d{dilation}",
    )(h, w, to_cast)


def _pack_w_in_kernel(src_ref, valid_ref, wt_ref, o_ref):
    x = wt_ref[...]
    row = lax.broadcasted_iota(jnp.int32, x.shape, 0)
    x = jnp.where(row < valid_ref[pl.program_id(0)], x, 0.0)
    o_ref[...] = x.T.astype(o_ref.dtype)


def _pack_w_in(wt, segments, n_out):
    N, D = wt.shape
    n_blocks = n_out // PACK_COLS
    src = np.zeros((n_blocks,), np.int32)
    valid = np.zeros((n_blocks,), np.int32)
    for dst, s, width in segments:
        assert dst % PACK_COLS == 0
        for i in range(dst // PACK_COLS, -(-(dst + width) // PACK_COLS)):
            start = s + i * PACK_COLS - dst
            assert start % SUBLANES == 0 and start + PACK_COLS <= N
            src[i] = start // SUBLANES
            valid[i] = min(PACK_COLS, dst + width - i * PACK_COLS)
    return pl.pallas_call(
        _pack_w_in_kernel,
        grid_spec=pltpu.PrefetchScalarGridSpec(
            num_scalar_prefetch=2,
            grid=(n_blocks,),
            in_specs=[pl.BlockSpec((pl.Element(PACK_COLS), pl.Element(D)),
                                   lambda i, src, valid: (src[i] * SUBLANES, 0))],
            out_specs=pl.BlockSpec((D, PACK_COLS), lambda i, src, valid: (0, i)),
        ),
        out_shape=jax.ShapeDtypeStruct((D, n_out), BF16),
        compiler_params=_params(1),
        name="pack_w_in",
    )(jnp.asarray(src), jnp.asarray(valid), wt)


def _gla_kernel(q_ref, k_ref, v_ref, g_ref, lr_ref, wup_ref, bgk_ref, nw_ref, o_ref, s_ref):
    C = GLA_BLOCK

    @pl.when(pl.program_id(1) == 0)
    def _():
        s_ref[...] = jnp.zeros_like(s_ref)

    row = lax.broadcasted_iota(jnp.int32, (C, C), 0)
    col = lax.broadcasted_iota(jnp.int32, (C, C), 1)
    causal = row >= col
    tri = jnp.where(causal, 1.0, 0.0).astype(BF16)
    ones = jnp.ones((C, LANES), BF16)
    scale = GLA_DK ** -0.5
    n_blk = q_ref.shape[0] // C
    heads = range(GLA_HEADS)
    ks = [slice(h * GLA_DK, (h + 1) * GLA_DK) for h in heads]
    vs = [slice(h * GLA_DV, (h + 1) * GLA_DV) for h in heads]
    units = [(blk, h) for blk in range(n_blk) for h in heads]
    rows = {u: slice(u[0] * C, (u[0] + 1) * C) for u in units}
    gks, b_cum, tot, att, q_dec, k_dec = {}, {}, {}, {}, {}, {}
    o_intra, q_in, k_end, decay = {}, {}, {}, {}
    for u in units:
        pre = _dot(lr_ref[rows[u], :], wup_ref[:, ks[u[1]]]) + bgk_ref[:, ks[u[1]]]
        log_sig = jnp.minimum(pre, 0.0) - jnp.log(1.0 + jnp.exp(-jnp.abs(pre)))
        gks[u] = _split_bf16(log_sig * (1.0 / GLA_GATE_NORM))
    for u in units:
        gk_hi, gk_lo = gks[u]
        b_cum[u] = _dot(tri, gk_hi) + _dot(tri, gk_lo)
        tot[u] = _dot_t(gk_hi, ones, ((0,), (0,))) + _dot_t(gk_lo, ones, ((0,), (0,)))
    for u in units:
        b = b_cum[u]
        b_mid = b[C // 2 - 1:C // 2, :]
        q_dec[u] = q_ref[rows[u], ks[u[1]]].astype(F32) * jnp.exp(b - b_mid)
        k_dec[u] = k_ref[rows[u], ks[u[1]]].astype(F32) * jnp.exp(b_mid - b)
        att[u] = _dot_t((q_dec[u] * scale).astype(BF16), k_dec[u].astype(BF16), ((1,), (1,)))
    for u in units:
        b = b_cum[u]
        b_last = b[C - 1:C, :]
        b_mid = b[C // 2 - 1:C // 2, :]
        o_intra[u] = _dot(jnp.where(causal, att[u], 0.0).astype(BF16), v_ref[rows[u], vs[u[1]]])
        q_in[u] = (q_dec[u] * (jnp.exp(b_mid) * scale)).astype(BF16)
        k_end[u] = (k_dec[u] * jnp.exp(b_last - b_mid)).astype(BF16)
        decay[u] = jnp.concatenate([jnp.exp(tot[u])] * (GLA_DV // LANES), axis=1)
    for blk in range(n_blk):
        rs = slice(blk * C, (blk + 1) * C)
        os = []
        for h in heads:
            s_old = s_ref[h]
            os.append(o_intra[blk, h] + _dot(q_in[blk, h], s_old.astype(BF16)))
            s_ref[h] = decay[blk, h] * s_old + _dot_t(k_end[blk, h], v_ref[rs, vs[h]], ((0,), (0,)))
        ys = []
        for h, o in zip(heads, os):
            ms = jnp.mean(o * o, axis=-1, keepdims=True)
            g = g_ref[rs, vs[h]].astype(F32)
            y = (o * lax.rsqrt(ms + RMS_EPS) * nw_ref[...]) * (g * jax.nn.sigmoid(g))
            ys.append(y.astype(o_ref.dtype))
        o_ref[rs, :] = jnp.concatenate(ys, axis=1)


def _gla(z, lr, wup, bgk, norm_w, B, S):
    C = GLA_STEP_ROWS
    nc = S // C
    return pl.pallas_call(
        _gla_kernel,
        grid=(B, nc),
        in_specs=[
            pl.BlockSpec((C, GLA_QK_W), lambda b, c: (b * nc + c, 0)),
            pl.BlockSpec((C, GLA_QK_W), lambda b, c: (b * nc + c, 1)),
            pl.BlockSpec((C, GLA_V_W), lambda b, c: (b * nc + c, 1)),
            pl.BlockSpec((C, GLA_V_W), lambda b, c: (b * nc + c, 2)),
            pl.BlockSpec((C, LANES), lambda b, c: (b * nc + c, 0)),
            pl.BlockSpec((LANES, GLA_QK_W), lambda b, c: (0, 0)),
            pl.BlockSpec((1, GLA_QK_W), lambda b, c: (0, 0)),
            pl.BlockSpec((1, GLA_DV), lambda b, c: (0, 0)),
        ],
        out_specs=pl.BlockSpec((C, GLA_V_W), lambda b, c: (b * nc + c, 0)),
        out_shape=jax.ShapeDtypeStruct((B * S, GLA_V_W), BF16),
        scratch_shapes=[pltpu.VMEM((GLA_HEADS, GLA_DK, GLA_DV), F32)],
        compiler_params=_params(2),
        name="gla",
    )(z, z, z, z, lr, wup, bgk, norm_w)


def _dil_attn_kernel(q_ref, k_ref, v_ref, bucket_ref, table_ref, cast_ref, o_ref, lse_ref, cast_out_ref,
                     bias_ref, *, n_cls, n_blk, buckets, head0):
    scale = DIL_HEAD_DIM ** -0.5
    lane = lax.broadcasted_iota(jnp.int32, (DIL_BLOCK, LANES), 1)
    cast_out_ref[...] = cast_ref[...].astype(cast_out_ref.dtype)

    @pl.when((pl.program_id(0) == 0) & (pl.program_id(1) == 0))
    def _():
        bucket = bucket_ref[...]
        for h in range(DIL_SLOTS):
            bias = jnp.full(bucket.shape, NEG_INF, F32)
            for k in buckets:
                bias = jnp.where(bucket == k, table_ref[k, head0 + h] * LOG2E, bias)
            bias_ref[h] = bias

    def blocks(where, first):
        n_keys = DIL_BLOCK if first else 2 * DIL_BLOCK
        ones = jnp.ones((n_keys, DIL_HEAD_DIM), BF16)
        cols = [slice(h * DIL_HEAD_DIM, (h + 1) * DIL_HEAD_DIM) for h in range(DIL_SLOTS)]
        rows_q = [pl.ds(start, DIL_BLOCK) for _, start in where]
        rows_k = [rq if first else pl.ds(start - DIL_BLOCK, 2 * DIL_BLOCK) for rq, (_, start) in zip(rows_q, where)]
        units = [(u, h) for u in range(len(where)) for h in range(DIL_SLOTS)]
        scores = [
            _dot_t(q_ref[where[u][0], rows_q[u], cols[h]], k_ref[where[u][0], rows_k[u], cols[h]], ((1,), (1,)))
            * (scale * LOG2E) + (bias_ref[h, :, DIL_BLOCK:] if first else bias_ref[h])
            for u, h in units]
        maxes = [jnp.max(s, axis=-1, keepdims=True) for s in scores]
        probs = [jnp.exp2(s - m).astype(BF16) for s, m in zip(scores, maxes)]
        num_ls = [_dot(p, jnp.concatenate([v_ref[where[u][0], rows_k[u], cols[h]], ones], axis=1))
                  for p, (u, h) in zip(probs, units)]
        for u, (c, _) in enumerate(where):
            outs = []
            lse_tile = jnp.zeros((DIL_BLOCK, LANES), F32)
            for h in range(DIL_SLOTS):
                num_l, m = num_ls[u * DIL_SLOTS + h], maxes[u * DIL_SLOTS + h]
                l = num_l[:, DIL_HEAD_DIM:]
                outs.append((num_l[:, :DIL_HEAD_DIM] / l).astype(o_ref.dtype))
                lse_tile = jnp.where(lane == h, m * LN2 + jnp.log(l), lse_tile)
            o_ref[c, rows_q[u], :] = jnp.concatenate(outs, axis=1)
            lse_ref[c, rows_q[u], :] = lse_tile

    def row(n):
        return pl.multiple_of(n * DIL_BLOCK, DIL_BLOCK)

    if n_blk == 1:
        def pair_of_classes(i, carry):
            blocks([(2 * i, 0), (2 * i + 1, 0)], True)
            return carry
        lax.fori_loop(0, n_cls // 2, pair_of_classes, 0)
    else:
        def one_class(c, carry):
            blocks([(c, 0)], True)

            def pair_of_blocks(i, inner):
                blocks([(c, row(2 * i + 1)), (c, row(2 * i + 2))], False)
                return inner
            lax.fori_loop(0, (n_blk - 1) // 2, pair_of_blocks, 0)
            if (n_blk - 1) % 2:
                blocks([(c, (n_blk - 1) * DIL_BLOCK)], False)
            return carry

        if n_cls == 1:
            one_class(0, 0)
        else:
            lax.fori_loop(0, n_cls, one_class, 0)


def _band_buckets(dilation):
    a_idx = np.arange(DIL_BLOCK)[:, None]
    c_idx = np.arange(2 * DIL_BLOCK)[None, :]
    steps = DIL_BLOCK + a_idx - c_idx
    in_band = (steps >= 0) & (steps <= DIL_STEPS)
    bucket = _t5_bucket(np.clip(steps, 0, None) * dilation)
    return np.where(in_band, bucket, -1).astype(np.int32)


def _dil_attn(qkv, col_block, rel_bias, group, to_cast):
    B, d, L, _ = qkv.shape
    n_cls = d
    bucket = _band_buckets(d)
    buckets = tuple(int(k) for k in np.unique(bucket) if k >= 0)
    cast_rows = to_cast.shape[0] // B
    assert cast_rows * B == to_cast.shape[0] and cast_rows % BF16_ROWS == 0
    cast_spec = pl.BlockSpec((cast_rows, to_cast.shape[1]), lambda b, r: (b, 0))

    def part(p):
        return pl.BlockSpec((None, n_cls, L, DIL_W), lambda b, r: (b, r, 0, col_block + p))

    o, lse, cast = pl.pallas_call(
        functools.partial(_dil_attn_kernel, n_cls=n_cls, n_blk=L // DIL_BLOCK, buckets=buckets,
                          head0=group * DIL_SLOTS),
        grid=(B, d // n_cls),
        in_specs=[
            part(0), part(1), part(2),
            pl.BlockSpec((DIL_BLOCK, 2 * DIL_BLOCK), lambda b, r: (0, 0)),
            pl.BlockSpec(memory_space=pltpu.SMEM),
            cast_spec,
        ],
        scratch_shapes=[pltpu.VMEM((DIL_SLOTS, DIL_BLOCK, 2 * DIL_BLOCK), F32)],
        out_specs=[
            pl.BlockSpec((None, n_cls, L, DIL_W), lambda b, r: (b, r, 0, 0)),
            pl.BlockSpec((None, n_cls, L, LANES), lambda b, r: (b, r, 0, 0)),
            cast_spec,
        ],
        out_shape=[
            jax.ShapeDtypeStruct((B, d, L, DIL_W), BF16),
            jax.ShapeDtypeStruct((B, d, L, LANES), F32),
            jax.ShapeDtypeStruct(to_cast.shape, BF16),
        ],
        compiler_params=_params(2),
        name=f"dil_attn_d{d}",
    )(qkv, qkv, qkv, jnp.asarray(bucket), rel_bias, to_cast)
    return (o, lse), cast


def _t5_bucket(dist):
    max_exact = REL_BUCKETS // 2
    d = np.maximum(dist, 1).astype(np.float64)
    large = max_exact + (np.log(d / max_exact) / np.log(REL_MAX_DIST / max_exact)
                         * (REL_BUCKETS - max_exact)).astype(np.int64)
    large = np.minimum(large, REL_BUCKETS - 1)
    return np.where(dist < max_exact, dist, large).astype(np.int32)


def _merge_kernel(og_ref, o1_ref, o4_ref, o16_ref, l1_ref, l4_ref, l16_ref, ga_ref, gb_ref, x_ref,
                  wa_ref, wb_ref, wo_ref, bg_ref, o_ref, s4_ref, s16_ref, sl4_ref, sl16_ref):
    T = og_ref.shape[0]
    a = _dot(og_ref[...], wa_ref[...])
    for d, grp_ref, lse_ref, s_ref, sl_ref in ((4, o4_ref, l4_ref, s4_ref, sl4_ref),
                                                (16, o16_ref, l16_ref, s16_ref, sl16_ref)):
        for r in range(d):
            rows = pl.ds(r, T // d, stride=d)
            sl_ref[rows, :] = lse_ref[r]
            for c in range(DIL_SLOTS):
                s_ref[c, rows, :] = grp_ref[r, :, c * LANES:(c + 1) * LANES].astype(F32)
    lses = (l1_ref[...], sl4_ref[...], sl16_ref[...])
    m = jnp.maximum(jnp.maximum(lses[0], lses[1]), lses[2])
    es = [jnp.exp(l - m) for l in lses]
    inv = 1.0 / (es[0] + es[1] + es[2])
    ws = [e * inv for e in es]
    heads = []
    for c in range(DIL_SLOTS):
        cs = slice(c * DIL_HEAD_DIM, (c + 1) * DIL_HEAD_DIM)
        mix = (ws[0][:, c:c + 1] * o1_ref[:, cs].astype(F32)
               + ws[1][:, c:c + 1] * s4_ref[c]
               + ws[2][:, c:c + 1] * s16_ref[c])
        heads.append(mix.astype(BF16))
    bm = _dot(jnp.concatenate(heads, axis=1), wb_ref[...])
    gate_a = jax.nn.sigmoid(ga_ref[...].astype(F32) + bg_ref[:, :D_MODEL])
    gate_b = jax.nn.sigmoid(gb_ref[...].astype(F32) + bg_ref[:, D_MODEL:])
    merged = (gate_a * a + gate_b * bm).astype(BF16)
    o_ref[...] = x_ref[...] + _dot(merged, wo_ref[...])


def _merge(o_gla, groups, z, gate_col_block, x, wa, wb, wo, b_gate, bm):
    B, S, D = x.shape
    nt = S // bm
    (o1, l1), (o4, l4), (o16, l16) = groups
    const = lambda i: (0, 0)
    resident = dict(pipeline_mode=pl.Buffered(1))

    def natural(width):
        return pl.BlockSpec((None, None, bm, width), lambda i: (i // nt, 0, i % nt, 0))

    def class_major(d, width):
        return pl.BlockSpec((None, d, bm // d, width), lambda i: (i // nt, 0, i % nt, 0))

    out = pl.pallas_call(
        _merge_kernel,
        grid=(B * nt,),
        in_specs=[
            pl.BlockSpec((bm, GLA_V_W), lambda i: (i, 0)),
            natural(DIL_W), class_major(4, DIL_W), class_major(16, DIL_W),
            natural(LANES), class_major(4, LANES), class_major(16, LANES),
            pl.BlockSpec((bm, D_MODEL), lambda i: (i, gate_col_block)),
            pl.BlockSpec((bm, D_MODEL), lambda i: (i, gate_col_block + 1)),
            pl.BlockSpec((bm, D_MODEL), lambda i: (i, 0)),
            pl.BlockSpec((GLA_V_W, D_MODEL), const, **resident),
            pl.BlockSpec((DIL_W, D_MODEL), const, **resident),
            pl.BlockSpec((D_MODEL, D_MODEL), const, **resident),
            pl.BlockSpec((1, 2 * D_MODEL), const),
        ],
        out_specs=pl.BlockSpec((bm, D_MODEL), lambda i: (i, 0)),
        out_shape=jax.ShapeDtypeStruct((B * S, D_MODEL), F32),
        scratch_shapes=[
            pltpu.VMEM((DIL_SLOTS, bm, LANES), F32), pltpu.VMEM((DIL_SLOTS, bm, LANES), F32),
            pltpu.VMEM((bm, LANES), F32), pltpu.VMEM((bm, LANES), F32),
        ],
        compiler_params=_params(1),
        name="merge_out_proj",
    )(o_gla, o1, o4, o16, l1, l4, l16, z, z, x.reshape(B * S, D), wa, wb, wo, b_gate)
    return out


def _ffn_kernel(x_ref, nw_ref, wgu_ref, wd_ref, fw_ref, o_ref, h_ref):
    f = pl.program_id(1)

    @pl.when(f == 0)
    def _():
        x = x_ref[...]
        ms = jnp.mean(x * x, axis=-1, keepdims=True)
        h_ref[...] = (x * lax.rsqrt(ms + RMS_EPS) * nw_ref[...]).astype(BF16)
        o_ref[...] = x

    bf = wd_ref.shape[0]
    gate_up = _dot(h_ref[...], wgu_ref[...])
    gate, up = gate_up[:, :bf], gate_up[:, bf:]
    act = (gate * jax.nn.sigmoid(gate) * up).astype(BF16)
    for c in range(0, D_MODEL, FFN_OUT_CHUNK):
        o_ref[:, c:c + FFN_OUT_CHUNK] += _dot(act, wd_ref[:, c:c + FFN_OUT_CHUNK])

    @pl.when(f == pl.num_programs(1) - 1)
    def _():
        y = o_ref[...]
        ms = jnp.mean(y * y, axis=-1, keepdims=True)
        o_ref[...] = y * lax.rsqrt(ms + RMS_EPS) * fw_ref[...]


def _ffn_weight_moves(bf):
    moves = []
    for f in range(D_FF // bf):
        moves.append((2 * f * bf, f * bf, bf))
        moves.append(((2 * f + 1) * bf, D_FF + f * bf, bf))
    return moves


def _ffn(x1, norm_w, w_in, w_out, final_w, bm, bf):
    M = x1.shape[0]
    nf = D_FF // bf
    return pl.pallas_call(
        _ffn_kernel,
        grid=(M // bm, nf),
        in_specs=[
            pl.BlockSpec((bm, D_MODEL), lambda i, f: (i, 0)),
            pl.BlockSpec((1, D_MODEL), lambda i, f: (0, 0)),
            pl.BlockSpec((D_MODEL, 2 * bf), lambda i, f: (0, f)),
            pl.BlockSpec((bf, D_MODEL), lambda i, f: (f, 0)),
            pl.BlockSpec((1, D_MODEL), lambda i, f: (0, 0)),
        ],
        out_specs=pl.BlockSpec((bm, D_MODEL), lambda i, f: (i, 0)),
        out_shape=jax.ShapeDtypeStruct((M, D_MODEL), F32),
        scratch_shapes=[pltpu.VMEM((bm, D_MODEL), BF16)],
        compiler_params=_params(2),
        name="swiglu_final_norm",
    )(x1, norm_w, w_in, w_out, final_w)


def kernel(x, attn_norm, w_in, w_gk_up, b_gk, gla_norm, b_gate, w_branch_gla, w_branch_dil,
           w_out, ffn_norm, w_ffn_in, w_ffn_out, rel_bias, final_norm):
    B, S, D = x.shape
    assert D == D_MODEL and S % (16 * DIL_BLOCK) == 0 and attn_norm.shape[0] == 1

    o_lr = 2 * GLA_QK_W + 2 * GLA_V_W
    o_dil = o_lr + GLA_LOWRANK
    o_gate = o_dil + 9 * DIL_W
    n_nat = o_lr + 2 * D_MODEL + 3 * DIL_W
    n_dil = 3 * DIL_W
    lr_col = n_nat
    col_d4 = -(-(n_nat + LANES) // n_dil) * n_dil
    col_d16 = col_d4 + n_dil
    gate_col_block = o_lr // D_MODEL
    d1_col = o_lr + 2 * D_MODEL
    w_all = _pack_w_in(
        jnp.swapaxes(w_in, 1, 2)[0],
        segments=((0, 0, o_lr), (o_lr, o_gate, 2 * D_MODEL), (d1_col, o_dil, 3 * DIL_W),
                  (lr_col, o_lr, GLA_LOWRANK), (col_d4, o_dil + 3 * DIL_W, 6 * DIL_W)),
        n_out=col_d16 + 3 * DIL_W)

    an = attn_norm[0][None, :]
    z_nat, h, lr = _in_proj_norm(x, an, w_all, n_nat, n_nat // PROJ_NORM_COL_TILES, lr_col // LANES,
                                 PROJ_NORM_ROWS)
    z_d4, w_ffn_in_bf = _in_proj_dil(h, w_all, col_d4, n_dil, 4, w_ffn_in[0], _ffn_weight_moves(FFN_HIDDEN_BLOCK))
    z_d16, w_ffn_out_bf = _in_proj_dil(h, w_all, col_d16, n_dil, 16, w_ffn_out[0])
    z2d = z_nat.reshape(B * S, n_nat)
    z_nat = z_nat.reshape(B, 1, S, n_nat)

    wup = jnp.concatenate([w_gk_up[0], jnp.zeros((LANES - GLA_LOWRANK, GLA_QK_W), F32)], axis=0).astype(BF16)
    o_gla = _gla(z2d, lr.reshape(B * S, LANES), wup, b_gk[0][None, :], gla_norm[0][None, :], B, S)

    groups, merge_w = [], []
    for gi, (zg, col_block, w) in enumerate(((z_nat, d1_col // DIL_W, w_branch_gla[0]), (z_d4, 0, w_branch_dil[0]),
                                             (z_d16, 0, w_out[0]))):
        group, w_bf = _dil_attn(zg, col_block, rel_bias, gi, w)
        groups.append(group)
        merge_w.append(w_bf)
    x1 = _merge(o_gla, groups, z2d, gate_col_block, x, merge_w[0], merge_w[1], merge_w[2],
                b_gate[0][None, :], MERGE_ROWS)
    out = _ffn(x1, ffn_norm[0][None, :], w_ffn_in_bf, w_ffn_out_bf, final_norm[None, :],
               FFN_ROWS, FFN_HIDDEN_BLOCK)
    return out.reshape(B, S, D)
```

```python
import functools

import numpy as np
import jax
import jax.numpy as jnp
from jax import lax
from jax.experimental import pallas as pl
from jax.experimental.pallas import tpu as pltpu

F32 = jnp.float32
BF16 = jnp.bfloat16

D_MODEL = 2048
GLA_HEADS = 4
GLA_DK = 256
GLA_DV = 512
GLA_QK_W = GLA_HEADS * GLA_DK
GLA_V_W = GLA_HEADS * GLA_DV
GLA_LOWRANK = 16
GLA_GATE_NORM = 16.0
DIL_DILATIONS = (1, 4, 16)
DIL_SLOTS = 8
DIL_HEAD_DIM = 128
DIL_W = DIL_SLOTS * DIL_HEAD_DIM
DIL_STEPS = 128
DIL_BLOCK = 128
REL_BUCKETS = 32
REL_MAX_DIST = 2048
D_FF = 5632
RMS_EPS = 1e-6
NEG_INF = -1e30
LOG2E = 1.4426950408889634
LN2 = 0.6931471805599453

LANES = 128
SUBLANES = 8
BF16_ROWS = 16
MXU_WIDTH = 256
VMEM_LIMIT = 56 * 1024 * 1024

PACK_COLS = 512
PROJ_NAT_ROWS = 512
PROJ_ROWS = 1024
PROJ_CHUNK = MXU_WIDTH
GLA_BLOCK = 128
GLA_STEP_ROWS = 512
MERGE_ROWS = 256
FFN_ROWS = 1024
FFN_HIDDEN_BLOCK = 512
FFN_OUT_CHUNK = 512


def _params(n_axes):
    return pltpu.CompilerParams(dimension_semantics=("arbitrary",) * n_axes, vmem_limit_bytes=VMEM_LIMIT)


def _dot(a, b):
    return jnp.dot(a, b, preferred_element_type=F32)


def _dot_t(a, b, dims):
    return lax.dot_general(a, b, (dims, ((), ())), preferred_element_type=F32)


def _split_bf16(x):
    hi = x.astype(BF16)
    lo = (x - hi.astype(F32)).astype(BF16)
    return hi, lo


def _in_proj_norm_kernel(x_ref, g_ref, w_ref, w_lr_ref, o_ref, h_ref, lr_ref):
    x = x_ref[...]
    ms = jnp.mean(x * x, axis=-1, keepdims=True)
    h_ref[...] = (x * lax.rsqrt(ms + RMS_EPS) * g_ref[...]).astype(h_ref.dtype)
    lr_ref[...] = _dot(h_ref[...], w_lr_ref[...]).astype(lr_ref.dtype)
    for c in range(0, w_ref.shape[1], PROJ_CHUNK):
        o_ref[:, c:c + PROJ_CHUNK] = _dot(h_ref[...], w_ref[:, c:c + PROJ_CHUNK]).astype(o_ref.dtype)


def _in_proj_norm(x, norm_w, w, n_cols, lr_col_block):
    B, S, D = x.shape
    rows = PROJ_NAT_ROWS
    resident = dict(pipeline_mode=pl.Buffered(1))
    return pl.pallas_call(
        _in_proj_norm_kernel,
        grid=(B, S // rows),
        in_specs=[
            pl.BlockSpec((None, rows, D), lambda b, t: (b, t, 0)),
            pl.BlockSpec((1, D), lambda b, t: (0, 0)),
            pl.BlockSpec((D, n_cols), lambda b, t: (0, 0), **resident),
            pl.BlockSpec((D, LANES), lambda b, t: (0, lr_col_block), **resident),
        ],
        out_specs=[
            pl.BlockSpec((None, rows, n_cols), lambda b, t: (b, t, 0)),
            pl.BlockSpec((None, rows, D), lambda b, t: (b, t, 0)),
            pl.BlockSpec((None, rows, LANES), lambda b, t: (b, t, 0)),
        ],
        out_shape=[jax.ShapeDtypeStruct((B, S, n_cols), BF16), jax.ShapeDtypeStruct((B, S, D), BF16),
                   jax.ShapeDtypeStruct((B, S, LANES), BF16)],
        compiler_params=_params(2),
        name="in_proj_norm",
    )(x, norm_w, w, w)


def _in_proj_kernel(h_ref, w_ref, o_ref):
    for c in range(0, w_ref.shape[1], PROJ_CHUNK):
        o_ref[:, c:c + PROJ_CHUNK] = _dot(h_ref[...], w_ref[:, c:c + PROJ_CHUNK]).astype(o_ref.dtype)


def _in_proj(h, w, col0, n_cols):
    B, S, D = h.shape
    rows = PROJ_NAT_ROWS
    assert col0 % n_cols == 0
    return pl.pallas_call(
        _in_proj_kernel,
        grid=(B, S // rows),
        in_specs=[
            pl.BlockSpec((None, rows, D), lambda b, t: (b, t, 0)),
            pl.BlockSpec((D, n_cols), lambda b, t: (0, col0 // n_cols), pipeline_mode=pl.Buffered(1)),
        ],
        out_specs=pl.BlockSpec((None, rows, n_cols), lambda b, t: (b, t, 0)),
        out_shape=jax.ShapeDtypeStruct((B, S, n_cols), BF16),
        compiler_params=_params(2),
        name="in_proj_rest",
    )(h, w)


def _in_proj_dil_kernel(h_ref, w_ref, cast_ref, o_ref, cast_out_ref, *scratch, dilation, cast_moves):
    rows = h_ref.shape[0]
    chunk_slabs = PROJ_CHUNK // LANES

    for dst, src, width in cast_moves:
        cast_out_ref[:, dst:dst + width] = cast_ref[:, src:src + width].astype(cast_out_ref.dtype)

    passes, rest = [], dilation
    while rest > 1:
        passes.append(min(rest, 4))
        rest //= passes[-1]

    def deinterleave(res, chunk):
        base = (chunk % 2) * chunk_slabs
        for i in range(chunk_slabs):
            scratch[0][base + i] = res[:, i * LANES:(i + 1) * LANES]
        src_ref, n_groups = scratch[0], 1
        for p, stride in enumerate(passes):
            group_rows = rows // n_groups
            last = p == len(passes) - 1
            for g in range(n_groups):
                for r in range(stride):
                    for i in range(chunk_slabs):
                        part = src_ref[base + i, pl.ds(g * group_rows + r, group_rows // stride, stride=stride), :]
                        if last:
                            col = (chunk * chunk_slabs + i) * LANES
                            o_ref[g + n_groups * r, :, col:col + LANES] = part.astype(o_ref.dtype)
                        else:
                            dst = (g + n_groups * r) * (group_rows // stride)
                            scratch[1][base + i, dst:dst + group_rows // stride, :] = part
            if not last:
                src_ref, n_groups = scratch[1], n_groups * stride

    n_chunks = w_ref.shape[1] // PROJ_CHUNK
    res = _dot(h_ref[...], w_ref[:, :PROJ_CHUNK])
    for i in range(n_chunks):
        nxt = _dot(h_ref[...], w_ref[:, (i + 1) * PROJ_CHUNK:(i + 2) * PROJ_CHUNK]) if i + 1 < n_chunks else None
        deinterleave(res, i)
        res = nxt


def _in_proj_dil(h, w, col0, n_cols, dilation, to_cast, cast_moves=None):
    B, S, D = h.shape
    rows = PROJ_ROWS
    nt = S // rows
    assert col0 % n_cols == 0 and n_cols % PROJ_CHUNK == 0
    cast_rows = to_cast.shape[0] // (B * nt)
    assert cast_rows * B * nt == to_cast.shape[0] and cast_rows % BF16_ROWS == 0
    if cast_moves is None:
        cast_moves = ((0, 0, to_cast.shape[1]),)
    assert sum(m[2] for m in cast_moves) == to_cast.shape[1]
    cast_spec = pl.BlockSpec((cast_rows, to_cast.shape[1]), lambda b, t: (b * nt + t, 0))
    n_slabs = 2 * PROJ_CHUNK // LANES
    scratch = [pltpu.VMEM((n_slabs, rows, LANES), F32)]
    if dilation > 4:
        scratch.append(pltpu.VMEM((n_slabs, rows, LANES), F32))
    return pl.pallas_call(
        functools.partial(_in_proj_dil_kernel, dilation=dilation, cast_moves=tuple(cast_moves)),
        grid=(B, nt),
        in_specs=[
            pl.BlockSpec((None, rows, D), lambda b, t: (b, t, 0)),
            pl.BlockSpec((D, n_cols), lambda b, t: (0, col0 // n_cols), pipeline_mode=pl.Buffered(1)),
            cast_spec,
        ],
        out_specs=[
            pl.BlockSpec((None, dilation, rows // dilation, n_cols), lambda b, t: (b, 0, t, 0)),
            cast_spec,
        ],
        out_shape=[jax.ShapeDtypeStruct((B, dilation, S // dilation, n_cols), BF16),
                   jax.ShapeDtypeStruct(to_cast.shape, BF16)],
        scratch_shapes=scratch,
        compiler_params=_params(2),
        name=f"in_proj_d{dilation}",
    )(h, w, to_cast)


def _pack_w_in_kernel(src_ref, valid_ref, wt_ref, o_ref):
    x = wt_ref[...]
    row = lax.broadcasted_iota(jnp.int32, x.shape, 0)
    x = jnp.where(row < valid_ref[pl.program_id(0)], x, 0.0)
    o_ref[...] = x.T.astype(o_ref.dtype)


def _pack_w_in(wt, segments, n_out):
    N, D = wt.shape
    n_blocks = n_out // PACK_COLS
    src = np.zeros((n_blocks,), np.int32)
    valid = np.zeros((n_blocks,), np.int32)
    for dst, s, width in segments:
        assert dst % PACK_COLS == 0
        for i in range(dst // PACK_COLS, -(-(dst + width) // PACK_COLS)):
            start = s + i * PACK_COLS - dst
            assert start % SUBLANES == 0 and start + PACK_COLS <= N
            src[i] = start // SUBLANES
            valid[i] = min(PACK_COLS, dst + width - i * PACK_COLS)
    return pl.pallas_call(
        _pack_w_in_kernel,
        grid_spec=pltpu.PrefetchScalarGridSpec(
            num_scalar_prefetch=2,
            grid=(n_blocks,),
            in_specs=[pl.BlockSpec((pl.Element(PACK_COLS), pl.Element(D)),
                                   lambda i, src, valid: (src[i] * SUBLANES, 0))],
            out_specs=pl.BlockSpec((D, PACK_COLS), lambda i, src, valid: (0, i)),
        ),
        out_shape=jax.ShapeDtypeStruct((D, n_out), BF16),
        compiler_params=_params(1),
        name="pack_w_in",
    )(jnp.asarray(src), jnp.asarray(valid), wt)


def _gla_kernel(q_ref, k_ref, v_ref, g_ref, lr_ref, wup_ref, bgk_ref, nw_ref, o_ref, s_ref):
    C = GLA_BLOCK

    @pl.when(pl.program_id(1) == 0)
    def _():
        s_ref[...] = jnp.zeros_like(s_ref)

    row = lax.broadcasted_iota(jnp.int32, (C, C), 0)
    col = lax.broadcasted_iota(jnp.int32, (C, C), 1)
    causal = row >= col
    tri = jnp.where(causal, 1.0, 0.0).astype(BF16)
    ones = jnp.ones((C, LANES), BF16)
    scale = GLA_DK ** -0.5
    n_blk = q_ref.shape[0] // C
    heads = range(GLA_HEADS)
    ks = [slice(h * GLA_DK, (h + 1) * GLA_DK) for h in heads]
    vs = [slice(h * GLA_DV, (h + 1) * GLA_DV) for h in heads]
    units = [(blk, h) for blk in range(n_blk) for h in heads]
    rows = {u: slice(u[0] * C, (u[0] + 1) * C) for u in units}
    gks, b_cum, tot, att, q_dec, k_dec = {}, {}, {}, {}, {}, {}
    o_intra, q_in, k_end, decay = {}, {}, {}, {}
    for u in units:
        pre = _dot(lr_ref[rows[u], :], wup_ref[:, ks[u[1]]]) + bgk_ref[:, ks[u[1]]]
        log_sig = jnp.minimum(pre, 0.0) - jnp.log(1.0 + jnp.exp(-jnp.abs(pre)))
        gks[u] = _split_bf16(log_sig * (1.0 / GLA_GATE_NORM))
    for u in units:
        gk_hi, gk_lo = gks[u]
        b_cum[u] = _dot(tri, gk_hi) + _dot(tri, gk_lo)
        tot[u] = _dot_t(gk_hi, ones, ((0,), (0,))) + _dot_t(gk_lo, ones, ((0,), (0,)))
    for u in units:
        b = b_cum[u]
        b_mid = b[C // 2 - 1:C // 2, :]
        q_dec[u] = q_ref[rows[u], ks[u[1]]].astype(F32) * jnp.exp(b - b_mid)
        k_dec[u] = k_ref[rows[u], ks[u[1]]].astype(F32) * jnp.exp(b_mid - b)
        att[u] = _dot_t((q_dec[u] * scale).astype(BF16), k_dec[u].astype(BF16), ((1,), (1,)))
    for u in units:
        b = b_cum[u]
        b_last = b[C - 1:C, :]
        b_mid = b[C // 2 - 1:C // 2, :]
        o_intra[u] = _dot(jnp.where(causal, att[u], 0.0).astype(BF16), v_ref[rows[u], vs[u[1]]])
        q_in[u] = (q_dec[u] * (jnp.exp(b_mid) * scale)).astype(BF16)
        k_end[u] = (k_dec[u] * jnp.exp(b_last - b_mid)).astype(BF16)
        decay[u] = jnp.concatenate([jnp.exp(tot[u])] * (GLA_DV // LANES), axis=1)
    for blk in range(n_blk):
        rs = slice(blk * C, (blk + 1) * C)
        os = []
        for h in heads:
            s_old = s_ref[h]
            os.append(o_intra[blk, h] + _dot(q_in[blk, h], s_old.astype(BF16)))
            s_ref[h] = decay[blk, h] * s_old + _dot_t(k_end[blk, h], v_ref[rs, vs[h]], ((0,), (0,)))
        ys = []
        for h, o in zip(heads, os):
            ms = jnp.mean(o * o, axis=-1, keepdims=True)
            g = g_ref[rs, vs[h]].astype(F32)
            y = (o * lax.rsqrt(ms + RMS_EPS) * nw_ref[...]) * (g * jax.nn.sigmoid(g))
            ys.append(y.astype(o_ref.dtype))
        o_ref[rs, :] = jnp.concatenate(ys, axis=1)


def _gla(z, lr, wup, bgk, norm_w, B, S):
    C = GLA_STEP_ROWS
    nc = S // C
    return pl.pallas_call(
        _gla_kernel,
        grid=(B, nc),
        in_specs=[
            pl.BlockSpec((C, GLA_QK_W), lambda b, c: (b * nc + c, 0)),
            pl.BlockSpec((C, GLA_QK_W), lambda b, c: (b * nc + c, 1)),
            pl.BlockSpec((C, GLA_V_W), lambda b, c: (b * nc + c, 1)),
            pl.BlockSpec((C, GLA_V_W), lambda b, c: (b * nc + c, 2)),
            pl.BlockSpec((C, LANES), lambda b, c: (b * nc + c, 0)),
            pl.BlockSpec((LANES, GLA_QK_W), lambda b, c: (0, 0)),
            pl.BlockSpec((1, GLA_QK_W), lambda b, c: (0, 0)),
            pl.BlockSpec((1, GLA_DV), lambda b, c: (0, 0)),
        ],
        out_specs=pl.BlockSpec((C, GLA_V_W), lambda b, c: (b * nc + c, 0)),
        out_shape=jax.ShapeDtypeStruct((B * S, GLA_V_W), BF16),
        scratch_shapes=[pltpu.VMEM((GLA_HEADS, GLA_DK, GLA_DV), F32)],
        compiler_params=_params(2),
        name="gla",
    )(z, z, z, z, lr, wup, bgk, norm_w)


def _dil_attn_kernel(q_ref, k_ref, v_ref, bucket_ref, table_ref, cast_ref, o_ref, lse_ref, cast_out_ref,
                     bias_ref, *, n_cls, n_blk, buckets, head0):
    scale = DIL_HEAD_DIM ** -0.5
    lane = lax.broadcasted_iota(jnp.int32, (DIL_BLOCK, LANES), 1)
    cast_out_ref[...] = cast_ref[...].astype(cast_out_ref.dtype)

    @pl.when((pl.program_id(0) == 0) & (pl.program_id(1) == 0))
    def _():
        bucket = bucket_ref[...]
        for h in range(DIL_SLOTS):
            bias = jnp.full(bucket.shape, NEG_INF, F32)
            for k in buckets:
                bias = jnp.where(bucket == k, table_ref[k, head0 + h] * LOG2E, bias)
            bias_ref[h] = bias

    def blocks(where, first):
        n_keys = DIL_BLOCK if first else 2 * DIL_BLOCK
        ones = jnp.ones((n_keys, DIL_HEAD_DIM), BF16)
        cols = [slice(h * DIL_HEAD_DIM, (h + 1) * DIL_HEAD_DIM) for h in range(DIL_SLOTS)]
        rows_q = [pl.ds(start, DIL_BLOCK) for _, start in where]
        rows_k = [rq if first else pl.ds(start - DIL_BLOCK, 2 * DIL_BLOCK) for rq, (_, start) in zip(rows_q, where)]
        units = [(u, h) for u in range(len(where)) for h in range(DIL_SLOTS)]
        scores = [
            _dot_t(q_ref[where[u][0], rows_q[u], cols[h]], k_ref[where[u][0], rows_k[u], cols[h]], ((1,), (1,)))
            * (scale * LOG2E) + (bias_ref[h, :, DIL_BLOCK:] if first else bias_ref[h])
            for u, h in units]
        maxes = [jnp.max(s, axis=-1, keepdims=True) for s in scores]
        probs = [jnp.exp2(s - m).astype(BF16) for s, m in zip(scores, maxes)]
        num_ls = [_dot(p, jnp.concatenate([v_ref[where[u][0], rows_k[u], cols[h]], ones], axis=1))
                  for p, (u, h) in zip(probs, units)]
        for u, (c, _) in enumerate(where):
            outs = []
            lse_tile = jnp.zeros((DIL_BLOCK, LANES), F32)
            for h in range(DIL_SLOTS):
                num_l, m = num_ls[u * DIL_SLOTS + h], maxes[u * DIL_SLOTS + h]
                l = num_l[:, DIL_HEAD_DIM:]
                outs.append((num_l[:, :DIL_HEAD_DIM] / l).astype(o_ref.dtype))
                lse_tile = jnp.where(lane == h, m * LN2 + jnp.log(l), lse_tile)
            o_ref[c, rows_q[u], :] = jnp.concatenate(outs, axis=1)
            lse_ref[c, rows_q[u], :] = lse_tile

    def row(n):
        return pl.multiple_of(n * DIL_BLOCK, DIL_BLOCK)

    if n_blk == 1:
        def pair_of_classes(i, carry):
            blocks([(2 * i, 0), (2 * i + 1, 0)], True)
            return carry
        lax.fori_loop(0, n_cls // 2, pair_of_classes, 0)
    else:
        def one_class(c, carry):
            blocks([(c, 0)], True)

            def pair_of_blocks(i, inner):
                blocks([(c, row(2 * i + 1)), (c, row(2 * i + 2))], False)
                return inner
            lax.fori_loop(0, (n_blk - 1) // 2, pair_of_blocks, 0)
            if (n_blk - 1) % 2:
                blocks([(c, (n_blk - 1) * DIL_BLOCK)], False)
            return carry

        if n_cls == 1:
            one_class(0, 0)
        else:
            lax.fori_loop(0, n_cls, one_class, 0)


def _band_buckets(dilation):
    a_idx = np.arange(DIL_BLOCK)[:, None]
    c_idx = np.arange(2 * DIL_BLOCK)[None, :]
    steps = DIL_BLOCK + a_idx - c_idx
    in_band = (steps >= 0) & (steps <= DIL_STEPS)
    bucket = _t5_bucket(np.clip(steps, 0, None) * dilation)
    return np.where(in_band, bucket, -1).astype(np.int32)


def _dil_attn(qkv, col_block, rel_bias, group, to_cast):
    B, d, L, _ = qkv.shape
    n_cls = d
    bucket = _band_buckets(d)
    buckets = tuple(int(k) for k in np.unique(bucket) if k >= 0)
    cast_rows = to_cast.shape[0] // B
    assert cast_rows * B == to_cast.shape[0] and cast_rows % BF16_ROWS == 0
    cast_spec = pl.BlockSpec((cast_rows, to_cast.shape[1]), lambda b, r: (b, 0))

    def part(p):
        return pl.BlockSpec((None, n_cls, L, DIL_W), lambda b, r: (b, r, 0, col_block + p))

    o, lse, cast = pl.pallas_call(
        functools.partial(_dil_attn_kernel, n_cls=n_cls, n_blk=L // DIL_BLOCK, buckets=buckets,
                          head0=group * DIL_SLOTS),
        grid=(B, d // n_cls),
        in_specs=[
            part(0), part(1), part(2),
            pl.BlockSpec((DIL_BLOCK, 2 * DIL_BLOCK), lambda b, r: (0, 0)),
            pl.BlockSpec(memory_space=pltpu.SMEM),
            cast_spec,
        ],
        scratch_shapes=[pltpu.VMEM((DIL_SLOTS, DIL_BLOCK, 2 * DIL_BLOCK), F32)],
        out_specs=[
            pl.BlockSpec((None, n_cls, L, DIL_W), lambda b, r: (b, r, 0, 0)),
            pl.BlockSpec((None, n_cls, L, LANES), lambda b, r: (b, r, 0, 0)),
            cast_spec,
        ],
        out_shape=[
            jax.ShapeDtypeStruct((B, d, L, DIL_W), BF16),
            jax.ShapeDtypeStruct((B, d, L, LANES), F32),
            jax.ShapeDtypeStruct(to_cast.shape, BF16),
        ],
        compiler_params=_params(2),
        name=f"dil_attn_d{d}",
    )(qkv, qkv, qkv, jnp.asarray(bucket), rel_bias, to_cast)
    return (o, lse), cast


def _t5_bucket(dist):
    max_exact = REL_BUCKETS // 2
    d = np.maximum(dist, 1).astype(np.float64)
    large = max_exact + (np.log(d / max_exact) / np.log(REL_MAX_DIST / max_exact)
                         * (REL_BUCKETS - max_exact)).astype(np.int64)
    large = np.minimum(large, REL_BUCKETS - 1)
    return np.where(dist < max_exact, dist, large).astype(np.int32)


def _merge_kernel(og_ref, o1_ref, o4_ref, o16_ref, l1_ref, l4_ref, l16_ref, ga_ref, gb_ref, x_ref,
                  wa_ref, wb_ref, wo_ref, bg_ref, o_ref, s4_ref, s16_ref, sl4_ref, sl16_ref):
    T = og_ref.shape[0]
    a = _dot(og_ref[...], wa_ref[...])
    for d, grp_ref, lse_ref, s_ref, sl_ref in ((4, o4_ref, l4_ref, s4_ref, sl4_ref),
                                                (16, o16_ref, l16_ref, s16_ref, sl16_ref)):
        for r in range(d):
            rows = pl.ds(r, T // d, stride=d)
            sl_ref[rows, :] = lse_ref[r]
            for c in range(DIL_SLOTS):
                s_ref[c, rows, :] = grp_ref[r, :, c * LANES:(c + 1) * LANES].astype(F32)
    lses = (l1_ref[...], sl4_ref[...], sl16_ref[...])
    m = jnp.maximum(jnp.maximum(lses[0], lses[1]), lses[2])
    es = [jnp.exp(l - m) for l in lses]
    inv = 1.0 / (es[0] + es[1] + es[2])
    ws = [e * inv for e in es]
    heads = []
    for c in range(DIL_SLOTS):
        cs = slice(c * DIL_HEAD_DIM, (c + 1) * DIL_HEAD_DIM)
        mix = (ws[0][:, c:c + 1] * o1_ref[:, cs].astype(F32)
               + ws[1][:, c:c + 1] * s4_ref[c]
               + ws[2][:, c:c + 1] * s16_ref[c])
        heads.append(mix.astype(BF16))
    bm = _dot(jnp.concatenate(heads, axis=1), wb_ref[...])
    gate_a = jax.nn.sigmoid(ga_ref[...].astype(F32) + bg_ref[:, :D_MODEL])
    gate_b = jax.nn.sigmoid(gb_ref[...].astype(F32) + bg_ref[:, D_MODEL:])
    merged = (gate_a * a + gate_b * bm).astype(BF16)
    o_ref[...] = x_ref[...] + _dot(merged, wo_ref[...])


def _merge(o_gla, groups, z, gate_col_block, x, wa, wb, wo, b_gate, bm):
    B, S, D = x.shape
    nt = S // bm
    (o1, l1), (o4, l4), (o16, l16) = groups
    const = lambda i: (0, 0)
    resident = dict(pipeline_mode=pl.Buffered(1))

    def natural(width):
        return pl.BlockSpec((None, None, bm, width), lambda i: (i // nt, 0, i % nt, 0))

    def class_major(d, width):
        return pl.BlockSpec((None, d, bm // d, width), lambda i: (i // nt, 0, i % nt, 0))

    out = pl.pallas_call(
        _merge_kernel,
        grid=(B * nt,),
        in_specs=[
            pl.BlockSpec((bm, GLA_V_W), lambda i: (i, 0)),
            natural(DIL_W), class_major(4, DIL_W), class_major(16, DIL_W),
            natural(LANES), class_major(4, LANES), class_major(16, LANES),
            pl.BlockSpec((bm, D_MODEL), lambda i: (i, gate_col_block)),
            pl.BlockSpec((bm, D_MODEL), lambda i: (i, gate_col_block + 1)),
            pl.BlockSpec((bm, D_MODEL), lambda i: (i, 0)),
            pl.BlockSpec((GLA_V_W, D_MODEL), const, **resident),
            pl.BlockSpec((DIL_W, D_MODEL), const, **resident),
            pl.BlockSpec((D_MODEL, D_MODEL), const, **resident),
            pl.BlockSpec((1, 2 * D_MODEL), const),
        ],
        out_specs=pl.BlockSpec((bm, D_MODEL), lambda i: (i, 0)),
        out_shape=jax.ShapeDtypeStruct((B * S, D_MODEL), F32),
        scratch_shapes=[
            pltpu.VMEM((DIL_SLOTS, bm, LANES), F32), pltpu.VMEM((DIL_SLOTS, bm, LANES), F32),
            pltpu.VMEM((bm, LANES), F32), pltpu.VMEM((bm, LANES), F32),
        ],
        compiler_params=_params(1),
        name="merge_out_proj",
    )(o_gla, o1, o4, o16, l1, l4, l16, z, z, x.reshape(B * S, D), wa, wb, wo, b_gate)
    return out


def _ffn_kernel(x_ref, nw_ref, wgu_ref, wd_ref, fw_ref, o_ref, h_ref):
    f = pl.program_id(1)

    @pl.when(f == 0)
    def _():
        x = x_ref[...]
        ms = jnp.mean(x * x, axis=-1, keepdims=True)
        h_ref[...] = (x * lax.rsqrt(ms + RMS_EPS) * nw_ref[...]).astype(BF16)
        o_ref[...] = x

    bf = wd_ref.shape[0]
    gate_up = _dot(h_ref[...], wgu_ref[...])
    gate, up = gate_up[:, :bf], gate_up[:, bf:]
    act = (gate * jax.nn.sigmoid(gate) * up).astype(BF16)
    for c in range(0, D_MODEL, FFN_OUT_CHUNK):
        o_ref[:, c:c + FFN_OUT_CHUNK] += _dot(act, wd_ref[:, c:c + FFN_OUT_CHUNK])

    @pl.when(f == pl.num_programs(1) - 1)
    def _():
        y = o_ref[...]
        ms = jnp.mean(y * y, axis=-1, keepdims=True)
        o_ref[...] = y * lax.rsqrt(ms + RMS_EPS) * fw_ref[...]


def _ffn_weight_moves(bf):
    moves = []
    for f in range(D_FF // bf):
        moves.append((2 * f * bf, f * bf, bf))
        moves.append(((2 * f + 1) * bf, D_FF + f * bf, bf))
    return moves


def _ffn(x1, norm_w, w_in, w_out, final_w, bm, bf):
    M = x1.shape[0]
    nf = D_FF // bf
    return pl.pallas_call(
        _ffn_kernel,
        grid=(M // bm, nf),
        in_specs=[
            pl.BlockSpec((bm, D_MODEL), lambda i, f: (i, 0)),
            pl.BlockSpec((1, D_MODEL), lambda i, f: (0, 0)),
            pl.BlockSpec((D_MODEL, 2 * bf), lambda i, f: (0, f)),
            pl.BlockSpec((bf, D_MODEL), lambda i, f: (f, 0)),
            pl.BlockSpec((1, D_MODEL), lambda i, f: (0, 0)),
        ],
        out_specs=pl.BlockSpec((bm, D_MODEL), lambda i, f: (i, 0)),
        out_shape=jax.ShapeDtypeStruct((M, D_MODEL), F32),
        scratch_shapes=[pltpu.VMEM((bm, D_MODEL), BF16)],
        compiler_params=_params(2),
        name="swiglu_final_norm",
    )(x1, norm_w, w_in, w_out, final_w)


def kernel(x, attn_norm, w_in, w_gk_up, b_gk, gla_norm, b_gate, w_branch_gla, w_branch_dil,
           w_out, ffn_norm, w_ffn_in, w_ffn_out, rel_bias, final_norm):
    B, S, D = x.shape
    assert D == D_MODEL and S % (16 * DIL_BLOCK) == 0 and attn_norm.shape[0] == 1

    o_lr = 2 * GLA_QK_W + 2 * GLA_V_W
    o_dil = o_lr + GLA_LOWRANK
    o_gate = o_dil + 9 * DIL_W
    n_gla = o_lr
    n_rest = 2 * D_MODEL + 3 * DIL_W
    n_dil = 3 * DIL_W
    lr_col = n_gla
    col_rest = -(-(n_gla + LANES) // n_rest) * n_rest
    col_d4 = -(-(col_rest + n_rest) // n_dil) * n_dil
    col_d16 = col_d4 + n_dil
    w_all = _pack_w_in(
        jnp.swapaxes(w_in, 1, 2)[0],
        segments=((0, 0, n_gla), (lr_col, o_lr, GLA_LOWRANK), (col_rest, o_gate, 2 * D_MODEL),
                  (col_rest + 2 * D_MODEL, o_dil, 3 * DIL_W), (col_d4, o_dil + 3 * DIL_W, 6 * DIL_W)),
        n_out=col_d16 + n_dil)

    z_gla, h, lr = _in_proj_norm(x, attn_norm[0][None, :], w_all, n_gla, lr_col // LANES)
    z_rest = _in_proj(h, w_all, col_rest, n_rest)
    z_d4, w_ffn_in_bf = _in_proj_dil(h, w_all, col_d4, n_dil, 4, w_ffn_in[0], _ffn_weight_moves(FFN_HIDDEN_BLOCK))
    z_d16, w_ffn_out_bf = _in_proj_dil(h, w_all, col_d16, n_dil, 16, w_ffn_out[0])

    wup = jnp.concatenate([w_gk_up[0], jnp.zeros((LANES - GLA_LOWRANK, GLA_QK_W), F32)], axis=0).astype(BF16)
    o_gla = _gla(z_gla.reshape(B * S, n_gla), lr.reshape(B * S, LANES), wup, b_gk[0][None, :],
                 gla_norm[0][None, :], B, S)

    groups, merge_w = [], []
    z_d1 = z_rest.reshape(B, 1, S, n_rest)
    for gi, (zg, col_block, w) in enumerate(((z_d1, 2 * D_MODEL // DIL_W, w_branch_gla[0]),
                                             (z_d4, 0, w_branch_dil[0]), (z_d16, 0, w_out[0]))):
        group, w_bf = _dil_attn(zg, col_block, rel_bias, gi, w)
        groups.append(group)
        merge_w.append(w_bf)
    x1 = _merge(o_gla, groups, z_rest.reshape(B * S, n_rest), 0, x, merge_w[0], merge_w[1], merge_w[2],
                b_gate[0][None, :], MERGE_ROWS)
    out = _ffn(x1, ffn_norm[0][None, :], w_ffn_in_bf, w_ffn_out_bf, final_norm[None, :],
               FFN_ROWS, FFN_HIDDEN_BLOCK)
    return out.reshape(B, S, D)
```

```python
import functools

import numpy as np
import jax
import jax.numpy as jnp
from jax import lax
from jax.experimental import pallas as pl
from jax.experimental.pallas import tpu as pltpu

F32 = jnp.float32
BF16 = jnp.bfloat16

D_MODEL = 2048
GLA_HEADS = 4
GLA_DK = 256
GLA_DV = 512
GLA_QK_W = GLA_HEADS * GLA_DK
GLA_V_W = GLA_HEADS * GLA_DV
GLA_LOWRANK = 16
GLA_GATE_NORM = 16.0
DIL_DILATIONS = (1, 4, 16)
DIL_SLOTS = 8
DIL_HEAD_DIM = 128
DIL_W = DIL_SLOTS * DIL_HEAD_DIM
DIL_STEPS = 128
DIL_BLOCK = 128
REL_BUCKETS = 32
REL_MAX_DIST = 2048
D_FF = 5632
RMS_EPS = 1e-6
NEG_INF = -1e30
LOG2E = 1.4426950408889634
LN2 = 0.6931471805599453

LANES = 128
SUBLANES = 8
BF16_ROWS = 16
MXU_WIDTH = 256
VMEM_LIMIT = 56 * 1024 * 1024

PACK_COLS = 512
PROJ_NAT_ROWS = 512
PROJ_ROWS = 1024
PROJ_CHUNK = MXU_WIDTH
GLA_BLOCK = 128
GLA_STEP_ROWS = 512
MERGE_ROWS = 256
FFN_PART_ROWS = 512
FFN_HIDDEN_BLOCK = 512
FFN_PARTS = ((0, 4), (4, 8), (8, 11))
FFN_OUT_CHUNK = 512


def _params(n_axes):
    return pltpu.CompilerParams(dimension_semantics=("arbitrary",) * n_axes, vmem_limit_bytes=VMEM_LIMIT)


def _dot(a, b):
    return jnp.dot(a, b, preferred_element_type=F32)


def _dot_t(a, b, dims):
    return lax.dot_general(a, b, (dims, ((), ())), preferred_element_type=F32)


def _split_bf16(x):
    hi = x.astype(BF16)
    lo = (x - hi.astype(F32)).astype(BF16)
    return hi, lo


def _in_proj_norm_kernel(x_ref, g_ref, w_ref, w_lr_ref, o_ref, h_ref, lr_ref):
    x = x_ref[...]
    ms = jnp.mean(x * x, axis=-1, keepdims=True)
    h_ref[...] = (x * lax.rsqrt(ms + RMS_EPS) * g_ref[...]).astype(h_ref.dtype)
    lr_ref[...] = _dot(h_ref[...], w_lr_ref[...]).astype(lr_ref.dtype)
    for c in range(0, w_ref.shape[1], PROJ_CHUNK):
        o_ref[:, c:c + PROJ_CHUNK] = _dot(h_ref[...], w_ref[:, c:c + PROJ_CHUNK]).astype(o_ref.dtype)


def _in_proj_norm(x, norm_w, w, n_cols, lr_col_block):
    B, S, D = x.shape
    rows = PROJ_NAT_ROWS
    resident = dict(pipeline_mode=pl.Buffered(1))
    return pl.pallas_call(
        _in_proj_norm_kernel,
        grid=(B, S // rows),
        in_specs=[
            pl.BlockSpec((None, rows, D), lambda b, t: (b, t, 0)),
            pl.BlockSpec((1, D), lambda b, t: (0, 0)),
            pl.BlockSpec((D, n_cols), lambda b, t: (0, 0), **resident),
            pl.BlockSpec((D, LANES), lambda b, t: (0, lr_col_block), **resident),
        ],
        out_specs=[
            pl.BlockSpec((None, rows, n_cols), lambda b, t: (b, t, 0)),
            pl.BlockSpec((None, rows, D), lambda b, t: (b, t, 0)),
            pl.BlockSpec((None, rows, LANES), lambda b, t: (b, t, 0)),
        ],
        out_shape=[jax.ShapeDtypeStruct((B, S, n_cols), BF16), jax.ShapeDtypeStruct((B, S, D), BF16),
                   jax.ShapeDtypeStruct((B, S, LANES), BF16)],
        compiler_params=_params(2),
        name="in_proj_norm",
    )(x, norm_w, w, w)


def _in_proj_kernel(h_ref, w_ref, o_ref):
    for c in range(0, w_ref.shape[1], PROJ_CHUNK):
        o_ref[:, c:c + PROJ_CHUNK] = _dot(h_ref[...], w_ref[:, c:c + PROJ_CHUNK]).astype(o_ref.dtype)


def _in_proj(h, w, col0, n_cols):
    B, S, D = h.shape
    rows = PROJ_NAT_ROWS
    assert col0 % n_cols == 0
    return pl.pallas_call(
        _in_proj_kernel,
        grid=(B, S // rows),
        in_specs=[
            pl.BlockSpec((None, rows, D), lambda b, t: (b, t, 0)),
            pl.BlockSpec((D, n_cols), lambda b, t: (0, col0 // n_cols), pipeline_mode=pl.Buffered(1)),
        ],
        out_specs=pl.BlockSpec((None, rows, n_cols), lambda b, t: (b, t, 0)),
        out_shape=jax.ShapeDtypeStruct((B, S, n_cols), BF16),
        compiler_params=_params(2),
        name="in_proj_rest",
    )(h, w)


def _in_proj_dil_kernel(h_ref, w_ref, cast_ref, o_ref, cast_out_ref, *scratch, dilation, cast_moves):
    rows = h_ref.shape[0]
    chunk_slabs = PROJ_CHUNK // LANES

    for dst, src, width in cast_moves:
        cast_out_ref[:, dst:dst + width] = cast_ref[:, src:src + width].astype(cast_out_ref.dtype)

    passes, rest = [], dilation
    while rest > 1:
        passes.append(min(rest, 4))
        rest //= passes[-1]

    def deinterleave(res, chunk):
        base = (chunk % 2) * chunk_slabs
        for i in range(chunk_slabs):
            scratch[0][base + i] = res[:, i * LANES:(i + 1) * LANES]
        src_ref, n_groups = scratch[0], 1
        for p, stride in enumerate(passes):
            group_rows = rows // n_groups
            last = p == len(passes) - 1
            for g in range(n_groups):
                for r in range(stride):
                    for i in range(chunk_slabs):
                        part = src_ref[base + i, pl.ds(g * group_rows + r, group_rows // stride, stride=stride), :]
                        if last:
                            col = (chunk * chunk_slabs + i) * LANES
                            o_ref[g + n_groups * r, :, col:col + LANES] = part.astype(o_ref.dtype)
                        else:
                            dst = (g + n_groups * r) * (group_rows // stride)
                            scratch[1][base + i, dst:dst + group_rows // stride, :] = part
            if not last:
                src_ref, n_groups = scratch[1], n_groups * stride

    n_chunks = w_ref.shape[1] // PROJ_CHUNK
    res = _dot(h_ref[...], w_ref[:, :PROJ_CHUNK])
    for i in range(n_chunks):
        nxt = _dot(h_ref[...], w_ref[:, (i + 1) * PROJ_CHUNK:(i + 2) * PROJ_CHUNK]) if i + 1 < n_chunks else None
        deinterleave(res, i)
        res = nxt


def _in_proj_dil(h, w, col0, n_cols, dilation, to_cast, cast_moves=None):
    B, S, D = h.shape
    rows = PROJ_ROWS
    nt = S // rows
    assert col0 % n_cols == 0 and n_cols % PROJ_CHUNK == 0
    cast_rows = to_cast.shape[0] // (B * nt)
    assert cast_rows * B * nt == to_cast.shape[0] and cast_rows % BF16_ROWS == 0
    if cast_moves is None:
        cast_moves = ((0, 0, to_cast.shape[1]),)
    assert sum(m[2] for m in cast_moves) == to_cast.shape[1]
    cast_spec = pl.BlockSpec((cast_rows, to_cast.shape[1]), lambda b, t: (b * nt + t, 0))
    n_slabs = 2 * PROJ_CHUNK // LANES
    scratch = [pltpu.VMEM((n_slabs, rows, LANES), F32)]
    if dilation > 4:
        scratch.append(pltpu.VMEM((n_slabs, rows, LANES), F32))
    return pl.pallas_call(
        functools.partial(_in_proj_dil_kernel, dilation=dilation, cast_moves=tuple(cast_moves)),
        grid=(B, nt),
        in_specs=[
            pl.BlockSpec((None, rows, D), lambda b, t: (b, t, 0)),
            pl.BlockSpec((D, n_cols), lambda b, t: (0, col0 // n_cols), pipeline_mode=pl.Buffered(1)),
            cast_spec,
        ],
        out_specs=[
            pl.BlockSpec((None, dilation, rows // dilation, n_cols), lambda b, t: (b, 0, t, 0)),
            cast_spec,
        ],
        out_shape=[jax.ShapeDtypeStruct((B, dilation, S // dilation, n_cols), BF16),
                   jax.ShapeDtypeStruct(to_cast.shape, BF16)],
        scratch_shapes=scratch,
        compiler_params=_params(2),
        name=f"in_proj_d{dilation}",
    )(h, w, to_cast)


def _pack_w_in_kernel(src_ref, valid_ref, wt_ref, o_ref):
    x = wt_ref[...]
    row = lax.broadcasted_iota(jnp.int32, x.shape, 0)
    x = jnp.where(row < valid_ref[pl.program_id(0)], x, 0.0)
    o_ref[...] = x.T.astype(o_ref.dtype)


def _pack_w_in(wt, segments, n_out):
    N, D = wt.shape
    n_blocks = n_out // PACK_COLS
    src = np.zeros((n_blocks,), np.int32)
    valid = np.zeros((n_blocks,), np.int32)
    for dst, s, width in segments:
        assert dst % PACK_COLS == 0
        for i in range(dst // PACK_COLS, -(-(dst + width) // PACK_COLS)):
            start = s + i * PACK_COLS - dst
            assert start % SUBLANES == 0 and start + PACK_COLS <= N
            src[i] = start // SUBLANES
            valid[i] = min(PACK_COLS, dst + width - i * PACK_COLS)
    return pl.pallas_call(
        _pack_w_in_kernel,
        grid_spec=pltpu.PrefetchScalarGridSpec(
            num_scalar_prefetch=2,
            grid=(n_blocks,),
            in_specs=[pl.BlockSpec((pl.Element(PACK_COLS), pl.Element(D)),
                                   lambda i, src, valid: (src[i] * SUBLANES, 0))],
            out_specs=pl.BlockSpec((D, PACK_COLS), lambda i, src, valid: (0, i)),
        ),
        out_shape=jax.ShapeDtypeStruct((D, n_out), BF16),
        compiler_params=_params(1),
        name="pack_w_in",
    )(jnp.asarray(src), jnp.asarray(valid), wt)


def _gla_kernel(q_ref, k_ref, v_ref, g_ref, lr_ref, wup_ref, bgk_ref, nw_ref, o_ref, s_ref):
    C = GLA_BLOCK

    @pl.when(pl.program_id(1) == 0)
    def _():
        s_ref[...] = jnp.zeros_like(s_ref)

    row = lax.broadcasted_iota(jnp.int32, (C, C), 0)
    col = lax.broadcasted_iota(jnp.int32, (C, C), 1)
    causal = row >= col
    tri = jnp.where(causal, 1.0, 0.0).astype(BF16)
    ones = jnp.ones((C, LANES), BF16)
    scale = GLA_DK ** -0.5
    n_blk = q_ref.shape[0] // C
    heads = range(GLA_HEADS)
    ks = [slice(h * GLA_DK, (h + 1) * GLA_DK) for h in heads]
    vs = [slice(h * GLA_DV, (h + 1) * GLA_DV) for h in heads]
    units = [(blk, h) for blk in range(n_blk) for h in heads]
    rows = {u: slice(u[0] * C, (u[0] + 1) * C) for u in units}
    gks, b_cum, tot, att, q_dec, k_dec = {}, {}, {}, {}, {}, {}
    o_intra, q_in, k_end, decay = {}, {}, {}, {}
    for u in units:
        pre = _dot(lr_ref[rows[u], :], wup_ref[:, ks[u[1]]]) + bgk_ref[:, ks[u[1]]]
        log_sig = jnp.minimum(pre, 0.0) - jnp.log(1.0 + jnp.exp(-jnp.abs(pre)))
        gks[u] = _split_bf16(log_sig * (1.0 / GLA_GATE_NORM))
    for u in units:
        gk_hi, gk_lo = gks[u]
        b_cum[u] = _dot(tri, gk_hi) + _dot(tri, gk_lo)
        tot[u] = _dot_t(gk_hi, ones, ((0,), (0,))) + _dot_t(gk_lo, ones, ((0,), (0,)))
    for u in units:
        b = b_cum[u]
        b_mid = b[C // 2 - 1:C // 2, :]
        q_dec[u] = q_ref[rows[u], ks[u[1]]].astype(F32) * jnp.exp(b - b_mid)
        k_dec[u] = k_ref[rows[u], ks[u[1]]].astype(F32) * jnp.exp(b_mid - b)
        att[u] = _dot_t((q_dec[u] * scale).astype(BF16), k_dec[u].astype(BF16), ((1,), (1,)))
    for u in units:
        b = b_cum[u]
        b_last = b[C - 1:C, :]
        b_mid = b[C // 2 - 1:C // 2, :]
        o_intra[u] = _dot(jnp.where(causal, att[u], 0.0).astype(BF16), v_ref[rows[u], vs[u[1]]])
        q_in[u] = (q_dec[u] * (jnp.exp(b_mid) * scale)).astype(BF16)
        k_end[u] = (k_dec[u] * jnp.exp(b_last - b_mid)).astype(BF16)
        decay[u] = jnp.concatenate([jnp.exp(tot[u])] * (GLA_DV // LANES), axis=1)
    for blk in range(n_blk):
        rs = slice(blk * C, (blk + 1) * C)
        os = []
        for h in heads:
            s_old = s_ref[h]
            os.append(o_intra[blk, h] + _dot(q_in[blk, h], s_old.astype(BF16)))
            s_ref[h] = decay[blk, h] * s_old + _dot_t(k_end[blk, h], v_ref[rs, vs[h]], ((0,), (0,)))
        ys = []
        for h, o in zip(heads, os):
            ms = jnp.mean(o * o, axis=-1, keepdims=True)
            g = g_ref[rs, vs[h]].astype(F32)
            y = (o * lax.rsqrt(ms + RMS_EPS) * nw_ref[...]) * (g * jax.nn.sigmoid(g))
            ys.append(y.astype(o_ref.dtype))
        o_ref[rs, :] = jnp.concatenate(ys, axis=1)


def _gla(z, lr, wup, bgk, norm_w, B, S):
    C = GLA_STEP_ROWS
    nc = S // C
    return pl.pallas_call(
        _gla_kernel,
        grid=(B, nc),
        in_specs=[
            pl.BlockSpec((C, GLA_QK_W), lambda b, c: (b * nc + c, 0)),
            pl.BlockSpec((C, GLA_QK_W), lambda b, c: (b * nc + c, 1)),
            pl.BlockSpec((C, GLA_V_W), lambda b, c: (b * nc + c, 1)),
            pl.BlockSpec((C, GLA_V_W), lambda b, c: (b * nc + c, 2)),
            pl.BlockSpec((C, LANES), lambda b, c: (b * nc + c, 0)),
            pl.BlockSpec((LANES, GLA_QK_W), lambda b, c: (0, 0)),
            pl.BlockSpec((1, GLA_QK_W), lambda b, c: (0, 0)),
            pl.BlockSpec((1, GLA_DV), lambda b, c: (0, 0)),
        ],
        out_specs=pl.BlockSpec((C, GLA_V_W), lambda b, c: (b * nc + c, 0)),
        out_shape=jax.ShapeDtypeStruct((B * S, GLA_V_W), BF16),
        scratch_shapes=[pltpu.VMEM((GLA_HEADS, GLA_DK, GLA_DV), F32)],
        compiler_params=_params(2),
        name="gla",
    )(z, z, z, z, lr, wup, bgk, norm_w)


def _dil_attn_kernel(q_ref, k_ref, v_ref, bucket_ref, table_ref, cast_ref, o_ref, lse_ref, cast_out_ref,
                     bias_ref, *, n_cls, n_blk, buckets, head0):
    scale = DIL_HEAD_DIM ** -0.5
    lane = lax.broadcasted_iota(jnp.int32, (DIL_BLOCK, LANES), 1)
    cast_out_ref[...] = cast_ref[...].astype(cast_out_ref.dtype)

    @pl.when((pl.program_id(0) == 0) & (pl.program_id(1) == 0))
    def _():
        bucket = bucket_ref[...]
        for h in range(DIL_SLOTS):
            bias = jnp.full(bucket.shape, NEG_INF, F32)
            for k in buckets:
                bias = jnp.where(bucket == k, table_ref[k, head0 + h] * LOG2E, bias)
            bias_ref[h] = bias

    def blocks(where, first):
        n_keys = DIL_BLOCK if first else 2 * DIL_BLOCK
        ones = jnp.ones((n_keys, DIL_HEAD_DIM), BF16)
        cols = [slice(h * DIL_HEAD_DIM, (h + 1) * DIL_HEAD_DIM) for h in range(DIL_SLOTS)]
        rows_q = [pl.ds(start, DIL_BLOCK) for _, start in where]
        rows_k = [rq if first else pl.ds(start - DIL_BLOCK, 2 * DIL_BLOCK) for rq, (_, start) in zip(rows_q, where)]
        units = [(u, h) for u in range(len(where)) for h in range(DIL_SLOTS)]
        scores = [
            _dot_t(q_ref[where[u][0], rows_q[u], cols[h]], k_ref[where[u][0], rows_k[u], cols[h]], ((1,), (1,)))
            * (scale * LOG2E) + (bias_ref[h, :, DIL_BLOCK:] if first else bias_ref[h])
            for u, h in units]
        maxes = [jnp.max(s, axis=-1, keepdims=True) for s in scores]
        probs = [jnp.exp2(s - m).astype(BF16) for s, m in zip(scores, maxes)]
        num_ls = [_dot(p, jnp.concatenate([v_ref[where[u][0], rows_k[u], cols[h]], ones], axis=1))
                  for p, (u, h) in zip(probs, units)]
        for u, (c, _) in enumerate(where):
            outs = []
            lse_tile = jnp.zeros((DIL_BLOCK, LANES), F32)
            for h in range(DIL_SLOTS):
                num_l, m = num_ls[u * DIL_SLOTS + h], maxes[u * DIL_SLOTS + h]
                l = num_l[:, DIL_HEAD_DIM:]
                outs.append((num_l[:, :DIL_HEAD_DIM] / l).astype(o_ref.dtype))
                lse_tile = jnp.where(lane == h, m * LN2 + jnp.log(l), lse_tile)
            o_ref[c, rows_q[u], :] = jnp.concatenate(outs, axis=1)
            lse_ref[c, rows_q[u], :] = lse_tile

    def row(n):
        return pl.multiple_of(n * DIL_BLOCK, DIL_BLOCK)

    if n_blk == 1:
        def pair_of_classes(i, carry):
            blocks([(2 * i, 0), (2 * i + 1, 0)], True)
            return carry
        lax.fori_loop(0, n_cls // 2, pair_of_classes, 0)
    else:
        def one_class(c, carry):
            blocks([(c, 0)], True)

            def pair_of_blocks(i, inner):
                blocks([(c, row(2 * i + 1)), (c, row(2 * i + 2))], False)
                return inner
            lax.fori_loop(0, (n_blk - 1) // 2, pair_of_blocks, 0)
            if (n_blk - 1) % 2:
                blocks([(c, (n_blk - 1) * DIL_BLOCK)], False)
            return carry

        if n_cls == 1:
            one_class(0, 0)
        else:
            lax.fori_loop(0, n_cls, one_class, 0)


def _band_buckets(dilation):
    a_idx = np.arange(DIL_BLOCK)[:, None]
    c_idx = np.arange(2 * DIL_BLOCK)[None, :]
    steps = DIL_BLOCK + a_idx - c_idx
    in_band = (steps >= 0) & (steps <= DIL_STEPS)
    bucket = _t5_bucket(np.clip(steps, 0, None) * dilation)
    return np.where(in_band, bucket, -1).astype(np.int32)


def _dil_attn(qkv, col_block, rel_bias, group, to_cast):
    B, d, L, _ = qkv.shape
    n_cls = d
    bucket = _band_buckets(d)
    buckets = tuple(int(k) for k in np.unique(bucket) if k >= 0)
    cast_rows = to_cast.shape[0] // B
    assert cast_rows * B == to_cast.shape[0] and cast_rows % BF16_ROWS == 0
    cast_spec = pl.BlockSpec((cast_rows, to_cast.shape[1]), lambda b, r: (b, 0))

    def part(p):
        return pl.BlockSpec((None, n_cls, L, DIL_W), lambda b, r: (b, r, 0, col_block + p))

    o, lse, cast = pl.pallas_call(
        functools.partial(_dil_attn_kernel, n_cls=n_cls, n_blk=L // DIL_BLOCK, buckets=buckets,
                          head0=group * DIL_SLOTS),
        grid=(B, d // n_cls),
        in_specs=[
            part(0), part(1), part(2),
            pl.BlockSpec((DIL_BLOCK, 2 * DIL_BLOCK), lambda b, r: (0, 0)),
            pl.BlockSpec(memory_space=pltpu.SMEM),
            cast_spec,
        ],
        scratch_shapes=[pltpu.VMEM((DIL_SLOTS, DIL_BLOCK, 2 * DIL_BLOCK), F32)],
        out_specs=[
            pl.BlockSpec((None, n_cls, L, DIL_W), lambda b, r: (b, r, 0, 0)),
            pl.BlockSpec((None, n_cls, L, LANES), lambda b, r: (b, r, 0, 0)),
            cast_spec,
        ],
        out_shape=[
            jax.ShapeDtypeStruct((B, d, L, DIL_W), BF16),
            jax.ShapeDtypeStruct((B, d, L, LANES), F32),
            jax.ShapeDtypeStruct(to_cast.shape, BF16),
        ],
        compiler_params=_params(2),
        name=f"dil_attn_d{d}",
    )(qkv, qkv, qkv, jnp.asarray(bucket), rel_bias, to_cast)
    return (o, lse), cast


def _t5_bucket(dist):
    max_exact = REL_BUCKETS // 2
    d = np.maximum(dist, 1).astype(np.float64)
    large = max_exact + (np.log(d / max_exact) / np.log(REL_MAX_DIST / max_exact)
                         * (REL_BUCKETS - max_exact)).astype(np.int64)
    large = np.minimum(large, REL_BUCKETS - 1)
    return np.where(dist < max_exact, dist, large).astype(np.int32)


def _merge_kernel(og_ref, o1_ref, o4_ref, o16_ref, l1_ref, l4_ref, l16_ref, ga_ref, gb_ref, x_ref,
                  wa_ref, wb_ref, wo_ref, bg_ref, o_ref, s4_ref, s16_ref, sl4_ref, sl16_ref):
    T = og_ref.shape[0]
    a = _dot(og_ref[...], wa_ref[...])
    for d, grp_ref, lse_ref, s_ref, sl_ref in ((4, o4_ref, l4_ref, s4_ref, sl4_ref),
                                                (16, o16_ref, l16_ref, s16_ref, sl16_ref)):
        for r in range(d):
            rows = pl.ds(r, T // d, stride=d)
            sl_ref[rows, :] = lse_ref[r]
            for c in range(DIL_SLOTS):
                s_ref[c, rows, :] = grp_ref[r, :, c * LANES:(c + 1) * LANES].astype(F32)
    lses = (l1_ref[...], sl4_ref[...], sl16_ref[...])
    m = jnp.maximum(jnp.maximum(lses[0], lses[1]), lses[2])
    es = [jnp.exp(l - m) for l in lses]
    inv = 1.0 / (es[0] + es[1] + es[2])
    ws = [e * inv for e in es]
    heads = []
    for c in range(DIL_SLOTS):
        cs = slice(c * DIL_HEAD_DIM, (c + 1) * DIL_HEAD_DIM)
        mix = (ws[0][:, c:c + 1] * o1_ref[:, cs].astype(F32)
               + ws[1][:, c:c + 1] * s4_ref[c]
               + ws[2][:, c:c + 1] * s16_ref[c])
        heads.append(mix.astype(BF16))
    bm = _dot(jnp.concatenate(heads, axis=1), wb_ref[...])
    gate_a = jax.nn.sigmoid(ga_ref[...].astype(F32) + bg_ref[:, :D_MODEL])
    gate_b = jax.nn.sigmoid(gb_ref[...].astype(F32) + bg_ref[:, D_MODEL:])
    merged = (gate_a * a + gate_b * bm).astype(BF16)
    o_ref[...] = x_ref[...] + _dot(merged, wo_ref[...])


def _merge(o_gla, groups, z, gate_col_block, x, wa, wb, wo, b_gate, bm):
    B, S, D = x.shape
    nt = S // bm
    (o1, l1), (o4, l4), (o16, l16) = groups
    const = lambda i: (0, 0)
    resident = dict(pipeline_mode=pl.Buffered(1))

    def natural(width):
        return pl.BlockSpec((None, None, bm, width), lambda i: (i // nt, 0, i % nt, 0))

    def class_major(d, width):
        return pl.BlockSpec((None, d, bm // d, width), lambda i: (i // nt, 0, i % nt, 0))

    out = pl.pallas_call(
        _merge_kernel,
        grid=(B * nt,),
        in_specs=[
            pl.BlockSpec((bm, GLA_V_W), lambda i: (i, 0)),
            natural(DIL_W), class_major(4, DIL_W), class_major(16, DIL_W),
            natural(LANES), class_major(4, LANES), class_major(16, LANES),
            pl.BlockSpec((bm, D_MODEL), lambda i: (i, gate_col_block)),
            pl.BlockSpec((bm, D_MODEL), lambda i: (i, gate_col_block + 1)),
            pl.BlockSpec((bm, D_MODEL), lambda i: (i, 0)),
            pl.BlockSpec((GLA_V_W, D_MODEL), const, **resident),
            pl.BlockSpec((DIL_W, D_MODEL), const, **resident),
            pl.BlockSpec((D_MODEL, D_MODEL), const, **resident),
            pl.BlockSpec((1, 2 * D_MODEL), const),
        ],
        out_specs=pl.BlockSpec((bm, D_MODEL), lambda i: (i, 0)),
        out_shape=jax.ShapeDtypeStruct((B * S, D_MODEL), F32),
        scratch_shapes=[
            pltpu.VMEM((DIL_SLOTS, bm, LANES), F32), pltpu.VMEM((DIL_SLOTS, bm, LANES), F32),
            pltpu.VMEM((bm, LANES), F32), pltpu.VMEM((bm, LANES), F32),
        ],
        compiler_params=_params(1),
        name="merge_out_proj",
    )(o_gla, o1, o4, o16, l1, l4, l16, z, z, x.reshape(B * S, D), wa, wb, wo, b_gate)
    return out


def _ffn_kernel(x_ref, nw_ref, wgu_ref, wd_ref, fw_ref, o_ref, h_ref):
    f = pl.program_id(1)

    @pl.when(f == 0)
    def _():
        x = x_ref[...]
        ms = jnp.mean(x * x, axis=-1, keepdims=True)
        h_ref[...] = (x * lax.rsqrt(ms + RMS_EPS) * nw_ref[...]).astype(BF16)
        o_ref[...] = x

    bf = wd_ref.shape[0]
    gate_up = _dot(h_ref[...], wgu_ref[...])
    gate, up = gate_up[:, :bf], gate_up[:, bf:]
    act = (gate * jax.nn.sigmoid(gate) * up).astype(BF16)
    for c in range(0, D_MODEL, FFN_OUT_CHUNK):
        o_ref[:, c:c + FFN_OUT_CHUNK] += _dot(act, wd_ref[:, c:c + FFN_OUT_CHUNK])

    @pl.when(f == pl.num_programs(1) - 1)
    def _():
        y = o_ref[...]
        ms = jnp.mean(y * y, axis=-1, keepdims=True)
        o_ref[...] = y * lax.rsqrt(ms + RMS_EPS) * fw_ref[...]


def _ffn_part_kernel(*refs, first, last):
    refs = list(refs)
    x_ref = refs.pop(0)
    h_in_ref = None if first else refs.pop(0)
    nw_ref = refs.pop(0) if first else None
    wgu_ref, wd_ref = refs.pop(0), refs.pop(0)
    fw_ref = refs.pop(0) if last else None
    o_ref = refs.pop(0)
    h_ref = refs.pop(0) if first else h_in_ref
    bf = FFN_HIDDEN_BLOCK

    x = x_ref[...]
    if first:
        ms = jnp.mean(x * x, axis=-1, keepdims=True)
        h_ref[...] = (x * lax.rsqrt(ms + RMS_EPS) * nw_ref[...]).astype(h_ref.dtype)
    o_ref[...] = x
    for f in range(wd_ref.shape[0] // bf):
        gate_up = _dot(h_ref[...], wgu_ref[:, 2 * f * bf:(2 * f + 2) * bf])
        gate, up = gate_up[:, :bf], gate_up[:, bf:]
        act = (gate * jax.nn.sigmoid(gate) * up).astype(BF16)
        for c in range(0, D_MODEL, FFN_OUT_CHUNK):
            o_ref[:, c:c + FFN_OUT_CHUNK] += _dot(act, wd_ref[f * bf:(f + 1) * bf, c:c + FFN_OUT_CHUNK])
    if last:
        y = o_ref[...]
        ms = jnp.mean(y * y, axis=-1, keepdims=True)
        o_ref[...] = y * lax.rsqrt(ms + RMS_EPS) * fw_ref[...]


def _ffn_part(x, h, norm_w, w_in, w_out, final_w, blocks, first, last):
    M = x.shape[0]
    rows, bf = FFN_PART_ROWS, FFN_HIDDEN_BLOCK
    n_hidden = (blocks[1] - blocks[0]) * bf
    row_spec = pl.BlockSpec((rows, D_MODEL), lambda i: (i, 0))
    vec_spec = pl.BlockSpec((1, D_MODEL), lambda i: (0, 0))
    resident = dict(pipeline_mode=pl.Buffered(1))
    wgu_spec = pl.BlockSpec((pl.Element(D_MODEL), pl.Element(2 * n_hidden)),
                            lambda i: (0, 2 * blocks[0] * bf), **resident)
    wd_spec = pl.BlockSpec((pl.Element(n_hidden), pl.Element(D_MODEL)), lambda i: (blocks[0] * bf, 0), **resident)
    args, in_specs = [x], [row_spec]
    if first:
        args.append(norm_w), in_specs.append(vec_spec)
    else:
        args.append(h), in_specs.append(row_spec)
    args += [w_in, w_out]
    in_specs += [wgu_spec, wd_spec]
    if last:
        args.append(final_w), in_specs.append(vec_spec)
    out_shape = [jax.ShapeDtypeStruct((M, D_MODEL), F32)]
    out_specs = [row_spec]
    if first:
        out_shape.append(jax.ShapeDtypeStruct((M, D_MODEL), BF16)), out_specs.append(row_spec)
    return pl.pallas_call(
        functools.partial(_ffn_part_kernel, first=first, last=last),
        grid=(M // rows,),
        in_specs=in_specs,
        out_specs=out_specs,
        out_shape=out_shape,
        compiler_params=_params(1),
        name=f"swiglu_{blocks[0]}_{blocks[1]}",
    )(*args)


def _ffn_weight_moves(bf):
    moves = []
    for f in range(D_FF // bf):
        moves.append((2 * f * bf, f * bf, bf))
        moves.append(((2 * f + 1) * bf, D_FF + f * bf, bf))
    return moves


def _ffn(x1, norm_w, w_in, w_out, final_w, bm, bf):
    M = x1.shape[0]
    nf = D_FF // bf
    return pl.pallas_call(
        _ffn_kernel,
        grid=(M // bm, nf),
        in_specs=[
            pl.BlockSpec((bm, D_MODEL), lambda i, f: (i, 0)),
            pl.BlockSpec((1, D_MODEL), lambda i, f: (0, 0)),
            pl.BlockSpec((D_MODEL, 2 * bf), lambda i, f: (0, f)),
            pl.BlockSpec((bf, D_MODEL), lambda i, f: (f, 0)),
            pl.BlockSpec((1, D_MODEL), lambda i, f: (0, 0)),
        ],
        out_specs=pl.BlockSpec((bm, D_MODEL), lambda i, f: (i, 0)),
        out_shape=jax.ShapeDtypeStruct((M, D_MODEL), F32),
        scratch_shapes=[pltpu.VMEM((bm, D_MODEL), BF16)],
        compiler_params=_params(2),
        name="swiglu_final_norm",
    )(x1, norm_w, w_in, w_out, final_w)


def kernel(x, attn_norm, w_in, w_gk_up, b_gk, gla_norm, b_gate, w_branch_gla, w_branch_dil,
           w_out, ffn_norm, w_ffn_in, w_ffn_out, rel_bias, final_norm):
    B, S, D = x.shape
    assert D == D_MODEL and S % (16 * DIL_BLOCK) == 0 and attn_norm.shape[0] == 1

    o_lr = 2 * GLA_QK_W + 2 * GLA_V_W
    o_dil = o_lr + GLA_LOWRANK
    o_gate = o_dil + 9 * DIL_W
    n_gla = o_lr
    n_rest = 2 * D_MODEL + 3 * DIL_W
    n_dil = 3 * DIL_W
    lr_col = n_gla
    col_rest = -(-(n_gla + LANES) // n_rest) * n_rest
    col_d4 = -(-(col_rest + n_rest) // n_dil) * n_dil
    col_d16 = col_d4 + n_dil
    w_all = _pack_w_in(
        jnp.swapaxes(w_in, 1, 2)[0],
        segments=((0, 0, n_gla), (lr_col, o_lr, GLA_LOWRANK), (col_rest, o_gate, 2 * D_MODEL),
                  (col_rest + 2 * D_MODEL, o_dil, 3 * DIL_W), (col_d4, o_dil + 3 * DIL_W, 6 * DIL_W)),
        n_out=col_d16 + n_dil)

    z_gla, h, lr = _in_proj_norm(x, attn_norm[0][None, :], w_all, n_gla, lr_col // LANES)
    z_rest = _in_proj(h, w_all, col_rest, n_rest)
    z_d4, w_ffn_in_bf = _in_proj_dil(h, w_all, col_d4, n_dil, 4, w_ffn_in[0], _ffn_weight_moves(FFN_HIDDEN_BLOCK))
    z_d16, w_ffn_out_bf = _in_proj_dil(h, w_all, col_d16, n_dil, 16, w_ffn_out[0])

    wup = jnp.concatenate([w_gk_up[0], jnp.zeros((LANES - GLA_LOWRANK, GLA_QK_W), F32)], axis=0).astype(BF16)
    o_gla = _gla(z_gla.reshape(B * S, n_gla), lr.reshape(B * S, LANES), wup, b_gk[0][None, :],
                 gla_norm[0][None, :], B, S)

    groups, merge_w = [], []
    z_d1 = z_rest.reshape(B, 1, S, n_rest)
    for gi, (zg, col_block, w) in enumerate(((z_d1, 2 * D_MODEL // DIL_W, w_branch_gla[0]),
                                             (z_d4, 0, w_branch_dil[0]), (z_d16, 0, w_out[0]))):
        group, w_bf = _dil_attn(zg, col_block, rel_bias, gi, w)
        groups.append(group)
        merge_w.append(w_bf)
    x1 = _merge(o_gla, groups, z_rest.reshape(B * S, n_rest), 0, x, merge_w[0], merge_w[1], merge_w[2],
                b_gate[0][None, :], MERGE_ROWS)
    acc, h_ffn = x1, None
    for i, blocks in enumerate(FFN_PARTS):
        first, last = i == 0, i == len(FFN_PARTS) - 1
        res = _ffn_part(acc, h_ffn, ffn_norm[0][None, :], w_ffn_in_bf, w_ffn_out_bf, final_norm[None, :],
                        blocks, first, last)
        acc, h_ffn = (res[0], res[1]) if first else (res[0], h_ffn)
    return acc.reshape(B, S, D)
```

```python
import functools

import numpy as np
import jax
import jax.numpy as jnp
from jax import lax
from jax.experimental import pallas as pl
from jax.experimental.pallas import tpu as pltpu

F32 = jnp.float32
BF16 = jnp.bfloat16

D_MODEL = 2048
GLA_HEADS = 4
GLA_DK = 256
GLA_DV = 512
GLA_QK_W = GLA_HEADS * GLA_DK
GLA_V_W = GLA_HEADS * GLA_DV
GLA_LOWRANK = 16
GLA_GATE_NORM = 16.0
DIL_DILATIONS = (1, 4, 16)
DIL_SLOTS = 8
DIL_HEAD_DIM = 128
DIL_W = DIL_SLOTS * DIL_HEAD_DIM
DIL_STEPS = 128
DIL_BLOCK = 128
REL_BUCKETS = 32
REL_MAX_DIST = 2048
D_FF = 5632
RMS_EPS = 1e-6
NEG_INF = -1e30
LOG2E = 1.4426950408889634
LN2 = 0.6931471805599453

LANES = 128
SUBLANES = 8
BF16_ROWS = 16
MXU_WIDTH = 256
VMEM_LIMIT = 56 * 1024 * 1024

PACK_COLS = 512
PROJ_NAT_ROWS = 512
PROJ_ROWS = 1024
PROJ_CHUNK = MXU_WIDTH
GLA_BLOCK = 128
GLA_STEP_ROWS = 512
MERGE_ROWS = 512
OUT_PROJ_ROWS = 1024
FFN_PART_ROWS = 512
FFN_HIDDEN_BLOCK = 512
FFN_PARTS = ((0, 4), (4, 8), (8, 11))
FFN_OUT_CHUNK = 512


def _params(n_axes):
    return pltpu.CompilerParams(dimension_semantics=("arbitrary",) * n_axes, vmem_limit_bytes=VMEM_LIMIT)


def _dot(a, b):
    return jnp.dot(a, b, preferred_element_type=F32)


def _dot_t(a, b, dims):
    return lax.dot_general(a, b, (dims, ((), ())), preferred_element_type=F32)


def _split_bf16(x):
    hi = x.astype(BF16)
    lo = (x - hi.astype(F32)).astype(BF16)
    return hi, lo


def _in_proj_norm_kernel(x_ref, g_ref, w_ref, w_lr_ref, o_ref, h_ref, lr_ref):
    x = x_ref[...]
    ms = jnp.mean(x * x, axis=-1, keepdims=True)
    h_ref[...] = (x * lax.rsqrt(ms + RMS_EPS) * g_ref[...]).astype(h_ref.dtype)
    lr_ref[...] = _dot(h_ref[...], w_lr_ref[...]).astype(lr_ref.dtype)
    for c in range(0, w_ref.shape[1], PROJ_CHUNK):
        o_ref[:, c:c + PROJ_CHUNK] = _dot(h_ref[...], w_ref[:, c:c + PROJ_CHUNK]).astype(o_ref.dtype)


def _in_proj_norm(x, norm_w, w, n_cols, lr_col_block):
    B, S, D = x.shape
    rows = PROJ_NAT_ROWS
    resident = dict(pipeline_mode=pl.Buffered(1))
    return pl.pallas_call(
        _in_proj_norm_kernel,
        grid=(B, S // rows),
        in_specs=[
            pl.BlockSpec((None, rows, D), lambda b, t: (b, t, 0)),
            pl.BlockSpec((1, D), lambda b, t: (0, 0)),
            pl.BlockSpec((D, n_cols), lambda b, t: (0, 0), **resident),
            pl.BlockSpec((D, LANES), lambda b, t: (0, lr_col_block), **resident),
        ],
        out_specs=[
            pl.BlockSpec((None, rows, n_cols), lambda b, t: (b, t, 0)),
            pl.BlockSpec((None, rows, D), lambda b, t: (b, t, 0)),
            pl.BlockSpec((None, rows, LANES), lambda b, t: (b, t, 0)),
        ],
        out_shape=[jax.ShapeDtypeStruct((B, S, n_cols), BF16), jax.ShapeDtypeStruct((B, S, D), BF16),
                   jax.ShapeDtypeStruct((B, S, LANES), BF16)],
        compiler_params=_params(2),
        name="in_proj_norm",
    )(x, norm_w, w, w)


def _in_proj_kernel(h_ref, w_ref, o_ref):
    for c in range(0, w_ref.shape[1], PROJ_CHUNK):
        o_ref[:, c:c + PROJ_CHUNK] = _dot(h_ref[...], w_ref[:, c:c + PROJ_CHUNK]).astype(o_ref.dtype)


def _in_proj(h, w, col0, n_cols):
    B, S, D = h.shape
    rows = PROJ_NAT_ROWS
    assert col0 % n_cols == 0
    return pl.pallas_call(
        _in_proj_kernel,
        grid=(B, S // rows),
        in_specs=[
            pl.BlockSpec((None, rows, D), lambda b, t: (b, t, 0)),
            pl.BlockSpec((D, n_cols), lambda b, t: (0, col0 // n_cols), pipeline_mode=pl.Buffered(1)),
        ],
        out_specs=pl.BlockSpec((None, rows, n_cols), lambda b, t: (b, t, 0)),
        out_shape=jax.ShapeDtypeStruct((B, S, n_cols), BF16),
        compiler_params=_params(2),
        name="in_proj_rest",
    )(h, w)


def _in_proj_dil_kernel(h_ref, w_ref, cast_ref, o_ref, cast_out_ref, *scratch, dilation, cast_moves):
    rows = h_ref.shape[0]
    chunk_slabs = PROJ_CHUNK // LANES

    for dst, src, width in cast_moves:
        cast_out_ref[:, dst:dst + width] = cast_ref[:, src:src + width].astype(cast_out_ref.dtype)

    passes, rest = [], dilation
    while rest > 1:
        passes.append(min(rest, 4))
        rest //= passes[-1]

    def deinterleave(res, chunk):
        base = (chunk % 2) * chunk_slabs
        for i in range(chunk_slabs):
            scratch[0][base + i] = res[:, i * LANES:(i + 1) * LANES]
        src_ref, n_groups = scratch[0], 1
        for p, stride in enumerate(passes):
            group_rows = rows // n_groups
            last = p == len(passes) - 1
            for g in range(n_groups):
                for r in range(stride):
                    for i in range(chunk_slabs):
                        part = src_ref[base + i, pl.ds(g * group_rows + r, group_rows // stride, stride=stride), :]
                        if last:
                            col = (chunk * chunk_slabs + i) * LANES
                            o_ref[g + n_groups * r, :, col:col + LANES] = part.astype(o_ref.dtype)
                        else:
                            dst = (g + n_groups * r) * (group_rows // stride)
                            scratch[1][base + i, dst:dst + group_rows // stride, :] = part
            if not last:
                src_ref, n_groups = scratch[1], n_groups * stride

    n_chunks = w_ref.shape[1] // PROJ_CHUNK
    res = _dot(h_ref[...], w_ref[:, :PROJ_CHUNK])
    for i in range(n_chunks):
        nxt = _dot(h_ref[...], w_ref[:, (i + 1) * PROJ_CHUNK:(i + 2) * PROJ_CHUNK]) if i + 1 < n_chunks else None
        deinterleave(res, i)
        res = nxt


def _in_proj_dil(h, w, col0, n_cols, dilation, to_cast, cast_moves=None):
    B, S, D = h.shape
    rows = PROJ_ROWS
    nt = S // rows
    assert col0 % n_cols == 0 and n_cols % PROJ_CHUNK == 0
    cast_rows = to_cast.shape[0] // (B * nt)
    assert cast_rows * B * nt == to_cast.shape[0] and cast_rows % BF16_ROWS == 0
    if cast_moves is None:
        cast_moves = ((0, 0, to_cast.shape[1]),)
    assert sum(m[2] for m in cast_moves) == to_cast.shape[1]
    cast_spec = pl.BlockSpec((cast_rows, to_cast.shape[1]), lambda b, t: (b * nt + t, 0))
    n_slabs = 2 * PROJ_CHUNK // LANES
    scratch = [pltpu.VMEM((n_slabs, rows, LANES), F32)]
    if dilation > 4:
        scratch.append(pltpu.VMEM((n_slabs, rows, LANES), F32))
    return pl.pallas_call(
        functools.partial(_in_proj_dil_kernel, dilation=dilation, cast_moves=tuple(cast_moves)),
        grid=(B, nt),
        in_specs=[
            pl.BlockSpec((None, rows, D), lambda b, t: (b, t, 0)),
            pl.BlockSpec((D, n_cols), lambda b, t: (0, col0 // n_cols), pipeline_mode=pl.Buffered(1)),
            cast_spec,
        ],
        out_specs=[
            pl.BlockSpec((None, dilation, rows // dilation, n_cols), lambda b, t: (b, 0, t, 0)),
            cast_spec,
        ],
        out_shape=[jax.ShapeDtypeStruct((B, dilation, S // dilation, n_cols), BF16),
                   jax.ShapeDtypeStruct(to_cast.shape, BF16)],
        scratch_shapes=scratch,
        compiler_params=_params(2),
        name=f"in_proj_d{dilation}",
    )(h, w, to_cast)


def _pack_w_in_kernel(src_ref, valid_ref, wt_ref, o_ref):
    x = wt_ref[...]
    row = lax.broadcasted_iota(jnp.int32, x.shape, 0)
    x = jnp.where(row < valid_ref[pl.program_id(0)], x, 0.0)
    o_ref[...] = x.T.astype(o_ref.dtype)


def _pack_w_in(wt, segments, n_out):
    N, D = wt.shape
    n_blocks = n_out // PACK_COLS
    src = np.zeros((n_blocks,), np.int32)
    valid = np.zeros((n_blocks,), np.int32)
    for dst, s, width in segments:
        assert dst % PACK_COLS == 0
        for i in range(dst // PACK_COLS, -(-(dst + width) // PACK_COLS)):
            start = s + i * PACK_COLS - dst
            assert start % SUBLANES == 0 and start + PACK_COLS <= N
            src[i] = start // SUBLANES
            valid[i] = min(PACK_COLS, dst + width - i * PACK_COLS)
    return pl.pallas_call(
        _pack_w_in_kernel,
        grid_spec=pltpu.PrefetchScalarGridSpec(
            num_scalar_prefetch=2,
            grid=(n_blocks,),
            in_specs=[pl.BlockSpec((pl.Element(PACK_COLS), pl.Element(D)),
                                   lambda i, src, valid: (src[i] * SUBLANES, 0))],
            out_specs=pl.BlockSpec((D, PACK_COLS), lambda i, src, valid: (0, i)),
        ),
        out_shape=jax.ShapeDtypeStruct((D, n_out), BF16),
        compiler_params=_params(1),
        name="pack_w_in",
    )(jnp.asarray(src), jnp.asarray(valid), wt)


def _gla_kernel(q_ref, k_ref, v_ref, g_ref, lr_ref, wup_ref, bgk_ref, nw_ref, *rest, n_casts):
    cast_refs, o_ref = rest[:n_casts], rest[n_casts]
    cast_out_refs, s_ref = rest[n_casts + 1:2 * n_casts + 1], rest[2 * n_casts + 1]
    C = GLA_BLOCK
    for src_ref, dst_ref in zip(cast_refs, cast_out_refs):
        dst_ref[...] = src_ref[...].astype(dst_ref.dtype)

    @pl.when(pl.program_id(1) == 0)
    def _():
        s_ref[...] = jnp.zeros_like(s_ref)

    row = lax.broadcasted_iota(jnp.int32, (C, C), 0)
    col = lax.broadcasted_iota(jnp.int32, (C, C), 1)
    causal = row >= col
    tri = jnp.where(causal, 1.0, 0.0).astype(BF16)
    ones = jnp.ones((C, LANES), BF16)
    scale = GLA_DK ** -0.5
    n_blk = q_ref.shape[0] // C
    heads = range(GLA_HEADS)
    ks = [slice(h * GLA_DK, (h + 1) * GLA_DK) for h in heads]
    vs = [slice(h * GLA_DV, (h + 1) * GLA_DV) for h in heads]
    units = [(blk, h) for blk in range(n_blk) for h in heads]
    rows = {u: slice(u[0] * C, (u[0] + 1) * C) for u in units}
    gks, b_cum, tot, att, q_dec, k_dec = {}, {}, {}, {}, {}, {}
    o_intra, q_in, k_end, decay = {}, {}, {}, {}
    for u in units:
        pre = _dot(lr_ref[rows[u], :], wup_ref[:, ks[u[1]]]) + bgk_ref[:, ks[u[1]]]
        log_sig = jnp.minimum(pre, 0.0) - jnp.log(1.0 + jnp.exp(-jnp.abs(pre)))
        gks[u] = _split_bf16(log_sig * (1.0 / GLA_GATE_NORM))
    for u in units:
        gk_hi, gk_lo = gks[u]
        b_cum[u] = _dot(tri, gk_hi) + _dot(tri, gk_lo)
        tot[u] = _dot_t(gk_hi, ones, ((0,), (0,))) + _dot_t(gk_lo, ones, ((0,), (0,)))
    for u in units:
        b = b_cum[u]
        b_mid = b[C // 2 - 1:C // 2, :]
        q_dec[u] = q_ref[rows[u], ks[u[1]]].astype(F32) * jnp.exp(b - b_mid)
        k_dec[u] = k_ref[rows[u], ks[u[1]]].astype(F32) * jnp.exp(b_mid - b)
        att[u] = _dot_t((q_dec[u] * scale).astype(BF16), k_dec[u].astype(BF16), ((1,), (1,)))
    for u in units:
        b = b_cum[u]
        b_last = b[C - 1:C, :]
        b_mid = b[C // 2 - 1:C // 2, :]
        o_intra[u] = _dot(jnp.where(causal, att[u], 0.0).astype(BF16), v_ref[rows[u], vs[u[1]]])
        q_in[u] = (q_dec[u] * (jnp.exp(b_mid) * scale)).astype(BF16)
        k_end[u] = (k_dec[u] * jnp.exp(b_last - b_mid)).astype(BF16)
        decay[u] = jnp.concatenate([jnp.exp(tot[u])] * (GLA_DV // LANES), axis=1)
    for blk in range(n_blk):
        rs = slice(blk * C, (blk + 1) * C)
        os = []
        for h in heads:
            s_old = s_ref[h]
            os.append(o_intra[blk, h] + _dot(q_in[blk, h], s_old.astype(BF16)))
            s_ref[h] = decay[blk, h] * s_old + _dot_t(k_end[blk, h], v_ref[rs, vs[h]], ((0,), (0,)))
        ys = []
        for h, o in zip(heads, os):
            ms = jnp.mean(o * o, axis=-1, keepdims=True)
            g = g_ref[rs, vs[h]].astype(F32)
            y = (o * lax.rsqrt(ms + RMS_EPS) * nw_ref[...]) * (g * jax.nn.sigmoid(g))
            ys.append(y.astype(o_ref.dtype))
        o_ref[rs, :] = jnp.concatenate(ys, axis=1)


def _gla(z, lr, wup, bgk, norm_w, B, S, to_cast):
    C = GLA_STEP_ROWS
    nc = S // C
    cast_specs = []
    for w in to_cast:
        rows = w.shape[0] // (B * nc)
        assert rows * B * nc == w.shape[0] and rows % BF16_ROWS == 0
        cast_specs.append(pl.BlockSpec((rows, w.shape[1]), lambda b, c: (b * nc + c, 0)))
    res = pl.pallas_call(
        functools.partial(_gla_kernel, n_casts=len(to_cast)),
        grid=(B, nc),
        in_specs=[
            pl.BlockSpec((C, GLA_QK_W), lambda b, c: (b * nc + c, 0)),
            pl.BlockSpec((C, GLA_QK_W), lambda b, c: (b * nc + c, 1)),
            pl.BlockSpec((C, GLA_V_W), lambda b, c: (b * nc + c, 1)),
            pl.BlockSpec((C, GLA_V_W), lambda b, c: (b * nc + c, 2)),
            pl.BlockSpec((C, LANES), lambda b, c: (b * nc + c, 0)),
            pl.BlockSpec((LANES, GLA_QK_W), lambda b, c: (0, 0)),
            pl.BlockSpec((1, GLA_QK_W), lambda b, c: (0, 0)),
            pl.BlockSpec((1, GLA_DV), lambda b, c: (0, 0)),
        ] + cast_specs,
        out_specs=[pl.BlockSpec((C, GLA_V_W), lambda b, c: (b * nc + c, 0))] + cast_specs,
        out_shape=[jax.ShapeDtypeStruct((B * S, GLA_V_W), BF16)]
        + [jax.ShapeDtypeStruct(w.shape, BF16) for w in to_cast],
        scratch_shapes=[pltpu.VMEM((GLA_HEADS, GLA_DK, GLA_DV), F32)],
        compiler_params=_params(2),
        name="gla",
    )(z, z, z, z, lr, wup, bgk, norm_w, *to_cast)
    return res[0], res[1:]


def _dil_attn_kernel(q_ref, k_ref, v_ref, bucket_ref, table_ref, o_ref, lse_ref, bias_ref, *,
                     n_cls, n_blk, buckets, head0):
    scale = DIL_HEAD_DIM ** -0.5
    lane = lax.broadcasted_iota(jnp.int32, (DIL_BLOCK, LANES), 1)

    @pl.when((pl.program_id(0) == 0) & (pl.program_id(1) == 0))
    def _():
        bucket = bucket_ref[...]
        for h in range(DIL_SLOTS):
            bias = jnp.full(bucket.shape, NEG_INF, F32)
            for k in buckets:
                bias = jnp.where(bucket == k, table_ref[k, head0 + h] * LOG2E, bias)
            bias_ref[h] = bias

    def blocks(where, first):
        n_keys = DIL_BLOCK if first else 2 * DIL_BLOCK
        ones = jnp.ones((n_keys, DIL_HEAD_DIM), BF16)
        cols = [slice(h * DIL_HEAD_DIM, (h + 1) * DIL_HEAD_DIM) for h in range(DIL_SLOTS)]
        rows_q = [pl.ds(start, DIL_BLOCK) for _, start in where]
        rows_k = [rq if first else pl.ds(start - DIL_BLOCK, 2 * DIL_BLOCK) for rq, (_, start) in zip(rows_q, where)]
        units = [(u, h) for u in range(len(where)) for h in range(DIL_SLOTS)]
        scores = [
            _dot_t(q_ref[where[u][0], rows_q[u], cols[h]], k_ref[where[u][0], rows_k[u], cols[h]], ((1,), (1,)))
            * (scale * LOG2E) + (bias_ref[h, :, DIL_BLOCK:] if first else bias_ref[h])
            for u, h in units]
        maxes = [jnp.max(s, axis=-1, keepdims=True) for s in scores]
        probs = [jnp.exp2(s - m).astype(BF16) for s, m in zip(scores, maxes)]
        num_ls = [_dot(p, jnp.concatenate([v_ref[where[u][0], rows_k[u], cols[h]], ones], axis=1))
                  for p, (u, h) in zip(probs, units)]
        for u, (c, _) in enumerate(where):
            outs = []
            lse_tile = jnp.zeros((DIL_BLOCK, LANES), F32)
            for h in range(DIL_SLOTS):
                num_l, m = num_ls[u * DIL_SLOTS + h], maxes[u * DIL_SLOTS + h]
                l = num_l[:, DIL_HEAD_DIM:]
                outs.append((num_l[:, :DIL_HEAD_DIM] / l).astype(o_ref.dtype))
                lse_tile = jnp.where(lane == h, m * LN2 + jnp.log(l), lse_tile)
            o_ref[c, rows_q[u], :] = jnp.concatenate(outs, axis=1)
            lse_ref[c, rows_q[u], :] = lse_tile

    def row(n):
        return pl.multiple_of(n * DIL_BLOCK, DIL_BLOCK)

    if n_blk == 1:
        def pair_of_classes(i, carry):
            blocks([(2 * i, 0), (2 * i + 1, 0)], True)
            return carry
        lax.fori_loop(0, n_cls // 2, pair_of_classes, 0)
    else:
        def one_class(c, carry):
            blocks([(c, 0)], True)

            def pair_of_blocks(i, inner):
                blocks([(c, row(2 * i + 1)), (c, row(2 * i + 2))], False)
                return inner
            lax.fori_loop(0, (n_blk - 1) // 2, pair_of_blocks, 0)
            if (n_blk - 1) % 2:
                blocks([(c, (n_blk - 1) * DIL_BLOCK)], False)
            return carry

        if n_cls == 1:
            one_class(0, 0)
        else:
            lax.fori_loop(0, n_cls, one_class, 0)


def _band_buckets(dilation):
    a_idx = np.arange(DIL_BLOCK)[:, None]
    c_idx = np.arange(2 * DIL_BLOCK)[None, :]
    steps = DIL_BLOCK + a_idx - c_idx
    in_band = (steps >= 0) & (steps <= DIL_STEPS)
    bucket = _t5_bucket(np.clip(steps, 0, None) * dilation)
    return np.where(in_band, bucket, -1).astype(np.int32)


def _dil_attn(qkv, col_block, rel_bias, group):
    B, d, L, _ = qkv.shape
    n_cls = d
    bucket = _band_buckets(d)
    buckets = tuple(int(k) for k in np.unique(bucket) if k >= 0)

    def part(p):
        return pl.BlockSpec((None, n_cls, L, DIL_W), lambda b, r: (b, r, 0, col_block + p))

    return pl.pallas_call(
        functools.partial(_dil_attn_kernel, n_cls=n_cls, n_blk=L // DIL_BLOCK, buckets=buckets,
                          head0=group * DIL_SLOTS),
        grid=(B, d // n_cls),
        in_specs=[
            part(0), part(1), part(2),
            pl.BlockSpec((DIL_BLOCK, 2 * DIL_BLOCK), lambda b, r: (0, 0)),
            pl.BlockSpec(memory_space=pltpu.SMEM),
        ],
        scratch_shapes=[pltpu.VMEM((DIL_SLOTS, DIL_BLOCK, 2 * DIL_BLOCK), F32)],
        out_specs=[
            pl.BlockSpec((None, n_cls, L, DIL_W), lambda b, r: (b, r, 0, 0)),
            pl.BlockSpec((None, n_cls, L, LANES), lambda b, r: (b, r, 0, 0)),
        ],
        out_shape=[
            jax.ShapeDtypeStruct((B, d, L, DIL_W), BF16),
            jax.ShapeDtypeStruct((B, d, L, LANES), F32),
        ],
        compiler_params=_params(2),
        name=f"dil_attn_d{d}",
    )(qkv, qkv, qkv, jnp.asarray(bucket), rel_bias)


def _t5_bucket(dist):
    max_exact = REL_BUCKETS // 2
    d = np.maximum(dist, 1).astype(np.float64)
    large = max_exact + (np.log(d / max_exact) / np.log(REL_MAX_DIST / max_exact)
                         * (REL_BUCKETS - max_exact)).astype(np.int64)
    large = np.minimum(large, REL_BUCKETS - 1)
    return np.where(dist < max_exact, dist, large).astype(np.int32)


def _merge_kernel(og_ref, o1_ref, o4_ref, o16_ref, l1_ref, l4_ref, l16_ref, ga_ref, gb_ref,
                  wa_ref, wb_ref, bg_ref, o_ref, s4_ref, s16_ref, sl4_ref, sl16_ref):
    T = og_ref.shape[0]
    a = _dot(og_ref[...], wa_ref[...])
    for d, grp_ref, lse_ref, s_ref, sl_ref in ((4, o4_ref, l4_ref, s4_ref, sl4_ref),
                                                (16, o16_ref, l16_ref, s16_ref, sl16_ref)):
        for r in range(d):
            rows = pl.ds(r, T // d, stride=d)
            sl_ref[rows, :] = lse_ref[r]
            for c in range(DIL_SLOTS):
                s_ref[c, rows, :] = grp_ref[r, :, c * LANES:(c + 1) * LANES].astype(F32)
    lses = (l1_ref[...], sl4_ref[...], sl16_ref[...])
    m = jnp.maximum(jnp.maximum(lses[0], lses[1]), lses[2])
    es = [jnp.exp(l - m) for l in lses]
    inv = 1.0 / (es[0] + es[1] + es[2])
    ws = [e * inv for e in es]
    heads = []
    for c in range(DIL_SLOTS):
        cs = slice(c * DIL_HEAD_DIM, (c + 1) * DIL_HEAD_DIM)
        mix = (ws[0][:, c:c + 1] * o1_ref[:, cs].astype(F32)
               + ws[1][:, c:c + 1] * s4_ref[c]
               + ws[2][:, c:c + 1] * s16_ref[c])
        heads.append(mix.astype(BF16))
    bm = _dot(jnp.concatenate(heads, axis=1), wb_ref[...])
    gate_a = jax.nn.sigmoid(ga_ref[...].astype(F32) + bg_ref[:, :D_MODEL])
    gate_b = jax.nn.sigmoid(gb_ref[...].astype(F32) + bg_ref[:, D_MODEL:])
    o_ref[...] = (gate_a * a + gate_b * bm).astype(o_ref.dtype)


def _merge(o_gla, groups, z, gate_col_block, B, S, wa, wb, b_gate, bm):
    nt = S // bm
    (o1, l1), (o4, l4), (o16, l16) = groups
    const = lambda i: (0, 0)
    resident = dict(pipeline_mode=pl.Buffered(1))

    def natural(width):
        return pl.BlockSpec((None, None, bm, width), lambda i: (i // nt, 0, i % nt, 0))

    def class_major(d, width):
        return pl.BlockSpec((None, d, bm // d, width), lambda i: (i // nt, 0, i % nt, 0))

    return pl.pallas_call(
        _merge_kernel,
        grid=(B * nt,),
        in_specs=[
            pl.BlockSpec((bm, GLA_V_W), lambda i: (i, 0)),
            natural(DIL_W), class_major(4, DIL_W), class_major(16, DIL_W),
            natural(LANES), class_major(4, LANES), class_major(16, LANES),
            pl.BlockSpec((bm, D_MODEL), lambda i: (i, gate_col_block)),
            pl.BlockSpec((bm, D_MODEL), lambda i: (i, gate_col_block + 1)),
            pl.BlockSpec((GLA_V_W, D_MODEL), const, **resident),
            pl.BlockSpec((DIL_W, D_MODEL), const, **resident),
            pl.BlockSpec((1, 2 * D_MODEL), const),
        ],
        out_specs=pl.BlockSpec((bm, D_MODEL), lambda i: (i, 0)),
        out_shape=jax.ShapeDtypeStruct((B * S, D_MODEL), BF16),
        scratch_shapes=[
            pltpu.VMEM((DIL_SLOTS, bm, LANES), F32), pltpu.VMEM((DIL_SLOTS, bm, LANES), F32),
            pltpu.VMEM((bm, LANES), F32), pltpu.VMEM((bm, LANES), F32),
        ],
        compiler_params=_params(1),
        name="merge_branches",
    )(o_gla, o1, o4, o16, l1, l4, l16, z, z, wa, wb, b_gate)


def _out_proj_kernel(m_ref, x_ref, w_ref, o_ref):
    for c in range(0, w_ref.shape[1], PROJ_CHUNK):
        cs = slice(c, c + PROJ_CHUNK)
        o_ref[:, cs] = x_ref[:, cs] + _dot(m_ref[...], w_ref[:, cs])


def _out_proj(merged, x2d, w):
    M, D = x2d.shape
    rows = OUT_PROJ_ROWS
    return pl.pallas_call(
        _out_proj_kernel,
        grid=(M // rows,),
        in_specs=[
            pl.BlockSpec((rows, D), lambda i: (i, 0)),
            pl.BlockSpec((rows, D), lambda i: (i, 0)),
            pl.BlockSpec((D, D), lambda i: (0, 0), pipeline_mode=pl.Buffered(1)),
        ],
        out_specs=pl.BlockSpec((rows, D), lambda i: (i, 0)),
        out_shape=jax.ShapeDtypeStruct((M, D), F32),
        compiler_params=_params(1),
        name="out_proj",
    )(merged, x2d, w)


def _ffn_part_kernel(*refs, first, last):
    refs = list(refs)
    x_ref = refs.pop(0)
    h_in_ref = None if first else refs.pop(0)
    nw_ref = refs.pop(0) if first else None
    wgu_ref, wd_ref = refs.pop(0), refs.pop(0)
    fw_ref = refs.pop(0) if last else None
    o_ref = refs.pop(0)
    h_ref = refs.pop(0) if first else h_in_ref
    bf = FFN_HIDDEN_BLOCK

    x = x_ref[...]
    if first:
        ms = jnp.mean(x * x, axis=-1, keepdims=True)
        h_ref[...] = (x * lax.rsqrt(ms + RMS_EPS) * nw_ref[...]).astype(h_ref.dtype)
    o_ref[...] = x
    for f in range(wd_ref.shape[0] // bf):
        gate_up = _dot(h_ref[...], wgu_ref[:, 2 * f * bf:(2 * f + 2) * bf])
        gate, up = gate_up[:, :bf], gate_up[:, bf:]
        act = (gate * jax.nn.sigmoid(gate) * up).astype(BF16)
        for c in range(0, D_MODEL, FFN_OUT_CHUNK):
            o_ref[:, c:c + FFN_OUT_CHUNK] += _dot(act, wd_ref[f * bf:(f + 1) * bf, c:c + FFN_OUT_CHUNK])
    if last:
        y = o_ref[...]
        ms = jnp.mean(y * y, axis=-1, keepdims=True)
        o_ref[...] = y * lax.rsqrt(ms + RMS_EPS) * fw_ref[...]


def _ffn_part(x, h, norm_w, w_in, w_out, final_w, blocks, first, last):
    M = x.shape[0]
    rows, bf = FFN_PART_ROWS, FFN_HIDDEN_BLOCK
    n_hidden = (blocks[1] - blocks[0]) * bf
    row_spec = pl.BlockSpec((rows, D_MODEL), lambda i: (i, 0))
    vec_spec = pl.BlockSpec((1, D_MODEL), lambda i: (0, 0))
    resident = dict(pipeline_mode=pl.Buffered(1))
    wgu_spec = pl.BlockSpec((pl.Element(D_MODEL), pl.Element(2 * n_hidden)),
                            lambda i: (0, 2 * blocks[0] * bf), **resident)
    wd_spec = pl.BlockSpec((pl.Element(n_hidden), pl.Element(D_MODEL)), lambda i: (blocks[0] * bf, 0), **resident)
    args, in_specs = [x], [row_spec]
    if first:
        args.append(norm_w), in_specs.append(vec_spec)
    else:
        args.append(h), in_specs.append(row_spec)
    args += [w_in, w_out]
    in_specs += [wgu_spec, wd_spec]
    if last:
        args.append(final_w), in_specs.append(vec_spec)
    out_shape = [jax.ShapeDtypeStruct((M, D_MODEL), F32)]
    out_specs = [row_spec]
    if first:
        out_shape.append(jax.ShapeDtypeStruct((M, D_MODEL), BF16)), out_specs.append(row_spec)
    return pl.pallas_call(
        functools.partial(_ffn_part_kernel, first=first, last=last),
        grid=(M // rows,),
        in_specs=in_specs,
        out_specs=out_specs,
        out_shape=out_shape,
        compiler_params=_params(1),
        name=f"swiglu_{blocks[0]}_{blocks[1]}",
    )(*args)


def _ffn_weight_moves(bf):
    moves = []
    for f in range(D_FF // bf):
        moves.append((2 * f * bf, f * bf, bf))
        moves.append(((2 * f + 1) * bf, D_FF + f * bf, bf))
    return moves


def kernel(x, attn_norm, w_in, w_gk_up, b_gk, gla_norm, b_gate, w_branch_gla, w_branch_dil,
           w_out, ffn_norm, w_ffn_in, w_ffn_out, rel_bias, final_norm):
    B, S, D = x.shape
    assert D == D_MODEL and S % (16 * DIL_BLOCK) == 0 and attn_norm.shape[0] == 1

    o_lr = 2 * GLA_QK_W + 2 * GLA_V_W
    o_dil = o_lr + GLA_LOWRANK
    o_gate = o_dil + 9 * DIL_W
    n_gla = o_lr
    n_rest = 2 * D_MODEL + 3 * DIL_W
    n_dil = 3 * DIL_W
    lr_col = n_gla
    col_rest = -(-(n_gla + LANES) // n_rest) * n_rest
    col_d4 = -(-(col_rest + n_rest) // n_dil) * n_dil
    col_d16 = col_d4 + n_dil
    w_all = _pack_w_in(
        jnp.swapaxes(w_in, 1, 2)[0],
        segments=((0, 0, n_gla), (lr_col, o_lr, GLA_LOWRANK), (col_rest, o_gate, 2 * D_MODEL),
                  (col_rest + 2 * D_MODEL, o_dil, 3 * DIL_W), (col_d4, o_dil + 3 * DIL_W, 6 * DIL_W)),
        n_out=col_d16 + n_dil)

    z_gla, h, lr = _in_proj_norm(x, attn_norm[0][None, :], w_all, n_gla, lr_col // LANES)
    z_rest = _in_proj(h, w_all, col_rest, n_rest)
    z_d4, w_ffn_in_bf = _in_proj_dil(h, w_all, col_d4, n_dil, 4, w_ffn_in[0], _ffn_weight_moves(FFN_HIDDEN_BLOCK))
    z_d16, w_ffn_out_bf = _in_proj_dil(h, w_all, col_d16, n_dil, 16, w_ffn_out[0])

    wup = jnp.concatenate([w_gk_up[0], jnp.zeros((LANES - GLA_LOWRANK, GLA_QK_W), F32)], axis=0).astype(BF16)
    o_gla, merge_w = _gla(z_gla.reshape(B * S, n_gla), lr.reshape(B * S, LANES), wup, b_gk[0][None, :],
                          gla_norm[0][None, :], B, S, (w_branch_gla[0], w_branch_dil[0], w_out[0]))

    z_d1 = z_rest.reshape(B, 1, S, n_rest)
    groups = [_dil_attn(zg, col_block, rel_bias, gi)
              for gi, (zg, col_block) in enumerate(((z_d1, 2 * D_MODEL // DIL_W), (z_d4, 0), (z_d16, 0)))]
    merged = _merge(o_gla, groups, z_rest.reshape(B * S, n_rest), 0, B, S, merge_w[0], merge_w[1],
                    b_gate[0][None, :], MERGE_ROWS)
    x1 = _out_proj(merged, x.reshape(B * S, D), merge_w[2])
    acc, h_ffn = x1, None
    for i, blocks in enumerate(FFN_PARTS):
        first, last = i == 0, i == len(FFN_PARTS) - 1
        res = _ffn_part(acc, h_ffn, ffn_norm[0][None, :], w_ffn_in_bf, w_ffn_out_bf, final_norm[None, :],
                        blocks, first, last)
        acc, h_ffn = (res[0], res[1]) if first else (res[0], h_ffn)
    return acc.reshape(B, S, D)
```

```python
import functools

import numpy as np
import jax
import jax.numpy as jnp
from jax import lax
from jax.experimental import pallas as pl
from jax.experimental.pallas import tpu as pltpu

F32 = jnp.float32
BF16 = jnp.bfloat16

D_MODEL = 2048
GLA_HEADS = 4
GLA_DK = 256
GLA_DV = 512
GLA_QK_W = GLA_HEADS * GLA_DK
GLA_V_W = GLA_HEADS * GLA_DV
GLA_LOWRANK = 16
GLA_GATE_NORM = 16.0
DIL_DILATIONS = (1, 4, 16)
DIL_SLOTS = 8
DIL_HEAD_DIM = 128
DIL_W = DIL_SLOTS * DIL_HEAD_DIM
DIL_STEPS = 128
DIL_BLOCK = 128
REL_BUCKETS = 32
REL_MAX_DIST = 2048
D_FF = 5632
RMS_EPS = 1e-6
NEG_INF = -1e30
LOG2E = 1.4426950408889634
LN2 = 0.6931471805599453

LANES = 128
SUBLANES = 8
BF16_ROWS = 16
MXU_WIDTH = 256
VMEM_LIMIT = 56 * 1024 * 1024

PACK_COLS = 512
PROJ_NAT_ROWS = 512
PROJ_ROWS = 1024
PROJ_CHUNK = MXU_WIDTH
GLA_BLOCK = 128
GLA_STEP_ROWS = 512
MERGE_ROWS = 512
OUT_PROJ_ROWS = 1024
FFN_PART_ROWS = 512
FFN_HIDDEN_BLOCK = 512
FFN_PARTS = ((0, 4), (4, 8), (8, 11))
FFN_OUT_CHUNK = 512


def _params(n_axes):
    return pltpu.CompilerParams(dimension_semantics=("arbitrary",) * n_axes, vmem_limit_bytes=VMEM_LIMIT)


def _dot(a, b):
    return jnp.dot(a, b, preferred_element_type=F32)


def _dot_t(a, b, dims):
    return lax.dot_general(a, b, (dims, ((), ())), preferred_element_type=F32)


def _split_bf16(x):
    hi = x.astype(BF16)
    lo = (x - hi.astype(F32)).astype(BF16)
    return hi, lo


def _in_proj_norm_kernel(x_ref, g_ref, w_ref, w_lr_ref, o_ref, h_ref, lr_ref):
    x = x_ref[...]
    ms = jnp.mean(x * x, axis=-1, keepdims=True)
    h_ref[...] = (x * lax.rsqrt(ms + RMS_EPS) * g_ref[...]).astype(h_ref.dtype)
    lr_ref[...] = _dot(h_ref[...], w_lr_ref[...]).astype(lr_ref.dtype)
    for c in range(0, w_ref.shape[1], PROJ_CHUNK):
        o_ref[:, c:c + PROJ_CHUNK] = _dot(h_ref[...], w_ref[:, c:c + PROJ_CHUNK]).astype(o_ref.dtype)


def _in_proj_norm(x, norm_w, w, n_cols, lr_col_block):
    B, S, D = x.shape
    rows = PROJ_NAT_ROWS
    resident = dict(pipeline_mode=pl.Buffered(1))
    return pl.pallas_call(
        _in_proj_norm_kernel,
        grid=(B, S // rows),
        in_specs=[
            pl.BlockSpec((None, rows, D), lambda b, t: (b, t, 0)),
            pl.BlockSpec((1, D), lambda b, t: (0, 0)),
            pl.BlockSpec((D, n_cols), lambda b, t: (0, 0), **resident),
            pl.BlockSpec((D, LANES), lambda b, t: (0, lr_col_block), **resident),
        ],
        out_specs=[
            pl.BlockSpec((None, rows, n_cols), lambda b, t: (b, t, 0)),
            pl.BlockSpec((None, rows, D), lambda b, t: (b, t, 0)),
            pl.BlockSpec((None, rows, LANES), lambda b, t: (b, t, 0)),
        ],
        out_shape=[jax.ShapeDtypeStruct((B, S, n_cols), BF16), jax.ShapeDtypeStruct((B, S, D), BF16),
                   jax.ShapeDtypeStruct((B, S, LANES), BF16)],
        compiler_params=_params(2),
        name="in_proj_norm",
    )(x, norm_w, w, w)


def _in_proj_kernel(h_ref, w_ref, o_ref):
    for c in range(0, w_ref.shape[1], PROJ_CHUNK):
        o_ref[:, c:c + PROJ_CHUNK] = _dot(h_ref[...], w_ref[:, c:c + PROJ_CHUNK]).astype(o_ref.dtype)


def _in_proj(h, w, col0, n_cols):
    B, S, D = h.shape
    rows = PROJ_NAT_ROWS
    assert col0 % n_cols == 0
    return pl.pallas_call(
        _in_proj_kernel,
        grid=(B, S // rows),
        in_specs=[
            pl.BlockSpec((None, rows, D), lambda b, t: (b, t, 0)),
            pl.BlockSpec((D, n_cols), lambda b, t: (0, col0 // n_cols), pipeline_mode=pl.Buffered(1)),
        ],
        out_specs=pl.BlockSpec((None, rows, n_cols), lambda b, t: (b, t, 0)),
        out_shape=jax.ShapeDtypeStruct((B, S, n_cols), BF16),
        compiler_params=_params(2),
        name="in_proj_rest",
    )(h, w)


def _in_proj_dil_kernel(h_ref, w_ref, cast_ref, o_ref, cast_out_ref, *scratch, dilation, cast_moves):
    rows = h_ref.shape[0]
    chunk_slabs = PROJ_CHUNK // LANES

    for dst, src, width in cast_moves:
        cast_out_ref[:, dst:dst + width] = cast_ref[:, src:src + width].astype(cast_out_ref.dtype)

    passes, rest = [], dilation
    while rest > 1:
        passes.append(min(rest, 4))
        rest //= passes[-1]

    def deinterleave(res, chunk):
        base = (chunk % 2) * chunk_slabs
        for i in range(chunk_slabs):
            scratch[0][base + i] = res[:, i * LANES:(i + 1) * LANES]
        src_ref, n_groups = scratch[0], 1
        for p, stride in enumerate(passes):
            group_rows = rows // n_groups
            last = p == len(passes) - 1
            for g in range(n_groups):
                for r in range(stride):
                    for i in range(chunk_slabs):
                        part = src_ref[base + i, pl.ds(g * group_rows + r, group_rows // stride, stride=stride), :]
                        if last:
                            col = (chunk * chunk_slabs + i) * LANES
                            o_ref[g + n_groups * r, :, col:col + LANES] = part.astype(o_ref.dtype)
                        else:
                            dst = (g + n_groups * r) * (group_rows // stride)
                            scratch[1][base + i, dst:dst + group_rows // stride, :] = part
            if not last:
                src_ref, n_groups = scratch[1], n_groups * stride

    n_chunks = w_ref.shape[1] // PROJ_CHUNK
    res = _dot(h_ref[...], w_ref[:, :PROJ_CHUNK])
    for i in range(n_chunks):
        nxt = _dot(h_ref[...], w_ref[:, (i + 1) * PROJ_CHUNK:(i + 2) * PROJ_CHUNK]) if i + 1 < n_chunks else None
        deinterleave(res, i)
        res = nxt


def _in_proj_dil(h, w, col0, n_cols, dilation, to_cast, cast_moves=None):
    B, S, D = h.shape
    rows = PROJ_ROWS
    nt = S // rows
    assert col0 % n_cols == 0 and n_cols % PROJ_CHUNK == 0
    cast_rows = to_cast.shape[0] // (B * nt)
    assert cast_rows * B * nt == to_cast.shape[0] and cast_rows % BF16_ROWS == 0
    if cast_moves is None:
        cast_moves = ((0, 0, to_cast.shape[1]),)
    assert sum(m[2] for m in cast_moves) == to_cast.shape[1]
    cast_spec = pl.BlockSpec((cast_rows, to_cast.shape[1]), lambda b, t: (b * nt + t, 0))
    n_slabs = 2 * PROJ_CHUNK // LANES
    scratch = [pltpu.VMEM((n_slabs, rows, LANES), F32)]
    if dilation > 4:
        scratch.append(pltpu.VMEM((n_slabs, rows, LANES), F32))
    return pl.pallas_call(
        functools.partial(_in_proj_dil_kernel, dilation=dilation, cast_moves=tuple(cast_moves)),
        grid=(B, nt),
        in_specs=[
            pl.BlockSpec((None, rows, D), lambda b, t: (b, t, 0)),
            pl.BlockSpec((D, n_cols), lambda b, t: (0, col0 // n_cols), pipeline_mode=pl.Buffered(1)),
            cast_spec,
        ],
        out_specs=[
            pl.BlockSpec((None, dilation, rows // dilation, n_cols), lambda b, t: (b, 0, t, 0)),
            cast_spec,
        ],
        out_shape=[jax.ShapeDtypeStruct((B, dilation, S // dilation, n_cols), BF16),
                   jax.ShapeDtypeStruct(to_cast.shape, BF16)],
        scratch_shapes=scratch,
        compiler_params=_params(2),
        name=f"in_proj_d{dilation}",
    )(h, w, to_cast)


def _pack_w_in_kernel(src_ref, valid_ref, wt_ref, o_ref):
    x = wt_ref[...]
    row = lax.broadcasted_iota(jnp.int32, x.shape, 0)
    x = jnp.where(row < valid_ref[pl.program_id(0)], x, 0.0)
    o_ref[...] = x.T.astype(o_ref.dtype)


def _pack_w_in(wt, segments, n_out):
    N, D = wt.shape
    n_blocks = n_out // PACK_COLS
    src = np.zeros((n_blocks,), np.int32)
    valid = np.zeros((n_blocks,), np.int32)
    for dst, s, width in segments:
        assert dst % PACK_COLS == 0
        for i in range(dst // PACK_COLS, -(-(dst + width) // PACK_COLS)):
            start = s + i * PACK_COLS - dst
            assert start % SUBLANES == 0 and start + PACK_COLS <= N
            src[i] = start // SUBLANES
            valid[i] = min(PACK_COLS, dst + width - i * PACK_COLS)
    return pl.pallas_call(
        _pack_w_in_kernel,
        grid_spec=pltpu.PrefetchScalarGridSpec(
            num_scalar_prefetch=2,
            grid=(n_blocks,),
            in_specs=[pl.BlockSpec((pl.Element(PACK_COLS), pl.Element(D)),
                                   lambda i, src, valid: (src[i] * SUBLANES, 0))],
            out_specs=pl.BlockSpec((D, PACK_COLS), lambda i, src, valid: (0, i)),
        ),
        out_shape=jax.ShapeDtypeStruct((D, n_out), BF16),
        compiler_params=_params(1),
        name="pack_w_in",
    )(jnp.asarray(src), jnp.asarray(valid), wt)


def _gla_kernel(q_ref, k_ref, v_ref, g_ref, lr_ref, wup_ref, bgk_ref, nw_ref, *rest, n_casts):
    cast_refs, o_ref = rest[:n_casts], rest[n_casts]
    cast_out_refs, s_ref = rest[n_casts + 1:2 * n_casts + 1], rest[2 * n_casts + 1]
    C = GLA_BLOCK
    for src_ref, dst_ref in zip(cast_refs, cast_out_refs):
        dst_ref[...] = src_ref[...].astype(dst_ref.dtype)

    @pl.when(pl.program_id(1) == 0)
    def _():
        s_ref[...] = jnp.zeros_like(s_ref)

    row = lax.broadcasted_iota(jnp.int32, (C, C), 0)
    col = lax.broadcasted_iota(jnp.int32, (C, C), 1)
    causal = row >= col
    tri = jnp.where(causal, 1.0, 0.0).astype(BF16)
    ones = jnp.ones((C, LANES), BF16)
    scale = GLA_DK ** -0.5
    n_blk = q_ref.shape[0] // C
    heads = range(GLA_HEADS)
    ks = [slice(h * GLA_DK, (h + 1) * GLA_DK) for h in heads]
    vs = [slice(h * GLA_DV, (h + 1) * GLA_DV) for h in heads]
    units = [(blk, h) for blk in range(n_blk) for h in heads]
    rows = {u: slice(u[0] * C, (u[0] + 1) * C) for u in units}
    gks, b_cum, tot, att, q_dec, k_dec = {}, {}, {}, {}, {}, {}
    o_intra, q_in, k_end, decay = {}, {}, {}, {}
    for u in units:
        pre = _dot(lr_ref[rows[u], :], wup_ref[:, ks[u[1]]]) + bgk_ref[:, ks[u[1]]]
        log_sig = jnp.minimum(pre, 0.0) - jnp.log(1.0 + jnp.exp(-jnp.abs(pre)))
        gks[u] = _split_bf16(log_sig * (1.0 / GLA_GATE_NORM))
    for u in units:
        gk_hi, gk_lo = gks[u]
        b_cum[u] = _dot(tri, gk_hi) + _dot(tri, gk_lo)
        tot[u] = _dot_t(gk_hi, ones, ((0,), (0,))) + _dot_t(gk_lo, ones, ((0,), (0,)))
    for u in units:
        b = b_cum[u]
        b_mid = b[C // 2 - 1:C // 2, :]
        q_dec[u] = q_ref[rows[u], ks[u[1]]].astype(F32) * jnp.exp(b - b_mid)
        k_dec[u] = k_ref[rows[u], ks[u[1]]].astype(F32) * jnp.exp(b_mid - b)
        att[u] = _dot_t((q_dec[u] * scale).astype(BF16), k_dec[u].astype(BF16), ((1,), (1,)))
    for u in units:
        b = b_cum[u]
        b_last = b[C - 1:C, :]
        o_intra[u] = _dot(jnp.where(causal, att[u], 0.0).astype(BF16), v_ref[rows[u], vs[u[1]]])
        q_in[u] = (q_ref[rows[u], ks[u[1]]].astype(F32) * (jnp.exp(b) * scale)).astype(BF16)
        k_end[u] = (k_ref[rows[u], ks[u[1]]].astype(F32) * jnp.exp(b_last - b)).astype(BF16)
        decay[u] = jnp.concatenate([jnp.exp(tot[u])] * (GLA_DV // LANES), axis=1)
    for blk in range(n_blk):
        rs = slice(blk * C, (blk + 1) * C)
        os = []
        for h in heads:
            s_old = s_ref[h]
            os.append(o_intra[blk, h] + _dot(q_in[blk, h], s_old.astype(BF16)))
            s_ref[h] = decay[blk, h] * s_old + _dot_t(k_end[blk, h], v_ref[rs, vs[h]], ((0,), (0,)))
        ys = []
        for h, o in zip(heads, os):
            ms = jnp.mean(o * o, axis=-1, keepdims=True)
            g = g_ref[rs, vs[h]].astype(F32)
            y = (o * lax.rsqrt(ms + RMS_EPS) * nw_ref[...]) * (g * jax.nn.sigmoid(g))
            ys.append(y.astype(o_ref.dtype))
        o_ref[rs, :] = jnp.concatenate(ys, axis=1)


def _gla(z, lr, wup, bgk, norm_w, B, S, to_cast):
    C = GLA_STEP_ROWS
    nc = S // C
    cast_specs = []
    for w in to_cast:
        rows = w.shape[0] // (B * nc)
        assert rows * B * nc == w.shape[0] and rows % BF16_ROWS == 0
        cast_specs.append(pl.BlockSpec((rows, w.shape[1]), lambda b, c: (b * nc + c, 0)))
    res = pl.pallas_call(
        functools.partial(_gla_kernel, n_casts=len(to_cast)),
        grid=(B, nc),
        in_specs=[
            pl.BlockSpec((C, GLA_QK_W), lambda b, c: (b * nc + c, 0)),
            pl.BlockSpec((C, GLA_QK_W), lambda b, c: (b * nc + c, 1)),
            pl.BlockSpec((C, GLA_V_W), lambda b, c: (b * nc + c, 1)),
            pl.BlockSpec((C, GLA_V_W), lambda b, c: (b * nc + c, 2)),
            pl.BlockSpec((C, LANES), lambda b, c: (b * nc + c, 0)),
            pl.BlockSpec((LANES, GLA_QK_W), lambda b, c: (0, 0)),
            pl.BlockSpec((1, GLA_QK_W), lambda b, c: (0, 0)),
            pl.BlockSpec((1, GLA_DV), lambda b, c: (0, 0)),
        ] + cast_specs,
        out_specs=[pl.BlockSpec((C, GLA_V_W), lambda b, c: (b * nc + c, 0))] + cast_specs,
        out_shape=[jax.ShapeDtypeStruct((B * S, GLA_V_W), BF16)]
        + [jax.ShapeDtypeStruct(w.shape, BF16) for w in to_cast],
        scratch_shapes=[pltpu.VMEM((GLA_HEADS, GLA_DK, GLA_DV), F32)],
        compiler_params=_params(2),
        name="gla",
    )(z, z, z, z, lr, wup, bgk, norm_w, *to_cast)
    return res[0], res[1:]


def _dil_attn_kernel(q_ref, k_ref, v_ref, bucket_ref, table_ref, o_ref, lse_ref, bias_ref, *,
                     n_cls, n_blk, buckets, head0):
    scale = DIL_HEAD_DIM ** -0.5
    lane = lax.broadcasted_iota(jnp.int32, (DIL_BLOCK, LANES), 1)

    @pl.when((pl.program_id(0) == 0) & (pl.program_id(1) == 0))
    def _():
        bucket = bucket_ref[...]
        for h in range(DIL_SLOTS):
            bias = jnp.full(bucket.shape, NEG_INF, F32)
            for k in buckets:
                bias = jnp.where(bucket == k, table_ref[k, head0 + h] * LOG2E, bias)
            bias_ref[h] = bias

    def blocks(where, first):
        n_keys = DIL_BLOCK if first else 2 * DIL_BLOCK
        ones = jnp.ones((n_keys, DIL_HEAD_DIM), BF16)
        cols = [slice(h * DIL_HEAD_DIM, (h + 1) * DIL_HEAD_DIM) for h in range(DIL_SLOTS)]
        rows_q = [pl.ds(start, DIL_BLOCK) for _, start in where]
        rows_k = [rq if first else pl.ds(start - DIL_BLOCK, 2 * DIL_BLOCK) for rq, (_, start) in zip(rows_q, where)]
        units = [(u, h) for u in range(len(where)) for h in range(DIL_SLOTS)]
        scores = [
            _dot_t(q_ref[where[u][0], rows_q[u], cols[h]], k_ref[where[u][0], rows_k[u], cols[h]], ((1,), (1,)))
            * (scale * LOG2E) + (bias_ref[h, :, DIL_BLOCK:] if first else bias_ref[h])
            for u, h in units]
        maxes = [jnp.max(s, axis=-1, keepdims=True) for s in scores]
        probs = [jnp.exp2(s - m).astype(BF16) for s, m in zip(scores, maxes)]
        num_ls = [_dot(p, jnp.concatenate([v_ref[where[u][0], rows_k[u], cols[h]], ones], axis=1))
                  for p, (u, h) in zip(probs, units)]
        for u, (c, _) in enumerate(where):
            outs = []
            lse_tile = jnp.zeros((DIL_BLOCK, LANES), F32)
            for h in range(DIL_SLOTS):
                num_l, m = num_ls[u * DIL_SLOTS + h], maxes[u * DIL_SLOTS + h]
                l = num_l[:, DIL_HEAD_DIM:]
                outs.append((num_l[:, :DIL_HEAD_DIM] / l).astype(o_ref.dtype))
                lse_tile = jnp.where(lane == h, m * LN2 + jnp.log(l), lse_tile)
            o_ref[c, rows_q[u], :] = jnp.concatenate(outs, axis=1)
            lse_ref[c, rows_q[u], :] = lse_tile

    def row(n):
        return pl.multiple_of(n * DIL_BLOCK, DIL_BLOCK)

    if n_blk == 1:
        def pair_of_classes(i, carry):
            blocks([(2 * i, 0), (2 * i + 1, 0)], True)
            return carry
        lax.fori_loop(0, n_cls // 2, pair_of_classes, 0)
    else:
        def one_class(c, carry):
            blocks([(c, 0)], True)

            def pair_of_blocks(i, inner):
                blocks([(c, row(2 * i + 1)), (c, row(2 * i + 2))], False)
                return inner
            lax.fori_loop(0, (n_blk - 1) // 2, pair_of_blocks, 0)
            if (n_blk - 1) % 2:
                blocks([(c, (n_blk - 1) * DIL_BLOCK)], False)
            return carry

        if n_cls == 1:
            one_class(0, 0)
        else:
            lax.fori_loop(0, n_cls, one_class, 0)


def _band_buckets(dilation):
    a_idx = np.arange(DIL_BLOCK)[:, None]
    c_idx = np.arange(2 * DIL_BLOCK)[None, :]
    steps = DIL_BLOCK + a_idx - c_idx
    in_band = (steps >= 0) & (steps <= DIL_STEPS)
    bucket = _t5_bucket(np.clip(steps, 0, None) * dilation)
    return np.where(in_band, bucket, -1).astype(np.int32)


def _dil_attn(qkv, col_block, rel_bias, group):
    B, d, L, _ = qkv.shape
    n_cls = d
    bucket = _band_buckets(d)
    buckets = tuple(int(k) for k in np.unique(bucket) if k >= 0)

    def part(p):
        return pl.BlockSpec((None, n_cls, L, DIL_W), lambda b, r: (b, r, 0, col_block + p))

    return pl.pallas_call(
        functools.partial(_dil_attn_kernel, n_cls=n_cls, n_blk=L // DIL_BLOCK, buckets=buckets,
                          head0=group * DIL_SLOTS),
        grid=(B, d // n_cls),
        in_specs=[
            part(0), part(1), part(2),
            pl.BlockSpec((DIL_BLOCK, 2 * DIL_BLOCK), lambda b, r: (0, 0)),
            pl.BlockSpec(memory_space=pltpu.SMEM),
        ],
        scratch_shapes=[pltpu.VMEM((DIL_SLOTS, DIL_BLOCK, 2 * DIL_BLOCK), F32)],
        out_specs=[
            pl.BlockSpec((None, n_cls, L, DIL_W), lambda b, r: (b, r, 0, 0)),
            pl.BlockSpec((None, n_cls, L, LANES), lambda b, r: (b, r, 0, 0)),
        ],
        out_shape=[
            jax.ShapeDtypeStruct((B, d, L, DIL_W), BF16),
            jax.ShapeDtypeStruct((B, d, L, LANES), F32),
        ],
        compiler_params=_params(2),
        name=f"dil_attn_d{d}",
    )(qkv, qkv, qkv, jnp.asarray(bucket), rel_bias)


def _t5_bucket(dist):
    max_exact = REL_BUCKETS // 2
    d = np.maximum(dist, 1).astype(np.float64)
    large = max_exact + (np.log(d / max_exact) / np.log(REL_MAX_DIST / max_exact)
                         * (REL_BUCKETS - max_exact)).astype(np.int64)
    large = np.minimum(large, REL_BUCKETS - 1)
    return np.where(dist < max_exact, dist, large).astype(np.int32)


def _merge_kernel(og_ref, o1_ref, o4_ref, o16_ref, l1_ref, l4_ref, l16_ref, ga_ref, gb_ref,
                  wa_ref, wb_ref, bg_ref, o_ref, s4_ref, s16_ref, sl4_ref, sl16_ref):
    T = og_ref.shape[0]
    a = _dot(og_ref[...], wa_ref[...])
    for d, grp_ref, lse_ref, s_ref, sl_ref in ((4, o4_ref, l4_ref, s4_ref, sl4_ref),
                                                (16, o16_ref, l16_ref, s16_ref, sl16_ref)):
        for r in range(d):
            rows = pl.ds(r, T // d, stride=d)
            sl_ref[rows, :] = lse_ref[r]
            for c in range(DIL_SLOTS):
                s_ref[c, rows, :] = grp_ref[r, :, c * LANES:(c + 1) * LANES].astype(F32)
    lses = (l1_ref[...], sl4_ref[...], sl16_ref[...])
    m = jnp.maximum(jnp.maximum(lses[0], lses[1]), lses[2])
    es = [jnp.exp(l - m) for l in lses]
    inv = 1.0 / (es[0] + es[1] + es[2])
    ws = [e * inv for e in es]
    heads = []
    for c in range(DIL_SLOTS):
        cs = slice(c * DIL_HEAD_DIM, (c + 1) * DIL_HEAD_DIM)
        mix = (ws[0][:, c:c + 1] * o1_ref[:, cs].astype(F32)
               + ws[1][:, c:c + 1] * s4_ref[c]
               + ws[2][:, c:c + 1] * s16_ref[c])
        heads.append(mix.astype(BF16))
    bm = _dot(jnp.concatenate(heads, axis=1), wb_ref[...])
    gate_a = jax.nn.sigmoid(ga_ref[...].astype(F32) + bg_ref[:, :D_MODEL])
    gate_b = jax.nn.sigmoid(gb_ref[...].astype(F32) + bg_ref[:, D_MODEL:])
    o_ref[...] = (gate_a * a + gate_b * bm).astype(o_ref.dtype)


def _merge(o_gla, groups, z, gate_col_block, B, S, wa, wb, b_gate, bm):
    nt = S // bm
    (o1, l1), (o4, l4), (o16, l16) = groups
    const = lambda i: (0, 0)
    resident = dict(pipeline_mode=pl.Buffered(1))

    def natural(width):
        return pl.BlockSpec((None, None, bm, width), lambda i: (i // nt, 0, i % nt, 0))

    def class_major(d, width):
        return pl.BlockSpec((None, d, bm // d, width), lambda i: (i // nt, 0, i % nt, 0))

    return pl.pallas_call(
        _merge_kernel,
        grid=(B * nt,),
        in_specs=[
            pl.BlockSpec((bm, GLA_V_W), lambda i: (i, 0)),
            natural(DIL_W), class_major(4, DIL_W), class_major(16, DIL_W),
            natural(LANES), class_major(4, LANES), class_major(16, LANES),
            pl.BlockSpec((bm, D_MODEL), lambda i: (i, gate_col_block)),
            pl.BlockSpec((bm, D_MODEL), lambda i: (i, gate_col_block + 1)),
            pl.BlockSpec((GLA_V_W, D_MODEL), const, **resident),
            pl.BlockSpec((DIL_W, D_MODEL), const, **resident),
            pl.BlockSpec((1, 2 * D_MODEL), const),
        ],
        out_specs=pl.BlockSpec((bm, D_MODEL), lambda i: (i, 0)),
        out_shape=jax.ShapeDtypeStruct((B * S, D_MODEL), BF16),
        scratch_shapes=[
            pltpu.VMEM((DIL_SLOTS, bm, LANES), F32), pltpu.VMEM((DIL_SLOTS, bm, LANES), F32),
            pltpu.VMEM((bm, LANES), F32), pltpu.VMEM((bm, LANES), F32),
        ],
        compiler_params=_params(1),
        name="merge_branches",
    )(o_gla, o1, o4, o16, l1, l4, l16, z, z, wa, wb, b_gate)


def _out_proj_kernel(m_ref, x_ref, w_ref, o_ref):
    for c in range(0, w_ref.shape[1], PROJ_CHUNK):
        cs = slice(c, c + PROJ_CHUNK)
        o_ref[:, cs] = x_ref[:, cs] + _dot(m_ref[...], w_ref[:, cs])


def _out_proj(merged, x2d, w):
    M, D = x2d.shape
    rows = OUT_PROJ_ROWS
    return pl.pallas_call(
        _out_proj_kernel,
        grid=(M // rows,),
        in_specs=[
            pl.BlockSpec((rows, D), lambda i: (i, 0)),
            pl.BlockSpec((rows, D), lambda i: (i, 0)),
            pl.BlockSpec((D, D), lambda i: (0, 0), pipeline_mode=pl.Buffered(1)),
        ],
        out_specs=pl.BlockSpec((rows, D), lambda i: (i, 0)),
        out_shape=jax.ShapeDtypeStruct((M, D), F32),
        compiler_params=_params(1),
        name="out_proj",
    )(merged, x2d, w)


def _ffn_part_kernel(*refs, first, last):
    refs = list(refs)
    x_ref = refs.pop(0)
    h_in_ref = None if first else refs.pop(0)
    nw_ref = refs.pop(0) if first else None
    wgu_ref, wd_ref = refs.pop(0), refs.pop(0)
    fw_ref = refs.pop(0) if last else None
    o_ref = refs.pop(0)
    h_ref = refs.pop(0) if first else h_in_ref
    bf = FFN_HIDDEN_BLOCK

    x = x_ref[...]
    if first:
        ms = jnp.mean(x * x, axis=-1, keepdims=True)
        h_ref[...] = (x * lax.rsqrt(ms + RMS_EPS) * nw_ref[...]).astype(h_ref.dtype)
    o_ref[...] = x
    for f in range(wd_ref.shape[0] // bf):
        gate_up = _dot(h_ref[...], wgu_ref[:, 2 * f * bf:(2 * f + 2) * bf])
        gate, up = gate_up[:, :bf], gate_up[:, bf:]
        act = (gate * jax.nn.sigmoid(gate) * up).astype(BF16)
        for c in range(0, D_MODEL, FFN_OUT_CHUNK):
            o_ref[:, c:c + FFN_OUT_CHUNK] += _dot(act, wd_ref[f * bf:(f + 1) * bf, c:c + FFN_OUT_CHUNK])
    if last:
        y = o_ref[...]
        ms = jnp.mean(y * y, axis=-1, keepdims=True)
        o_ref[...] = y * lax.rsqrt(ms + RMS_EPS) * fw_ref[...]


def _ffn_part(x, h, norm_w, w_in, w_out, final_w, blocks, first, last):
    M = x.shape[0]
    rows, bf = FFN_PART_ROWS, FFN_HIDDEN_BLOCK
    n_hidden = (blocks[1] - blocks[0]) * bf
    row_spec = pl.BlockSpec((rows, D_MODEL), lambda i: (i, 0))
    vec_spec = pl.BlockSpec((1, D_MODEL), lambda i: (0, 0))
    resident = dict(pipeline_mode=pl.Buffered(1))
    wgu_spec = pl.BlockSpec((pl.Element(D_MODEL), pl.Element(2 * n_hidden)),
                            lambda i: (0, 2 * blocks[0] * bf), **resident)
    wd_spec = pl.BlockSpec((pl.Element(n_hidden), pl.Element(D_MODEL)), lambda i: (blocks[0] * bf, 0), **resident)
    args, in_specs = [x], [row_spec]
    if first:
        args.append(norm_w), in_specs.append(vec_spec)
    else:
        args.append(h), in_specs.append(row_spec)
    args += [w_in, w_out]
    in_specs += [wgu_spec, wd_spec]
    if last:
        args.append(final_w), in_specs.append(vec_spec)
    out_shape = [jax.ShapeDtypeStruct((M, D_MODEL), F32)]
    out_specs = [row_spec]
    if first:
        out_shape.append(jax.ShapeDtypeStruct((M, D_MODEL), BF16)), out_specs.append(row_spec)
    return pl.pallas_call(
        functools.partial(_ffn_part_kernel, first=first, last=last),
        grid=(M // rows,),
        in_specs=in_specs,
        out_specs=out_specs,
        out_shape=out_shape,
        compiler_params=_params(1),
        name=f"swiglu_{blocks[0]}_{blocks[1]}",
    )(*args)


def _ffn_weight_moves(bf):
    moves = []
    for f in range(D_FF // bf):
        moves.append((2 * f * bf, f * bf, bf))
        moves.append(((2 * f + 1) * bf, D_FF + f * bf, bf))
    return moves


def kernel(x, attn_norm, w_in, w_gk_up, b_gk, gla_norm, b_gate, w_branch_gla, w_branch_dil,
           w_out, ffn_norm, w_ffn_in, w_ffn_out, rel_bias, final_norm):
    B, S, D = x.shape
    assert D == D_MODEL and S % (16 * DIL_BLOCK) == 0 and attn_norm.shape[0] == 1

    o_lr = 2 * GLA_QK_W + 2 * GLA_V_W
    o_dil = o_lr + GLA_LOWRANK
    o_gate = o_dil + 9 * DIL_W
    n_gla = o_lr
    n_rest = 2 * D_MODEL + 3 * DIL_W
    n_dil = 3 * DIL_W
    lr_col = n_gla
    col_rest = -(-(n_gla + LANES) // n_rest) * n_rest
    col_d4 = -(-(col_rest + n_rest) // n_dil) * n_dil
    col_d16 = col_d4 + n_dil
    w_all = _pack_w_in(
        jnp.swapaxes(w_in, 1, 2)[0],
        segments=((0, 0, n_gla), (lr_col, o_lr, GLA_LOWRANK), (col_rest, o_gate, 2 * D_MODEL),
                  (col_rest + 2 * D_MODEL, o_dil, 3 * DIL_W), (col_d4, o_dil + 3 * DIL_W, 6 * DIL_W)),
        n_out=col_d16 + n_dil)

    z_gla, h, lr = _in_proj_norm(x, attn_norm[0][None, :], w_all, n_gla, lr_col // LANES)
    z_rest = _in_proj(h, w_all, col_rest, n_rest)
    z_d4, w_ffn_in_bf = _in_proj_dil(h, w_all, col_d4, n_dil, DIL_DILATIONS[1], w_ffn_in[0],
                                     _ffn_weight_moves(FFN_HIDDEN_BLOCK))
    z_d16, w_ffn_out_bf = _in_proj_dil(h, w_all, col_d16, n_dil, DIL_DILATIONS[2], w_ffn_out[0])

    wup = jnp.concatenate([w_gk_up[0], jnp.zeros((LANES - GLA_LOWRANK, GLA_QK_W), F32)], axis=0).astype(BF16)
    o_gla, merge_w = _gla(z_gla.reshape(B * S, n_gla), lr.reshape(B * S, LANES), wup, b_gk[0][None, :],
                          gla_norm[0][None, :], B, S, (w_branch_gla[0], w_branch_dil[0], w_out[0]))

    z_d1 = z_rest.reshape(B, 1, S, n_rest)
    groups = [_dil_attn(zg, col_block, rel_bias, gi)
              for gi, (zg, col_block) in enumerate(((z_d1, 2 * D_MODEL // DIL_W), (z_d4, 0), (z_d16, 0)))]
    merged = _merge(o_gla, groups, z_rest.reshape(B * S, n_rest), 0, B, S, merge_w[0], merge_w[1],
                    b_gate[0][None, :], MERGE_ROWS)
    x1 = _out_proj(merged, x.reshape(B * S, D), merge_w[2])
    acc, h_ffn = x1, None
    for i, blocks in enumerate(FFN_PARTS):
        first, last = i == 0, i == len(FFN_PARTS) - 1
        res = _ffn_part(acc, h_ffn, ffn_norm[0][None, :], w_ffn_in_bf, w_ffn_out_bf, final_norm[None, :],
                        blocks, first, last)
        acc, h_ffn = (res[0], res[1]) if first else (res[0], h_ffn)
    return acc.reshape(B, S, D)
```

```python
import functools

import numpy as np
import jax
import jax.numpy as jnp
from jax import lax
from jax.experimental import pallas as pl
from jax.experimental.pallas import tpu as pltpu

F32 = jnp.float32
BF16 = jnp.bfloat16

D_MODEL = 2048
GLA_HEADS = 4
GLA_DK = 256
GLA_DV = 512
GLA_QK_W = GLA_HEADS * GLA_DK
GLA_V_W = GLA_HEADS * GLA_DV
GLA_LOWRANK = 16
GLA_GATE_NORM = 16.0
DIL_DILATIONS = (1, 4, 16)
DIL_SLOTS = 8
DIL_HEAD_DIM = 128
DIL_W = DIL_SLOTS * DIL_HEAD_DIM
DIL_STEPS = 128
DIL_BLOCK = 128
REL_BUCKETS = 32
REL_MAX_DIST = 2048
D_FF = 5632
RMS_EPS = 1e-6
NEG_INF = -1e30
LOG2E = 1.4426950408889634
LN2 = 0.6931471805599453

LANES = 128
SUBLANES = 8
BF16_ROWS = 16
MXU_WIDTH = 256
VMEM_LIMIT = 56 * 1024 * 1024

PACK_COLS = 512
PROJ_NAT_ROWS = 512
PROJ_ROWS = 1024
PROJ_CHUNK = MXU_WIDTH
GLA_BLOCK = 128
GLA_STEP_ROWS = 512
MERGE_ROWS = 512
OUT_PROJ_ROWS = 1024
FFN_PART_ROWS = 512
FFN_HIDDEN_BLOCK = 512
FFN_PARTS = ((0, 4), (4, 8), (8, 11))
FFN_OUT_CHUNK = 512


def _params(n_axes):
    return pltpu.CompilerParams(dimension_semantics=("arbitrary",) * n_axes, vmem_limit_bytes=VMEM_LIMIT)


def _dot(a, b):
    return jnp.dot(a, b, preferred_element_type=F32)


def _dot_t(a, b, dims):
    return lax.dot_general(a, b, (dims, ((), ())), preferred_element_type=F32)


def _split_bf16(x):
    hi = x.astype(BF16)
    lo = (x - hi.astype(F32)).astype(BF16)
    return hi, lo


def _in_proj_norm_kernel(x_ref, g_ref, w_ref, w_lr_ref, o_ref, h_ref, lr_ref):
    x = x_ref[...]
    ms = jnp.mean(x * x, axis=-1, keepdims=True)
    h_ref[...] = (x * lax.rsqrt(ms + RMS_EPS) * g_ref[...]).astype(h_ref.dtype)
    lr_ref[...] = _dot(h_ref[...], w_lr_ref[...]).astype(lr_ref.dtype)
    for c in range(0, w_ref.shape[1], PROJ_CHUNK):
        o_ref[:, c:c + PROJ_CHUNK] = _dot(h_ref[...], w_ref[:, c:c + PROJ_CHUNK]).astype(o_ref.dtype)


def _in_proj_norm(x, norm_w, w, n_cols, lr_col_block):
    B, S, D = x.shape
    rows = PROJ_NAT_ROWS
    resident = dict(pipeline_mode=pl.Buffered(1))
    return pl.pallas_call(
        _in_proj_norm_kernel,
        grid=(B, S // rows),
        in_specs=[
            pl.BlockSpec((None, rows, D), lambda b, t: (b, t, 0)),
            pl.BlockSpec((1, D), lambda b, t: (0, 0)),
            pl.BlockSpec((D, n_cols), lambda b, t: (0, 0), **resident),
            pl.BlockSpec((D, LANES), lambda b, t: (0, lr_col_block), **resident),
        ],
        out_specs=[
            pl.BlockSpec((None, rows, n_cols), lambda b, t: (b, t, 0)),
            pl.BlockSpec((None, rows, D), lambda b, t: (b, t, 0)),
            pl.BlockSpec((None, rows, LANES), lambda b, t: (b, t, 0)),
        ],
        out_shape=[jax.ShapeDtypeStruct((B, S, n_cols), BF16), jax.ShapeDtypeStruct((B, S, D), BF16),
                   jax.ShapeDtypeStruct((B, S, LANES), BF16)],
        compiler_params=_params(2),
        name="in_proj_norm",
    )(x, norm_w, w, w)


def _in_proj_kernel(h_ref, w_ref, o_ref):
    for c in range(0, w_ref.shape[1], PROJ_CHUNK):
        o_ref[:, c:c + PROJ_CHUNK] = _dot(h_ref[...], w_ref[:, c:c + PROJ_CHUNK]).astype(o_ref.dtype)


def _in_proj(h, w, col0, n_cols):
    B, S, D = h.shape
    rows = PROJ_NAT_ROWS
    assert col0 % n_cols == 0
    return pl.pallas_call(
        _in_proj_kernel,
        grid=(B, S // rows),
        in_specs=[
            pl.BlockSpec((None, rows, D), lambda b, t: (b, t, 0)),
            pl.BlockSpec((D, n_cols), lambda b, t: (0, col0 // n_cols), pipeline_mode=pl.Buffered(1)),
        ],
        out_specs=pl.BlockSpec((None, rows, n_cols), lambda b, t: (b, t, 0)),
        out_shape=jax.ShapeDtypeStruct((B, S, n_cols), BF16),
        compiler_params=_params(2),
        name="in_proj_rest",
    )(h, w)


def _in_proj_dil_kernel(h_ref, w_ref, cast_ref, o_ref, cast_out_ref, *scratch, dilation, cast_moves):
    rows = h_ref.shape[0]
    chunk_slabs = PROJ_CHUNK // LANES

    for dst, src, width in cast_moves:
        cast_out_ref[:, dst:dst + width] = cast_ref[:, src:src + width].astype(cast_out_ref.dtype)

    passes, rest = [], dilation
    while rest > 1:
        passes.append(min(rest, 4))
        rest //= passes[-1]

    def deinterleave(res, chunk):
        base = (chunk % 2) * chunk_slabs
        for i in range(chunk_slabs):
            scratch[0][base + i] = res[:, i * LANES:(i + 1) * LANES]
        src_ref, n_groups = scratch[0], 1
        for p, stride in enumerate(passes):
            group_rows = rows // n_groups
            last = p == len(passes) - 1
            for g in range(n_groups):
                for r in range(stride):
                    for i in range(chunk_slabs):
                        part = src_ref[base + i, pl.ds(g * group_rows + r, group_rows // stride, stride=stride), :]
                        if last:
                            col = (chunk * chunk_slabs + i) * LANES
                            o_ref[g + n_groups * r, :, col:col + LANES] = part.astype(o_ref.dtype)
                        else:
                            dst = (g + n_groups * r) * (group_rows // stride)
                            scratch[1][base + i, dst:dst + group_rows // stride, :] = part
            if not last:
                src_ref, n_groups = scratch[1], n_groups * stride

    n_chunks = w_ref.shape[1] // PROJ_CHUNK
    res = _dot(h_ref[...], w_ref[:, :PROJ_CHUNK])
    for i in range(n_chunks):
        nxt = _dot(h_ref[...], w_ref[:, (i + 1) * PROJ_CHUNK:(i + 2) * PROJ_CHUNK]) if i + 1 < n_chunks else None
        deinterleave(res, i)
        res = nxt


def _in_proj_dil(h, w, col0, n_cols, dilation, to_cast, cast_moves=None):
    B, S, D = h.shape
    rows = PROJ_ROWS
    nt = S // rows
    assert col0 % n_cols == 0 and n_cols % PROJ_CHUNK == 0
    cast_rows = to_cast.shape[0] // (B * nt)
    assert cast_rows * B * nt == to_cast.shape[0] and cast_rows % BF16_ROWS == 0
    if cast_moves is None:
        cast_moves = ((0, 0, to_cast.shape[1]),)
    assert sum(m[2] for m in cast_moves) == to_cast.shape[1]
    cast_spec = pl.BlockSpec((cast_rows, to_cast.shape[1]), lambda b, t: (b * nt + t, 0))
    n_slabs = 2 * PROJ_CHUNK // LANES
    scratch = [pltpu.VMEM((n_slabs, rows, LANES), F32)]
    if dilation > 4:
        scratch.append(pltpu.VMEM((n_slabs, rows, LANES), F32))
    return pl.pallas_call(
        functools.partial(_in_proj_dil_kernel, dilation=dilation, cast_moves=tuple(cast_moves)),
        grid=(B, nt),
        in_specs=[
            pl.BlockSpec((None, rows, D), lambda b, t: (b, t, 0)),
            pl.BlockSpec((D, n_cols), lambda b, t: (0, col0 // n_cols), pipeline_mode=pl.Buffered(1)),
            cast_spec,
        ],
        out_specs=[
            pl.BlockSpec((None, dilation, rows // dilation, n_cols), lambda b, t: (b, 0, t, 0)),
            cast_spec,
        ],
        out_shape=[jax.ShapeDtypeStruct((B, dilation, S // dilation, n_cols), BF16),
                   jax.ShapeDtypeStruct(to_cast.shape, BF16)],
        scratch_shapes=scratch,
        compiler_params=_params(2),
        name=f"in_proj_d{dilation}",
    )(h, w, to_cast)


def _pack_w_in_kernel(src_ref, valid_ref, wt_ref, o_ref):
    x = wt_ref[...]
    row = lax.broadcasted_iota(jnp.int32, x.shape, 0)
    x = jnp.where(row < valid_ref[pl.program_id(0)], x, 0.0)
    o_ref[...] = x.T.astype(o_ref.dtype)


def _pack_w_in(wt, segments, n_out):
    N, D = wt.shape
    n_blocks = n_out // PACK_COLS
    src = np.zeros((n_blocks,), np.int32)
    valid = np.zeros((n_blocks,), np.int32)
    for dst, s, width in segments:
        assert dst % PACK_COLS == 0
        for i in range(dst // PACK_COLS, -(-(dst + width) // PACK_COLS)):
            start = s + i * PACK_COLS - dst
            assert start % SUBLANES == 0 and start + PACK_COLS <= N
            src[i] = start // SUBLANES
            valid[i] = min(PACK_COLS, dst + width - i * PACK_COLS)
    return pl.pallas_call(
        _pack_w_in_kernel,
        grid_spec=pltpu.PrefetchScalarGridSpec(
            num_scalar_prefetch=2,
            grid=(n_blocks,),
            in_specs=[pl.BlockSpec((pl.Element(PACK_COLS), pl.Element(D)),
                                   lambda i, src, valid: (src[i] * SUBLANES, 0))],
            out_specs=pl.BlockSpec((D, PACK_COLS), lambda i, src, valid: (0, i)),
        ),
        out_shape=jax.ShapeDtypeStruct((D, n_out), BF16),
        compiler_params=_params(1),
        name="pack_w_in",
    )(jnp.asarray(src), jnp.asarray(valid), wt)


def _gla_kernel(q_ref, k_ref, v_ref, lr_ref, wup_ref, bgk_ref, nw_ref, *rest, n_casts):
    cast_refs, o_ref = rest[:n_casts], rest[n_casts]
    cast_out_refs, s_ref = rest[n_casts + 1:2 * n_casts + 1], rest[2 * n_casts + 1]
    C = GLA_BLOCK
    for src_ref, dst_ref in zip(cast_refs, cast_out_refs):
        dst_ref[...] = src_ref[...].astype(dst_ref.dtype)

    @pl.when(pl.program_id(1) == 0)
    def _():
        s_ref[...] = jnp.zeros_like(s_ref)

    row = lax.broadcasted_iota(jnp.int32, (C, C), 0)
    col = lax.broadcasted_iota(jnp.int32, (C, C), 1)
    causal = row >= col
    tri = jnp.where(causal, 1.0, 0.0).astype(BF16)
    ones = jnp.ones((C, LANES), BF16)
    scale = GLA_DK ** -0.5
    n_blk = q_ref.shape[0] // C
    heads = range(GLA_HEADS)
    ks = [slice(h * GLA_DK, (h + 1) * GLA_DK) for h in heads]
    vs = [slice(h * GLA_DV, (h + 1) * GLA_DV) for h in heads]
    units = [(blk, h) for blk in range(n_blk) for h in heads]
    rows = {u: slice(u[0] * C, (u[0] + 1) * C) for u in units}
    gks, b_cum, tot, att, q_dec, k_dec = {}, {}, {}, {}, {}, {}
    o_intra, q_in, k_end, decay = {}, {}, {}, {}
    for u in units:
        pre = _dot(lr_ref[rows[u], :], wup_ref[:, ks[u[1]]]) + bgk_ref[:, ks[u[1]]]
        log_sig = jnp.minimum(pre, 0.0) - jnp.log(1.0 + jnp.exp(-jnp.abs(pre)))
        gks[u] = _split_bf16(log_sig * (1.0 / GLA_GATE_NORM))
    for u in units:
        gk_hi, gk_lo = gks[u]
        b_cum[u] = _dot(tri, gk_hi) + _dot(tri, gk_lo)
        tot[u] = _dot_t(gk_hi, ones, ((0,), (0,))) + _dot_t(gk_lo, ones, ((0,), (0,)))
    for u in units:
        b = b_cum[u]
        b_mid = b[C // 2 - 1:C // 2, :]
        q_dec[u] = q_ref[rows[u], ks[u[1]]].astype(F32) * jnp.exp(b - b_mid)
        k_dec[u] = k_ref[rows[u], ks[u[1]]].astype(F32) * jnp.exp(b_mid - b)
        att[u] = _dot_t((q_dec[u] * scale).astype(BF16), k_dec[u].astype(BF16), ((1,), (1,)))
    for u in units:
        b = b_cum[u]
        b_last = b[C - 1:C, :]
        o_intra[u] = _dot(jnp.where(causal, att[u], 0.0).astype(BF16), v_ref[rows[u], vs[u[1]]])
        q_in[u] = (q_ref[rows[u], ks[u[1]]].astype(F32) * (jnp.exp(b) * scale)).astype(BF16)
        k_end[u] = (k_ref[rows[u], ks[u[1]]].astype(F32) * jnp.exp(b_last - b)).astype(BF16)
        decay[u] = jnp.concatenate([jnp.exp(tot[u])] * (GLA_DV // LANES), axis=1)
    for blk in range(n_blk):
        rs = slice(blk * C, (blk + 1) * C)
        os = []
        for h in heads:
            s_old = s_ref[h]
            os.append(o_intra[blk, h] + _dot(q_in[blk, h], s_old.astype(BF16)))
            s_ref[h] = decay[blk, h] * s_old + _dot_t(k_end[blk, h], v_ref[rs, vs[h]], ((0,), (0,)))
        ys = []
        for h, o in zip(heads, os):
            ms = jnp.mean(o * o, axis=-1, keepdims=True)
            ys.append((o * lax.rsqrt(ms + RMS_EPS) * nw_ref[...]).astype(o_ref.dtype))
        o_ref[rs, :] = jnp.concatenate(ys, axis=1)


def _gla(z, lr, wup, bgk, norm_w, B, S, to_cast):
    C = GLA_STEP_ROWS
    nc = S // C
    cast_specs = []
    for w in to_cast:
        rows = w.shape[0] // (B * nc)
        assert rows * B * nc == w.shape[0] and rows % BF16_ROWS == 0
        cast_specs.append(pl.BlockSpec((rows, w.shape[1]), lambda b, c: (b * nc + c, 0)))
    res = pl.pallas_call(
        functools.partial(_gla_kernel, n_casts=len(to_cast)),
        grid=(B, nc),
        in_specs=[
            pl.BlockSpec((C, GLA_QK_W), lambda b, c: (b * nc + c, 0)),
            pl.BlockSpec((C, GLA_QK_W), lambda b, c: (b * nc + c, 1)),
            pl.BlockSpec((C, GLA_V_W), lambda b, c: (b * nc + c, 1)),
            pl.BlockSpec((C, LANES), lambda b, c: (b * nc + c, 0)),
            pl.BlockSpec((LANES, GLA_QK_W), lambda b, c: (0, 0)),
            pl.BlockSpec((1, GLA_QK_W), lambda b, c: (0, 0)),
            pl.BlockSpec((1, GLA_DV), lambda b, c: (0, 0)),
        ] + cast_specs,
        out_specs=[pl.BlockSpec((C, GLA_V_W), lambda b, c: (b * nc + c, 0))] + cast_specs,
        out_shape=[jax.ShapeDtypeStruct((B * S, GLA_V_W), BF16)]
        + [jax.ShapeDtypeStruct(w.shape, BF16) for w in to_cast],
        scratch_shapes=[pltpu.VMEM((GLA_HEADS, GLA_DK, GLA_DV), F32)],
        compiler_params=_params(2),
        name="gla",
    )(z, z, z, lr, wup, bgk, norm_w, *to_cast)
    return res[0], res[1:]


def _dil_attn_kernel(q_ref, k_ref, v_ref, bucket_ref, table_ref, o_ref, lse_ref, bias_ref, *,
                     n_cls, n_blk, buckets, head0):
    scale = DIL_HEAD_DIM ** -0.5
    lane = lax.broadcasted_iota(jnp.int32, (DIL_BLOCK, LANES), 1)

    @pl.when((pl.program_id(0) == 0) & (pl.program_id(1) == 0))
    def _():
        bucket = bucket_ref[...]
        for h in range(DIL_SLOTS):
            bias = jnp.full(bucket.shape, NEG_INF, F32)
            for k in buckets:
                bias = jnp.where(bucket == k, table_ref[k, head0 + h] * LOG2E, bias)
            bias_ref[h] = bias

    def blocks(where, first):
        n_keys = DIL_BLOCK if first else 2 * DIL_BLOCK
        ones = jnp.ones((n_keys, DIL_HEAD_DIM), BF16)
        cols = [slice(h * DIL_HEAD_DIM, (h + 1) * DIL_HEAD_DIM) for h in range(DIL_SLOTS)]
        rows_q = [pl.ds(start, DIL_BLOCK) for _, start in where]
        rows_k = [rq if first else pl.ds(start - DIL_BLOCK, 2 * DIL_BLOCK) for rq, (_, start) in zip(rows_q, where)]
        units = [(u, h) for u in range(len(where)) for h in range(DIL_SLOTS)]
        scores = [
            _dot_t(q_ref[where[u][0], rows_q[u], cols[h]], k_ref[where[u][0], rows_k[u], cols[h]], ((1,), (1,)))
            * (scale * LOG2E) + (bias_ref[h, :, DIL_BLOCK:] if first else bias_ref[h])
            for u, h in units]
        maxes = [jnp.max(s, axis=-1, keepdims=True) for s in scores]
        probs = [jnp.exp2(s - m).astype(BF16) for s, m in zip(scores, maxes)]
        num_ls = [_dot(p, jnp.concatenate([v_ref[where[u][0], rows_k[u], cols[h]], ones], axis=1))
                  for p, (u, h) in zip(probs, units)]
        for u, (c, _) in enumerate(where):
            outs = []
            lse_tile = jnp.zeros((DIL_BLOCK, LANES), F32)
            for h in range(DIL_SLOTS):
                num_l, m = num_ls[u * DIL_SLOTS + h], maxes[u * DIL_SLOTS + h]
                l = num_l[:, DIL_HEAD_DIM:]
                outs.append((num_l[:, :DIL_HEAD_DIM] / l).astype(o_ref.dtype))
                lse_tile = jnp.where(lane == h, m * LN2 + jnp.log(l), lse_tile)
            o_ref[c, rows_q[u], :] = jnp.concatenate(outs, axis=1)
            lse_ref[c, rows_q[u], :] = lse_tile

    def row(n):
        return pl.multiple_of(n * DIL_BLOCK, DIL_BLOCK)

    if n_blk == 1:
        def pair_of_classes(i, carry):
            blocks([(2 * i, 0), (2 * i + 1, 0)], True)
            return carry
        lax.fori_loop(0, n_cls // 2, pair_of_classes, 0)
    else:
        def one_class(c, carry):
            blocks([(c, 0)], True)

            def pair_of_blocks(i, inner):
                blocks([(c, row(2 * i + 1)), (c, row(2 * i + 2))], False)
                return inner
            lax.fori_loop(0, (n_blk - 1) // 2, pair_of_blocks, 0)
            if (n_blk - 1) % 2:
                blocks([(c, (n_blk - 1) * DIL_BLOCK)], False)
            return carry

        if n_cls == 1:
            one_class(0, 0)
        else:
            lax.fori_loop(0, n_cls, one_class, 0)


def _band_buckets(dilation):
    a_idx = np.arange(DIL_BLOCK)[:, None]
    c_idx = np.arange(2 * DIL_BLOCK)[None, :]
    steps = DIL_BLOCK + a_idx - c_idx
    in_band = (steps >= 0) & (steps <= DIL_STEPS)
    bucket = _t5_bucket(np.clip(steps, 0, None) * dilation)
    return np.where(in_band, bucket, -1).astype(np.int32)


def _dil_attn(qkv, col_block, rel_bias, group):
    B, d, L, _ = qkv.shape
    n_cls = d
    bucket = _band_buckets(d)
    buckets = tuple(int(k) for k in np.unique(bucket) if k >= 0)

    def part(p):
        return pl.BlockSpec((None, n_cls, L, DIL_W), lambda b, r: (b, r, 0, col_block + p))

    return pl.pallas_call(
        functools.partial(_dil_attn_kernel, n_cls=n_cls, n_blk=L // DIL_BLOCK, buckets=buckets,
                          head0=group * DIL_SLOTS),
        grid=(B, d // n_cls),
        in_specs=[
            part(0), part(1), part(2),
            pl.BlockSpec((DIL_BLOCK, 2 * DIL_BLOCK), lambda b, r: (0, 0)),
            pl.BlockSpec(memory_space=pltpu.SMEM),
        ],
        scratch_shapes=[pltpu.VMEM((DIL_SLOTS, DIL_BLOCK, 2 * DIL_BLOCK), F32)],
        out_specs=[
            pl.BlockSpec((None, n_cls, L, DIL_W), lambda b, r: (b, r, 0, 0)),
            pl.BlockSpec((None, n_cls, L, LANES), lambda b, r: (b, r, 0, 0)),
        ],
        out_shape=[
            jax.ShapeDtypeStruct((B, d, L, DIL_W), BF16),
            jax.ShapeDtypeStruct((B, d, L, LANES), F32),
        ],
        compiler_params=_params(2),
        name=f"dil_attn_d{d}",
    )(qkv, qkv, qkv, jnp.asarray(bucket), rel_bias)


def _t5_bucket(dist):
    max_exact = REL_BUCKETS // 2
    d = np.maximum(dist, 1).astype(np.float64)
    large = max_exact + (np.log(d / max_exact) / np.log(REL_MAX_DIST / max_exact)
                         * (REL_BUCKETS - max_exact)).astype(np.int64)
    large = np.minimum(large, REL_BUCKETS - 1)
    return np.where(dist < max_exact, dist, large).astype(np.int32)


def _merge_kernel(og_ref, g_ref, o1_ref, o4_ref, o16_ref, l1_ref, l4_ref, l16_ref, ga_ref, gb_ref,
                  wa_ref, wb_ref, bg_ref, o_ref, s4_ref, s16_ref, sl4_ref, sl16_ref):
    T = og_ref.shape[0]
    g = g_ref[...].astype(F32)
    a = _dot((og_ref[...].astype(F32) * (g * jax.nn.sigmoid(g))).astype(BF16), wa_ref[...])
    for d, grp_ref, lse_ref, s_ref, sl_ref in ((4, o4_ref, l4_ref, s4_ref, sl4_ref),
                                                (16, o16_ref, l16_ref, s16_ref, sl16_ref)):
        for r in range(d):
            rows = pl.ds(r, T // d, stride=d)
            sl_ref[rows, :] = lse_ref[r]
            for c in range(DIL_SLOTS):
                s_ref[c, rows, :] = grp_ref[r, :, c * LANES:(c + 1) * LANES].astype(F32)
    lses = (l1_ref[...], sl4_ref[...], sl16_ref[...])
    m = jnp.maximum(jnp.maximum(lses[0], lses[1]), lses[2])
    es = [jnp.exp(l - m) for l in lses]
    inv = 1.0 / (es[0] + es[1] + es[2])
    ws = [e * inv for e in es]
    heads = []
    for c in range(DIL_SLOTS):
        cs = slice(c * DIL_HEAD_DIM, (c + 1) * DIL_HEAD_DIM)
        mix = (ws[0][:, c:c + 1] * o1_ref[:, cs].astype(F32)
               + ws[1][:, c:c + 1] * s4_ref[c]
               + ws[2][:, c:c + 1] * s16_ref[c])
        heads.append(mix.astype(BF16))
    bm = _dot(jnp.concatenate(heads, axis=1), wb_ref[...])
    gate_a = jax.nn.sigmoid(ga_ref[...].astype(F32) + bg_ref[:, :D_MODEL])
    gate_b = jax.nn.sigmoid(gb_ref[...].astype(F32) + bg_ref[:, D_MODEL:])
    o_ref[...] = (gate_a * a + gate_b * bm).astype(o_ref.dtype)


def _merge(o_gla, z_gla, g_col_block, groups, z, gate_col_block, B, S, wa, wb, b_gate, bm):
    nt = S // bm
    (o1, l1), (o4, l4), (o16, l16) = groups
    const = lambda i: (0, 0)
    resident = dict(pipeline_mode=pl.Buffered(1))

    def natural(width):
        return pl.BlockSpec((None, None, bm, width), lambda i: (i // nt, 0, i % nt, 0))

    def class_major(d, width):
        return pl.BlockSpec((None, d, bm // d, width), lambda i: (i // nt, 0, i % nt, 0))

    return pl.pallas_call(
        _merge_kernel,
        grid=(B * nt,),
        in_specs=[
            pl.BlockSpec((bm, GLA_V_W), lambda i: (i, 0)),
            pl.BlockSpec((bm, GLA_V_W), lambda i: (i, g_col_block)),
            natural(DIL_W), class_major(4, DIL_W), class_major(16, DIL_W),
            natural(LANES), class_major(4, LANES), class_major(16, LANES),
            pl.BlockSpec((bm, D_MODEL), lambda i: (i, gate_col_block)),
            pl.BlockSpec((bm, D_MODEL), lambda i: (i, gate_col_block + 1)),
            pl.BlockSpec((GLA_V_W, D_MODEL), const, **resident),
            pl.BlockSpec((DIL_W, D_MODEL), const, **resident),
            pl.BlockSpec((1, 2 * D_MODEL), const),
        ],
        out_specs=pl.BlockSpec((bm, D_MODEL), lambda i: (i, 0)),
        out_shape=jax.ShapeDtypeStruct((B * S, D_MODEL), BF16),
        scratch_shapes=[
            pltpu.VMEM((DIL_SLOTS, bm, LANES), F32), pltpu.VMEM((DIL_SLOTS, bm, LANES), F32),
            pltpu.VMEM((bm, LANES), F32), pltpu.VMEM((bm, LANES), F32),
        ],
        compiler_params=_params(1),
        name="merge_branches",
    )(o_gla, z_gla, o1, o4, o16, l1, l4, l16, z, z, wa, wb, b_gate)


def _out_proj_kernel(m_ref, x_ref, w_ref, o_ref):
    for c in range(0, w_ref.shape[1], PROJ_CHUNK):
        cs = slice(c, c + PROJ_CHUNK)
        o_ref[:, cs] = x_ref[:, cs] + _dot(m_ref[...], w_ref[:, cs])


def _out_proj(merged, x2d, w):
    M, D = x2d.shape
    rows = OUT_PROJ_ROWS
    return pl.pallas_call(
        _out_proj_kernel,
        grid=(M // rows,),
        in_specs=[
            pl.BlockSpec((rows, D), lambda i: (i, 0)),
            pl.BlockSpec((rows, D), lambda i: (i, 0)),
            pl.BlockSpec((D, D), lambda i: (0, 0), pipeline_mode=pl.Buffered(1)),
        ],
        out_specs=pl.BlockSpec((rows, D), lambda i: (i, 0)),
        out_shape=jax.ShapeDtypeStruct((M, D), F32),
        compiler_params=_params(1),
        name="out_proj",
    )(merged, x2d, w)


def _ffn_part_kernel(*refs, first, last):
    refs = list(refs)
    x_ref = refs.pop(0)
    h_in_ref = None if first else refs.pop(0)
    nw_ref = refs.pop(0) if first else None
    wgu_ref, wd_ref = refs.pop(0), refs.pop(0)
    fw_ref = refs.pop(0) if last else None
    o_ref = refs.pop(0)
    h_ref = refs.pop(0) if first else h_in_ref
    bf = FFN_HIDDEN_BLOCK

    x = x_ref[...]
    if first:
        ms = jnp.mean(x * x, axis=-1, keepdims=True)
        h_ref[...] = (x * lax.rsqrt(ms + RMS_EPS) * nw_ref[...]).astype(h_ref.dtype)
    o_ref[...] = x
    for f in range(wd_ref.shape[0] // bf):
        gate_up = _dot(h_ref[...], wgu_ref[:, 2 * f * bf:(2 * f + 2) * bf])
        gate, up = gate_up[:, :bf], gate_up[:, bf:]
        act = (gate * jax.nn.sigmoid(gate) * up).astype(BF16)
        for c in range(0, D_MODEL, FFN_OUT_CHUNK):
            o_ref[:, c:c + FFN_OUT_CHUNK] += _dot(act, wd_ref[f * bf:(f + 1) * bf, c:c + FFN_OUT_CHUNK])
    if last:
        y = o_ref[...]
        ms = jnp.mean(y * y, axis=-1, keepdims=True)
        o_ref[...] = y * lax.rsqrt(ms + RMS_EPS) * fw_ref[...]


def _ffn_part(x, h, norm_w, w_in, w_out, final_w, blocks, first, last):
    M = x.shape[0]
    rows, bf = FFN_PART_ROWS, FFN_HIDDEN_BLOCK
    n_hidden = (blocks[1] - blocks[0]) * bf
    row_spec = pl.BlockSpec((rows, D_MODEL), lambda i: (i, 0))
    vec_spec = pl.BlockSpec((1, D_MODEL), lambda i: (0, 0))
    resident = dict(pipeline_mode=pl.Buffered(1))
    wgu_spec = pl.BlockSpec((pl.Element(D_MODEL), pl.Element(2 * n_hidden)),
                            lambda i: (0, 2 * blocks[0] * bf), **resident)
    wd_spec = pl.BlockSpec((pl.Element(n_hidden), pl.Element(D_MODEL)), lambda i: (blocks[0] * bf, 0), **resident)
    args, in_specs = [x], [row_spec]
    if first:
        args.append(norm_w), in_specs.append(vec_spec)
    else:
        args.append(h), in_specs.append(row_spec)
    args += [w_in, w_out]
    in_specs += [wgu_spec, wd_spec]
    if last:
        args.append(final_w), in_specs.append(vec_spec)
    out_shape = [jax.ShapeDtypeStruct((M, D_MODEL), F32)]
    out_specs = [row_spec]
    if first:
        out_shape.append(jax.ShapeDtypeStruct((M, D_MODEL), BF16)), out_specs.append(row_spec)
    return pl.pallas_call(
        functools.partial(_ffn_part_kernel, first=first, last=last),
        grid=(M // rows,),
        in_specs=in_specs,
        out_specs=out_specs,
        out_shape=out_shape,
        compiler_params=_params(1),
        name=f"swiglu_{blocks[0]}_{blocks[1]}",
    )(*args)


def _ffn_weight_moves(bf):
    moves = []
    for f in range(D_FF // bf):
        moves.append((2 * f * bf, f * bf, bf))
        moves.append(((2 * f + 1) * bf, D_FF + f * bf, bf))
    return moves


def kernel(x, attn_norm, w_in, w_gk_up, b_gk, gla_norm, b_gate, w_branch_gla, w_branch_dil,
           w_out, ffn_norm, w_ffn_in, w_ffn_out, rel_bias, final_norm):
    B, S, D = x.shape
    assert D == D_MODEL and S % (16 * DIL_BLOCK) == 0 and attn_norm.shape[0] == 1

    o_lr = 2 * GLA_QK_W + 2 * GLA_V_W
    o_dil = o_lr + GLA_LOWRANK
    o_gate = o_dil + 9 * DIL_W
    n_gla = o_lr
    n_rest = 2 * D_MODEL + 3 * DIL_W
    n_dil = 3 * DIL_W
    lr_col = n_gla
    col_rest = -(-(n_gla + LANES) // n_rest) * n_rest
    col_d4 = -(-(col_rest + n_rest) // n_dil) * n_dil
    col_d16 = col_d4 + n_dil
    w_all = _pack_w_in(
        jnp.swapaxes(w_in, 1, 2)[0],
        segments=((0, 0, n_gla), (lr_col, o_lr, GLA_LOWRANK), (col_rest, o_gate, 2 * D_MODEL),
                  (col_rest + 2 * D_MODEL, o_dil, 3 * DIL_W), (col_d4, o_dil + 3 * DIL_W, 6 * DIL_W)),
        n_out=col_d16 + n_dil)

    z_gla, h, lr = _in_proj_norm(x, attn_norm[0][None, :], w_all, n_gla, lr_col // LANES)
    z_rest = _in_proj(h, w_all, col_rest, n_rest)
    z_d4, w_ffn_in_bf = _in_proj_dil(h, w_all, col_d4, n_dil, DIL_DILATIONS[1], w_ffn_in[0],
                                     _ffn_weight_moves(FFN_HIDDEN_BLOCK))
    z_d16, w_ffn_out_bf = _in_proj_dil(h, w_all, col_d16, n_dil, DIL_DILATIONS[2], w_ffn_out[0])

    wup = jnp.concatenate([w_gk_up[0], jnp.zeros((LANES - GLA_LOWRANK, GLA_QK_W), F32)], axis=0).astype(BF16)
    o_gla, merge_w = _gla(z_gla.reshape(B * S, n_gla), lr.reshape(B * S, LANES), wup, b_gk[0][None, :],
                          gla_norm[0][None, :], B, S, (w_branch_gla[0], w_branch_dil[0], w_out[0]))

    z_d1 = z_rest.reshape(B, 1, S, n_rest)
    groups = [_dil_attn(zg, col_block, rel_bias, gi)
              for gi, (zg, col_block) in enumerate(((z_d1, 2 * D_MODEL // DIL_W), (z_d4, 0), (z_d16, 0)))]
    merged = _merge(o_gla, z_gla.reshape(B * S, n_gla), (2 * GLA_QK_W + GLA_V_W) // GLA_V_W, groups,
                    z_rest.reshape(B * S, n_rest), 0, B, S, merge_w[0], merge_w[1], b_gate[0][None, :], MERGE_ROWS)
    x1 = _out_proj(merged, x.reshape(B * S, D), merge_w[2])
    acc, h_ffn = x1, None
    for i, blocks in enumerate(FFN_PARTS):
        first, last = i == 0, i == len(FFN_PARTS) - 1
        res = _ffn_part(acc, h_ffn, ffn_norm[0][None, :], w_ffn_in_bf, w_ffn_out_bf, final_norm[None, :],
                        blocks, first, last)
        acc, h_ffn = (res[0], res[1]) if first else (res[0], h_ffn)
    return acc.reshape(B, S, D)
```

```python
import functools

import numpy as np
import jax
import jax.numpy as jnp
from jax import lax
from jax.experimental import pallas as pl
from jax.experimental.pallas import tpu as pltpu

F32 = jnp.float32
BF16 = jnp.bfloat16

D_MODEL = 2048
GLA_HEADS = 4
GLA_DK = 256
GLA_DV = 512
GLA_QK_W = GLA_HEADS * GLA_DK
GLA_V_W = GLA_HEADS * GLA_DV
GLA_LOWRANK = 16
GLA_GATE_NORM = 16.0
DIL_DILATIONS = (1, 4, 16)
DIL_SLOTS = 8
DIL_HEAD_DIM = 128
DIL_W = DIL_SLOTS * DIL_HEAD_DIM
DIL_STEPS = 128
DIL_BLOCK = 128
REL_BUCKETS = 32
REL_MAX_DIST = 2048
D_FF = 5632
RMS_EPS = 1e-6
NEG_INF = -1e30
LOG2E = 1.4426950408889634
LN2 = 0.6931471805599453

LANES = 128
SUBLANES = 8
BF16_ROWS = 16
MXU_WIDTH = 256
VMEM_LIMIT = 56 * 1024 * 1024

PACK_COLS = 512
PROJ_NAT_ROWS = 512
PROJ_ROWS = 1024
PROJ_CHUNK = MXU_WIDTH
GLA_BLOCK = 128
GLA_STEP_ROWS = 1024
MERGE_ROWS = 512
OUT_PROJ_ROWS = 1024
FFN_PART_ROWS = 512
FFN_HIDDEN_BLOCK = 512
FFN_PARTS = ((0, 4), (4, 8), (8, 11))
FFN_OUT_CHUNK = 512


def _params(n_axes):
    return pltpu.CompilerParams(dimension_semantics=("arbitrary",) * n_axes, vmem_limit_bytes=VMEM_LIMIT)


def _dot(a, b):
    return jnp.dot(a, b, preferred_element_type=F32)


def _dot_t(a, b, dims):
    return lax.dot_general(a, b, (dims, ((), ())), preferred_element_type=F32)


def _split_bf16(x):
    hi = x.astype(BF16)
    lo = (x - hi.astype(F32)).astype(BF16)
    return hi, lo


def _in_proj_norm_kernel(x_ref, g_ref, w_ref, w_lr_ref, o_ref, h_ref, lr_ref):
    x = x_ref[...]
    ms = jnp.mean(x * x, axis=-1, keepdims=True)
    h_ref[...] = (x * lax.rsqrt(ms + RMS_EPS) * g_ref[...]).astype(h_ref.dtype)
    lr_ref[...] = _dot(h_ref[...], w_lr_ref[...]).astype(lr_ref.dtype)
    for c in range(0, w_ref.shape[1], PROJ_CHUNK):
        o_ref[:, c:c + PROJ_CHUNK] = _dot(h_ref[...], w_ref[:, c:c + PROJ_CHUNK]).astype(o_ref.dtype)


def _in_proj_norm(x, norm_w, w, n_cols, lr_col_block):
    B, S, D = x.shape
    rows = PROJ_NAT_ROWS
    resident = dict(pipeline_mode=pl.Buffered(1))
    return pl.pallas_call(
        _in_proj_norm_kernel,
        grid=(B, S // rows),
        in_specs=[
            pl.BlockSpec((None, rows, D), lambda b, t: (b, t, 0)),
            pl.BlockSpec((1, D), lambda b, t: (0, 0)),
            pl.BlockSpec((D, n_cols), lambda b, t: (0, 0), **resident),
            pl.BlockSpec((D, LANES), lambda b, t: (0, lr_col_block), **resident),
        ],
        out_specs=[
            pl.BlockSpec((None, rows, n_cols), lambda b, t: (b, t, 0)),
            pl.BlockSpec((None, rows, D), lambda b, t: (b, t, 0)),
            pl.BlockSpec((None, rows, LANES), lambda b, t: (b, t, 0)),
        ],
        out_shape=[jax.ShapeDtypeStruct((B, S, n_cols), BF16), jax.ShapeDtypeStruct((B, S, D), BF16),
                   jax.ShapeDtypeStruct((B, S, LANES), BF16)],
        compiler_params=_params(2),
        name="in_proj_norm",
    )(x, norm_w, w, w)


def _in_proj_kernel(h_ref, w_ref, o_ref):
    for c in range(0, w_ref.shape[1], PROJ_CHUNK):
        o_ref[:, c:c + PROJ_CHUNK] = _dot(h_ref[...], w_ref[:, c:c + PROJ_CHUNK]).astype(o_ref.dtype)


def _in_proj(h, w, col0, n_cols):
    B, S, D = h.shape
    rows = PROJ_NAT_ROWS
    assert col0 % n_cols == 0
    return pl.pallas_call(
        _in_proj_kernel,
        grid=(B, S // rows),
        in_specs=[
            pl.BlockSpec((None, rows, D), lambda b, t: (b, t, 0)),
            pl.BlockSpec((D, n_cols), lambda b, t: (0, col0 // n_cols), pipeline_mode=pl.Buffered(1)),
        ],
        out_specs=pl.BlockSpec((None, rows, n_cols), lambda b, t: (b, t, 0)),
        out_shape=jax.ShapeDtypeStruct((B, S, n_cols), BF16),
        compiler_params=_params(2),
        name="in_proj_rest",
    )(h, w)


def _in_proj_dil_kernel(h_ref, w_ref, cast_ref, o_ref, cast_out_ref, *scratch, dilation, cast_moves):
    rows = h_ref.shape[0]
    chunk_slabs = PROJ_CHUNK // LANES

    for dst, src, width in cast_moves:
        cast_out_ref[:, dst:dst + width] = cast_ref[:, src:src + width].astype(cast_out_ref.dtype)

    passes, rest = [], dilation
    while rest > 1:
        passes.append(min(rest, 4))
        rest //= passes[-1]

    def deinterleave(res, chunk):
        base = (chunk % 2) * chunk_slabs
        for i in range(chunk_slabs):
            scratch[0][base + i] = res[:, i * LANES:(i + 1) * LANES]
        src_ref, n_groups = scratch[0], 1
        for p, stride in enumerate(passes):
            group_rows = rows // n_groups
            last = p == len(passes) - 1
            for g in range(n_groups):
                for r in range(stride):
                    for i in range(chunk_slabs):
                        part = src_ref[base + i, pl.ds(g * group_rows + r, group_rows // stride, stride=stride), :]
                        if last:
                            col = (chunk * chunk_slabs + i) * LANES
                            o_ref[g + n_groups * r, :, col:col + LANES] = part.astype(o_ref.dtype)
                        else:
                            dst = (g + n_groups * r) * (group_rows // stride)
                            scratch[1][base + i, dst:dst + group_rows // stride, :] = part
            if not last:
                src_ref, n_groups = scratch[1], n_groups * stride

    n_chunks = w_ref.shape[1] // PROJ_CHUNK
    res = _dot(h_ref[...], w_ref[:, :PROJ_CHUNK])
    for i in range(n_chunks):
        nxt = _dot(h_ref[...], w_ref[:, (i + 1) * PROJ_CHUNK:(i + 2) * PROJ_CHUNK]) if i + 1 < n_chunks else None
        deinterleave(res, i)
        res = nxt


def _in_proj_dil(h, w, col0, n_cols, dilation, to_cast, cast_moves=None):
    B, S, D = h.shape
    rows = PROJ_ROWS
    nt = S // rows
    assert col0 % n_cols == 0 and n_cols % PROJ_CHUNK == 0
    cast_rows = to_cast.shape[0] // (B * nt)
    assert cast_rows * B * nt == to_cast.shape[0] and cast_rows % BF16_ROWS == 0
    if cast_moves is None:
        cast_moves = ((0, 0, to_cast.shape[1]),)
    assert sum(m[2] for m in cast_moves) == to_cast.shape[1]
    cast_spec = pl.BlockSpec((cast_rows, to_cast.shape[1]), lambda b, t: (b * nt + t, 0))
    n_slabs = 2 * PROJ_CHUNK // LANES
    scratch = [pltpu.VMEM((n_slabs, rows, LANES), F32)]
    if dilation > 4:
        scratch.append(pltpu.VMEM((n_slabs, rows, LANES), F32))
    return pl.pallas_call(
        functools.partial(_in_proj_dil_kernel, dilation=dilation, cast_moves=tuple(cast_moves)),
        grid=(B, nt),
        in_specs=[
            pl.BlockSpec((None, rows, D), lambda b, t: (b, t, 0)),
            pl.BlockSpec((D, n_cols), lambda b, t: (0, col0 // n_cols), pipeline_mode=pl.Buffered(1)),
            cast_spec,
        ],
        out_specs=[
            pl.BlockSpec((None, dilation, rows // dilation, n_cols), lambda b, t: (b, 0, t, 0)),
            cast_spec,
        ],
        out_shape=[jax.ShapeDtypeStruct((B, dilation, S // dilation, n_cols), BF16),
                   jax.ShapeDtypeStruct(to_cast.shape, BF16)],
        scratch_shapes=scratch,
        compiler_params=_params(2),
        name=f"in_proj_d{dilation}",
    )(h, w, to_cast)


def _pack_w_in_kernel(src_ref, valid_ref, wt_ref, o_ref):
    x = wt_ref[...]
    row = lax.broadcasted_iota(jnp.int32, x.shape, 0)
    x = jnp.where(row < valid_ref[pl.program_id(0)], x, 0.0)
    o_ref[...] = x.T.astype(o_ref.dtype)


def _pack_w_in(wt, segments, n_out):
    N, D = wt.shape
    n_blocks = n_out // PACK_COLS
    src = np.zeros((n_blocks,), np.int32)
    valid = np.zeros((n_blocks,), np.int32)
    for dst, s, width in segments:
        assert dst % PACK_COLS == 0
        for i in range(dst // PACK_COLS, -(-(dst + width) // PACK_COLS)):
            start = s + i * PACK_COLS - dst
            assert start % SUBLANES == 0 and start + PACK_COLS <= N
            src[i] = start // SUBLANES
            valid[i] = min(PACK_COLS, dst + width - i * PACK_COLS)
    return pl.pallas_call(
        _pack_w_in_kernel,
        grid_spec=pltpu.PrefetchScalarGridSpec(
            num_scalar_prefetch=2,
            grid=(n_blocks,),
            in_specs=[pl.BlockSpec((pl.Element(PACK_COLS), pl.Element(D)),
                                   lambda i, src, valid: (src[i] * SUBLANES, 0))],
            out_specs=pl.BlockSpec((D, PACK_COLS), lambda i, src, valid: (0, i)),
        ),
        out_shape=jax.ShapeDtypeStruct((D, n_out), BF16),
        compiler_params=_params(1),
        name="pack_w_in",
    )(jnp.asarray(src), jnp.asarray(valid), wt)


def _gla_kernel(q_ref, k_ref, v_ref, lr_ref, wup_ref, bgk_ref, nw_ref, *rest, n_casts):
    cast_refs, o_ref = rest[:n_casts], rest[n_casts]
    cast_out_refs, s_ref = rest[n_casts + 1:2 * n_casts + 1], rest[2 * n_casts + 1]
    C = GLA_BLOCK
    for src_ref, dst_ref in zip(cast_refs, cast_out_refs):
        dst_ref[...] = src_ref[...].astype(dst_ref.dtype)

    @pl.when(pl.program_id(1) == 0)
    def _():
        s_ref[...] = jnp.zeros_like(s_ref)

    row = lax.broadcasted_iota(jnp.int32, (C, C), 0)
    col = lax.broadcasted_iota(jnp.int32, (C, C), 1)
    causal = row >= col
    tri = jnp.where(causal, 1.0, 0.0).astype(BF16)
    ones = jnp.ones((C, LANES), BF16)
    scale = GLA_DK ** -0.5
    n_blk = q_ref.shape[0] // C
    heads = range(GLA_HEADS)
    ks = [slice(h * GLA_DK, (h + 1) * GLA_DK) for h in heads]
    vs = [slice(h * GLA_DV, (h + 1) * GLA_DV) for h in heads]
    units = [(blk, h) for blk in range(n_blk) for h in heads]
    rows = {u: slice(u[0] * C, (u[0] + 1) * C) for u in units}
    gks, b_cum, tot, att, q_dec, k_dec = {}, {}, {}, {}, {}, {}
    o_intra, q_in, k_end, decay = {}, {}, {}, {}
    for u in units:
        pre = _dot(lr_ref[rows[u], :], wup_ref[:, ks[u[1]]]) + bgk_ref[:, ks[u[1]]]
        log_sig = jnp.minimum(pre, 0.0) - jnp.log(1.0 + jnp.exp(-jnp.abs(pre)))
        gks[u] = _split_bf16(log_sig * (1.0 / GLA_GATE_NORM))
    for u in units:
        gk_hi, gk_lo = gks[u]
        b_cum[u] = _dot(tri, gk_hi) + _dot(tri, gk_lo)
        tot[u] = _dot_t(gk_hi, ones, ((0,), (0,))) + _dot_t(gk_lo, ones, ((0,), (0,)))
    for u in units:
        b = b_cum[u]
        b_mid = b[C // 2 - 1:C // 2, :]
        q_dec[u] = q_ref[rows[u], ks[u[1]]].astype(F32) * jnp.exp(b - b_mid)
        k_dec[u] = k_ref[rows[u], ks[u[1]]].astype(F32) * jnp.exp(b_mid - b)
        att[u] = _dot_t((q_dec[u] * scale).astype(BF16), k_dec[u].astype(BF16), ((1,), (1,)))
    for u in units:
        b = b_cum[u]
        b_last = b[C - 1:C, :]
        o_intra[u] = _dot(jnp.where(causal, att[u], 0.0).astype(BF16), v_ref[rows[u], vs[u[1]]])
        q_in[u] = (q_ref[rows[u], ks[u[1]]].astype(F32) * (jnp.exp(b) * scale)).astype(BF16)
        k_end[u] = (k_ref[rows[u], ks[u[1]]].astype(F32) * jnp.exp(b_last - b)).astype(BF16)
        decay[u] = jnp.concatenate([jnp.exp(tot[u])] * (GLA_DV // LANES), axis=1)
    for blk in range(n_blk):
        rs = slice(blk * C, (blk + 1) * C)
        os = []
        for h in heads:
            s_old = s_ref[h]
            os.append(o_intra[blk, h] + _dot(q_in[blk, h], s_old.astype(BF16)))
            s_ref[h] = decay[blk, h] * s_old + _dot_t(k_end[blk, h], v_ref[rs, vs[h]], ((0,), (0,)))
        ys = []
        for h, o in zip(heads, os):
            ms = jnp.mean(o * o, axis=-1, keepdims=True)
            ys.append((o * lax.rsqrt(ms + RMS_EPS) * nw_ref[...]).astype(o_ref.dtype))
        o_ref[rs, :] = jnp.concatenate(ys, axis=1)


def _gla(z, lr, wup, bgk, norm_w, B, S, to_cast):
    C = GLA_STEP_ROWS
    nc = S // C
    cast_specs = []
    for w in to_cast:
        rows = w.shape[0] // (B * nc)
        assert rows * B * nc == w.shape[0] and rows % BF16_ROWS == 0
        cast_specs.append(pl.BlockSpec((rows, w.shape[1]), lambda b, c: (b * nc + c, 0)))
    res = pl.pallas_call(
        functools.partial(_gla_kernel, n_casts=len(to_cast)),
        grid=(B, nc),
        in_specs=[
            pl.BlockSpec((C, GLA_QK_W), lambda b, c: (b * nc + c, 0)),
            pl.BlockSpec((C, GLA_QK_W), lambda b, c: (b * nc + c, 1)),
            pl.BlockSpec((C, GLA_V_W), lambda b, c: (b * nc + c, 1)),
            pl.BlockSpec((C, LANES), lambda b, c: (b * nc + c, 0)),
            pl.BlockSpec((LANES, GLA_QK_W), lambda b, c: (0, 0)),
            pl.BlockSpec((1, GLA_QK_W), lambda b, c: (0, 0)),
            pl.BlockSpec((1, GLA_DV), lambda b, c: (0, 0)),
        ] + cast_specs,
        out_specs=[pl.BlockSpec((C, GLA_V_W), lambda b, c: (b * nc + c, 0))] + cast_specs,
        out_shape=[jax.ShapeDtypeStruct((B * S, GLA_V_W), BF16)]
        + [jax.ShapeDtypeStruct(w.shape, BF16) for w in to_cast],
        scratch_shapes=[pltpu.VMEM((GLA_HEADS, GLA_DK, GLA_DV), F32)],
        compiler_params=_params(2),
        name="gla",
    )(z, z, z, lr, wup, bgk, norm_w, *to_cast)
    return res[0], res[1:]


def _dil_attn_kernel(q_ref, k_ref, v_ref, bucket_ref, table_ref, o_ref, lse_ref, bias_ref, *,
                     n_cls, n_blk, buckets, head0):
    scale = DIL_HEAD_DIM ** -0.5
    lane = lax.broadcasted_iota(jnp.int32, (DIL_BLOCK, LANES), 1)

    @pl.when((pl.program_id(0) == 0) & (pl.program_id(1) == 0))
    def _():
        bucket = bucket_ref[...]
        for h in range(DIL_SLOTS):
            bias = jnp.full(bucket.shape, NEG_INF, F32)
            for k in buckets:
                bias = jnp.where(bucket == k, table_ref[k, head0 + h] * LOG2E, bias)
            bias_ref[h] = bias

    def blocks(where, first):
        n_keys = DIL_BLOCK if first else 2 * DIL_BLOCK
        ones = jnp.ones((n_keys, DIL_HEAD_DIM), BF16)
        cols = [slice(h * DIL_HEAD_DIM, (h + 1) * DIL_HEAD_DIM) for h in range(DIL_SLOTS)]
        rows_q = [pl.ds(start, DIL_BLOCK) for _, start in where]
        rows_k = [rq if first else pl.ds(start - DIL_BLOCK, 2 * DIL_BLOCK) for rq, (_, start) in zip(rows_q, where)]
        units = [(u, h) for u in range(len(where)) for h in range(DIL_SLOTS)]
        scores = [
            _dot_t(q_ref[where[u][0], rows_q[u], cols[h]], k_ref[where[u][0], rows_k[u], cols[h]], ((1,), (1,)))
            * (scale * LOG2E) + (bias_ref[h, :, DIL_BLOCK:] if first else bias_ref[h])
            for u, h in units]
        maxes = [jnp.max(s, axis=-1, keepdims=True) for s in scores]
        probs = [jnp.exp2(s - m).astype(BF16) for s, m in zip(scores, maxes)]
        num_ls = [_dot(p, jnp.concatenate([v_ref[where[u][0], rows_k[u], cols[h]], ones], axis=1))
                  for p, (u, h) in zip(probs, units)]
        for u, (c, _) in enumerate(where):
            outs = []
            lse_tile = jnp.zeros((DIL_BLOCK, LANES), F32)
            for h in range(DIL_SLOTS):
                num_l, m = num_ls[u * DIL_SLOTS + h], maxes[u * DIL_SLOTS + h]
                l = num_l[:, DIL_HEAD_DIM:]
                outs.append((num_l[:, :DIL_HEAD_DIM] / l).astype(o_ref.dtype))
                lse_tile = jnp.where(lane == h, m * LN2 + jnp.log(l), lse_tile)
            o_ref[c, rows_q[u], :] = jnp.concatenate(outs, axis=1)
            lse_ref[c, rows_q[u], :] = lse_tile

    def row(n):
        return pl.multiple_of(n * DIL_BLOCK, DIL_BLOCK)

    if n_blk == 1:
        def pair_of_classes(i, carry):
            blocks([(2 * i, 0), (2 * i + 1, 0)], True)
            return carry
        lax.fori_loop(0, n_cls // 2, pair_of_classes, 0)
    else:
        def one_class(c, carry):
            blocks([(c, 0)], True)

            def pair_of_blocks(i, inner):
                blocks([(c, row(2 * i + 1)), (c, row(2 * i + 2))], False)
                return inner
            lax.fori_loop(0, (n_blk - 1) // 2, pair_of_blocks, 0)
            if (n_blk - 1) % 2:
                blocks([(c, (n_blk - 1) * DIL_BLOCK)], False)
            return carry

        if n_cls == 1:
            one_class(0, 0)
        else:
            lax.fori_loop(0, n_cls, one_class, 0)


def _band_buckets(dilation):
    a_idx = np.arange(DIL_BLOCK)[:, None]
    c_idx = np.arange(2 * DIL_BLOCK)[None, :]
    steps = DIL_BLOCK + a_idx - c_idx
    in_band = (steps >= 0) & (steps <= DIL_STEPS)
    bucket = _t5_bucket(np.clip(steps, 0, None) * dilation)
    return np.where(in_band, bucket, -1).astype(np.int32)


def _dil_attn(qkv, col_block, rel_bias, group):
    B, d, L, _ = qkv.shape
    n_cls = d
    bucket = _band_buckets(d)
    buckets = tuple(int(k) for k in np.unique(bucket) if k >= 0)

    def part(p):
        return pl.BlockSpec((None, n_cls, L, DIL_W), lambda b, r: (b, r, 0, col_block + p))

    return pl.pallas_call(
        functools.partial(_dil_attn_kernel, n_cls=n_cls, n_blk=L // DIL_BLOCK, buckets=buckets,
                          head0=group * DIL_SLOTS),
        grid=(B, d // n_cls),
        in_specs=[
            part(0), part(1), part(2),
            pl.BlockSpec((DIL_BLOCK, 2 * DIL_BLOCK), lambda b, r: (0, 0)),
            pl.BlockSpec(memory_space=pltpu.SMEM),
        ],
        scratch_shapes=[pltpu.VMEM((DIL_SLOTS, DIL_BLOCK, 2 * DIL_BLOCK), F32)],
        out_specs=[
            pl.BlockSpec((None, n_cls, L, DIL_W), lambda b, r: (b, r, 0, 0)),
            pl.BlockSpec((None, n_cls, L, LANES), lambda b, r: (b, r, 0, 0)),
        ],
        out_shape=[
            jax.ShapeDtypeStruct((B, d, L, DIL_W), BF16),
            jax.ShapeDtypeStruct((B, d, L, LANES), F32),
        ],
        compiler_params=_params(2),
        name=f"dil_attn_d{d}",
    )(qkv, qkv, qkv, jnp.asarray(bucket), rel_bias)


def _t5_bucket(dist):
    max_exact = REL_BUCKETS // 2
    d = np.maximum(dist, 1).astype(np.float64)
    large = max_exact + (np.log(d / max_exact) / np.log(REL_MAX_DIST / max_exact)
                         * (REL_BUCKETS - max_exact)).astype(np.int64)
    large = np.minimum(large, REL_BUCKETS - 1)
    return np.where(dist < max_exact, dist, large).astype(np.int32)


def _merge_kernel(og_ref, g_ref, o1_ref, o4_ref, o16_ref, l1_ref, l4_ref, l16_ref, ga_ref, gb_ref,
                  wa_ref, wb_ref, bg_ref, o_ref, s4_ref, s16_ref, sl4_ref, sl16_ref):
    T = og_ref.shape[0]
    g = g_ref[...].astype(F32)
    a = _dot((og_ref[...].astype(F32) * (g * jax.nn.sigmoid(g))).astype(BF16), wa_ref[...])
    for d, grp_ref, lse_ref, s_ref, sl_ref in ((4, o4_ref, l4_ref, s4_ref, sl4_ref),
                                                (16, o16_ref, l16_ref, s16_ref, sl16_ref)):
        for r in range(d):
            rows = pl.ds(r, T // d, stride=d)
            sl_ref[rows, :] = lse_ref[r]
            for c in range(DIL_SLOTS):
                s_ref[c, rows, :] = grp_ref[r, :, c * LANES:(c + 1) * LANES].astype(F32)
    lses = (l1_ref[...], sl4_ref[...], sl16_ref[...])
    m = jnp.maximum(jnp.maximum(lses[0], lses[1]), lses[2])
    es = [jnp.exp(l - m) for l in lses]
    inv = 1.0 / (es[0] + es[1] + es[2])
    ws = [e * inv for e in es]
    heads = []
    for c in range(DIL_SLOTS):
        cs = slice(c * DIL_HEAD_DIM, (c + 1) * DIL_HEAD_DIM)
        mix = (ws[0][:, c:c + 1] * o1_ref[:, cs].astype(F32)
               + ws[1][:, c:c + 1] * s4_ref[c]
               + ws[2][:, c:c + 1] * s16_ref[c])
        heads.append(mix.astype(BF16))
    bm = _dot(jnp.concatenate(heads, axis=1), wb_ref[...])
    gate_a = jax.nn.sigmoid(ga_ref[...].astype(F32) + bg_ref[:, :D_MODEL])
    gate_b = jax.nn.sigmoid(gb_ref[...].astype(F32) + bg_ref[:, D_MODEL:])
    o_ref[...] = (gate_a * a + gate_b * bm).astype(o_ref.dtype)


def _merge(o_gla, z_gla, g_col_block, groups, z, gate_col_block, B, S, wa, wb, b_gate, bm):
    nt = S // bm
    (o1, l1), (o4, l4), (o16, l16) = groups
    const = lambda i: (0, 0)
    resident = dict(pipeline_mode=pl.Buffered(1))

    def natural(width):
        return pl.BlockSpec((None, None, bm, width), lambda i: (i // nt, 0, i % nt, 0))

    def class_major(d, width):
        return pl.BlockSpec((None, d, bm // d, width), lambda i: (i // nt, 0, i % nt, 0))

    return pl.pallas_call(
        _merge_kernel,
        grid=(B * nt,),
        in_specs=[
            pl.BlockSpec((bm, GLA_V_W), lambda i: (i, 0)),
            pl.BlockSpec((bm, GLA_V_W), lambda i: (i, g_col_block)),
            natural(DIL_W), class_major(4, DIL_W), class_major(16, DIL_W),
            natural(LANES), class_major(4, LANES), class_major(16, LANES),
            pl.BlockSpec((bm, D_MODEL), lambda i: (i, gate_col_block)),
            pl.BlockSpec((bm, D_MODEL), lambda i: (i, gate_col_block + 1)),
            pl.BlockSpec((GLA_V_W, D_MODEL), const, **resident),
            pl.BlockSpec((DIL_W, D_MODEL), const, **resident),
            pl.BlockSpec((1, 2 * D_MODEL), const),
        ],
        out_specs=pl.BlockSpec((bm, D_MODEL), lambda i: (i, 0)),
        out_shape=jax.ShapeDtypeStruct((B * S, D_MODEL), BF16),
        scratch_shapes=[
            pltpu.VMEM((DIL_SLOTS, bm, LANES), F32), pltpu.VMEM((DIL_SLOTS, bm, LANES), F32),
            pltpu.VMEM((bm, LANES), F32), pltpu.VMEM((bm, LANES), F32),
        ],
        compiler_params=_params(1),
        name="merge_branches",
    )(o_gla, z_gla, o1, o4, o16, l1, l4, l16, z, z, wa, wb, b_gate)


def _out_proj_kernel(m_ref, x_ref, w_ref, o_ref):
    for c in range(0, w_ref.shape[1], PROJ_CHUNK):
        cs = slice(c, c + PROJ_CHUNK)
        o_ref[:, cs] = x_ref[:, cs] + _dot(m_ref[...], w_ref[:, cs])


def _out_proj(merged, x2d, w):
    M, D = x2d.shape
    rows = OUT_PROJ_ROWS
    return pl.pallas_call(
        _out_proj_kernel,
        grid=(M // rows,),
        in_specs=[
            pl.BlockSpec((rows, D), lambda i: (i, 0)),
            pl.BlockSpec((rows, D), lambda i: (i, 0)),
            pl.BlockSpec((D, D), lambda i: (0, 0), pipeline_mode=pl.Buffered(1)),
        ],
        out_specs=pl.BlockSpec((rows, D), lambda i: (i, 0)),
        out_shape=jax.ShapeDtypeStruct((M, D), F32),
        compiler_params=_params(1),
        name="out_proj",
    )(merged, x2d, w)


def _ffn_part_kernel(*refs, first, last):
    refs = list(refs)
    x_ref = refs.pop(0)
    h_in_ref = None if first else refs.pop(0)
    nw_ref = refs.pop(0) if first else None
    wgu_ref, wd_ref = refs.pop(0), refs.pop(0)
    fw_ref = refs.pop(0) if last else None
    o_ref = refs.pop(0)
    h_ref = refs.pop(0) if first else h_in_ref
    bf = FFN_HIDDEN_BLOCK

    x = x_ref[...]
    if first:
        ms = jnp.mean(x * x, axis=-1, keepdims=True)
        h_ref[...] = (x * lax.rsqrt(ms + RMS_EPS) * nw_ref[...]).astype(h_ref.dtype)
    o_ref[...] = x
    for f in range(wd_ref.shape[0] // bf):
        gate_up = _dot(h_ref[...], wgu_ref[:, 2 * f * bf:(2 * f + 2) * bf])
        gate, up = gate_up[:, :bf], gate_up[:, bf:]
        act = (gate * jax.nn.sigmoid(gate) * up).astype(BF16)
        for c in range(0, D_MODEL, FFN_OUT_CHUNK):
            o_ref[:, c:c + FFN_OUT_CHUNK] += _dot(act, wd_ref[f * bf:(f + 1) * bf, c:c + FFN_OUT_CHUNK])
    if last:
        y = o_ref[...]
        ms = jnp.mean(y * y, axis=-1, keepdims=True)
        o_ref[...] = y * lax.rsqrt(ms + RMS_EPS) * fw_ref[...]


def _ffn_part(x, h, norm_w, w_in, w_out, final_w, blocks, first, last):
    M = x.shape[0]
    rows, bf = FFN_PART_ROWS, FFN_HIDDEN_BLOCK
    n_hidden = (blocks[1] - blocks[0]) * bf
    row_spec = pl.BlockSpec((rows, D_MODEL), lambda i: (i, 0))
    vec_spec = pl.BlockSpec((1, D_MODEL), lambda i: (0, 0))
    resident = dict(pipeline_mode=pl.Buffered(1))
    wgu_spec = pl.BlockSpec((pl.Element(D_MODEL), pl.Element(2 * n_hidden)),
                            lambda i: (0, 2 * blocks[0] * bf), **resident)
    wd_spec = pl.BlockSpec((pl.Element(n_hidden), pl.Element(D_MODEL)), lambda i: (blocks[0] * bf, 0), **resident)
    args, in_specs = [x], [row_spec]
    if first:
        args.append(norm_w), in_specs.append(vec_spec)
    else:
        args.append(h), in_specs.append(row_spec)
    args += [w_in, w_out]
    in_specs += [wgu_spec, wd_spec]
    if last:
        args.append(final_w), in_specs.append(vec_spec)
    out_shape = [jax.ShapeDtypeStruct((M, D_MODEL), F32)]
    out_specs = [row_spec]
    if first:
        out_shape.append(jax.ShapeDtypeStruct((M, D_MODEL), BF16)), out_specs.append(row_spec)
    return pl.pallas_call(
        functools.partial(_ffn_part_kernel, first=first, last=last),
        grid=(M // rows,),
        in_specs=in_specs,
        out_specs=out_specs,
        out_shape=out_shape,
        compiler_params=_params(1),
        name=f"swiglu_{blocks[0]}_{blocks[1]}",
    )(*args)


def _ffn_weight_moves(bf):
    moves = []
    for f in range(D_FF // bf):
        moves.append((2 * f * bf, f * bf, bf))
        moves.append(((2 * f + 1) * bf, D_FF + f * bf, bf))
    return moves


def kernel(x, attn_norm, w_in, w_gk_up, b_gk, gla_norm, b_gate, w_branch_gla, w_branch_dil,
           w_out, ffn_norm, w_ffn_in, w_ffn_out, rel_bias, final_norm):
    B, S, D = x.shape
    assert D == D_MODEL and S % (16 * DIL_BLOCK) == 0 and attn_norm.shape[0] == 1

    o_lr = 2 * GLA_QK_W + 2 * GLA_V_W
    o_dil = o_lr + GLA_LOWRANK
    o_gate = o_dil + 9 * DIL_W
    n_gla = o_lr
    n_rest = 2 * D_MODEL + 3 * DIL_W
    n_dil = 3 * DIL_W
    lr_col = n_gla
    col_rest = -(-(n_gla + LANES) // n_rest) * n_rest
    col_d4 = -(-(col_rest + n_rest) // n_dil) * n_dil
    col_d16 = col_d4 + n_dil
    w_all = _pack_w_in(
        jnp.swapaxes(w_in, 1, 2)[0],
        segments=((0, 0, n_gla), (lr_col, o_lr, GLA_LOWRANK), (col_rest, o_gate, 2 * D_MODEL),
                  (col_rest + 2 * D_MODEL, o_dil, 3 * DIL_W), (col_d4, o_dil + 3 * DIL_W, 6 * DIL_W)),
        n_out=col_d16 + n_dil)

    z_gla, h, lr = _in_proj_norm(x, attn_norm[0][None, :], w_all, n_gla, lr_col // LANES)
    z_rest = _in_proj(h, w_all, col_rest, n_rest)
    z_d4, w_ffn_in_bf = _in_proj_dil(h, w_all, col_d4, n_dil, DIL_DILATIONS[1], w_ffn_in[0],
                                     _ffn_weight_moves(FFN_HIDDEN_BLOCK))
    z_d16, w_ffn_out_bf = _in_proj_dil(h, w_all, col_d16, n_dil, DIL_DILATIONS[2], w_ffn_out[0])

    wup = jnp.concatenate([w_gk_up[0], jnp.zeros((LANES - GLA_LOWRANK, GLA_QK_W), F32)], axis=0).astype(BF16)
    o_gla, merge_w = _gla(z_gla.reshape(B * S, n_gla), lr.reshape(B * S, LANES), wup, b_gk[0][None, :],
                          gla_norm[0][None, :], B, S, (w_branch_gla[0], w_branch_dil[0], w_out[0]))

    z_d1 = z_rest.reshape(B, 1, S, n_rest)
    groups = [_dil_attn(zg, col_block, rel_bias, gi)
              for gi, (zg, col_block) in enumerate(((z_d1, 2 * D_MODEL // DIL_W), (z_d4, 0), (z_d16, 0)))]
    merged = _merge(o_gla, z_gla.reshape(B * S, n_gla), (2 * GLA_QK_W + GLA_V_W) // GLA_V_W, groups,
                    z_rest.reshape(B * S, n_rest), 0, B, S, merge_w[0], merge_w[1], b_gate[0][None, :], MERGE_ROWS)
    x1 = _out_proj(merged, x.reshape(B * S, D), merge_w[2])
    acc, h_ffn = x1, None
    for i, blocks in enumerate(FFN_PARTS):
        first, last = i == 0, i == len(FFN_PARTS) - 1
        res = _ffn_part(acc, h_ffn, ffn_norm[0][None, :], w_ffn_in_bf, w_ffn_out_bf, final_norm[None, :],
                        blocks, first, last)
        acc, h_ffn = (res[0], res[1]) if first else (res[0], h_ffn)
    return acc.reshape(B, S, D)
```

```python
import functools

import numpy as np
import jax
import jax.numpy as jnp
from jax import lax
from jax.experimental import pallas as pl
from jax.experimental.pallas import tpu as pltpu

F32 = jnp.float32
BF16 = jnp.bfloat16

D_MODEL = 2048
GLA_HEADS = 4
GLA_DK = 256
GLA_DV = 512
GLA_QK_W = GLA_HEADS * GLA_DK
GLA_V_W = GLA_HEADS * GLA_DV
GLA_LOWRANK = 16
GLA_GATE_NORM = 16.0
DIL_DILATIONS = (1, 4, 16)
DIL_SLOTS = 8
DIL_HEAD_DIM = 128
DIL_W = DIL_SLOTS * DIL_HEAD_DIM
DIL_STEPS = 128
DIL_BLOCK = 128
REL_BUCKETS = 32
REL_MAX_DIST = 2048
D_FF = 5632
RMS_EPS = 1e-6
NEG_INF = -1e30
LOG2E = 1.4426950408889634
LN2 = 0.6931471805599453

LANES = 128
SUBLANES = 8
BF16_ROWS = 16
MXU_WIDTH = 256
VMEM_LIMIT = 56 * 1024 * 1024

PACK_COLS = 512
PROJ_NAT_ROWS = 512
PROJ_ROWS = 1024
PROJ_CHUNK = MXU_WIDTH
GLA_BLOCK = 128
GLA_STEP_ROWS = 1024
MERGE_ROWS = 512
OUT_PROJ_ROWS = 1024
FFN_PART_ROWS = 512
FFN_HIDDEN_BLOCK = 512
FFN_PARTS = ((0, 4), (4, 8), (8, 11))
FFN_OUT_CHUNK = 512


def _params(n_axes):
    return pltpu.CompilerParams(dimension_semantics=("arbitrary",) * n_axes, vmem_limit_bytes=VMEM_LIMIT)


def _dot(a, b):
    return jnp.dot(a, b, preferred_element_type=F32)


def _dot_t(a, b, dims):
    return lax.dot_general(a, b, (dims, ((), ())), preferred_element_type=F32)


def _split_bf16(x):
    hi = x.astype(BF16)
    lo = (x - hi.astype(F32)).astype(BF16)
    return hi, lo


def _in_proj_norm_kernel(x_ref, g_ref, w_ref, w_lr_ref, o_ref, h_ref, lr_ref):
    x = x_ref[...]
    ms = jnp.mean(x * x, axis=-1, keepdims=True)
    h_ref[...] = (x * lax.rsqrt(ms + RMS_EPS) * g_ref[...]).astype(h_ref.dtype)
    lr_ref[...] = _dot(h_ref[...], w_lr_ref[...]).astype(lr_ref.dtype)
    for c in range(0, w_ref.shape[1], PROJ_CHUNK):
        o_ref[:, c:c + PROJ_CHUNK] = _dot(h_ref[...], w_ref[:, c:c + PROJ_CHUNK]).astype(o_ref.dtype)


def _in_proj_norm(x, norm_w, w, n_cols, lr_col_block):
    B, S, D = x.shape
    rows = PROJ_NAT_ROWS
    resident = dict(pipeline_mode=pl.Buffered(1))
    return pl.pallas_call(
        _in_proj_norm_kernel,
        grid=(B, S // rows),
        in_specs=[
            pl.BlockSpec((None, rows, D), lambda b, t: (b, t, 0)),
            pl.BlockSpec((1, D), lambda b, t: (0, 0)),
            pl.BlockSpec((D, n_cols), lambda b, t: (0, 0), **resident),
            pl.BlockSpec((D, LANES), lambda b, t: (0, lr_col_block), **resident),
        ],
        out_specs=[
            pl.BlockSpec((None, rows, n_cols), lambda b, t: (b, t, 0)),
            pl.BlockSpec((None, rows, D), lambda b, t: (b, t, 0)),
            pl.BlockSpec((None, rows, LANES), lambda b, t: (b, t, 0)),
        ],
        out_shape=[jax.ShapeDtypeStruct((B, S, n_cols), BF16), jax.ShapeDtypeStruct((B, S, D), BF16),
                   jax.ShapeDtypeStruct((B, S, LANES), BF16)],
        compiler_params=_params(2),
        name="in_proj_norm",
    )(x, norm_w, w, w)


def _in_proj_kernel(h_ref, w_ref, o_ref):
    for c in range(0, w_ref.shape[1], PROJ_CHUNK):
        o_ref[:, c:c + PROJ_CHUNK] = _dot(h_ref[...], w_ref[:, c:c + PROJ_CHUNK]).astype(o_ref.dtype)


def _in_proj(h, w, col0, n_cols):
    B, S, D = h.shape
    rows = PROJ_NAT_ROWS
    assert col0 % n_cols == 0
    return pl.pallas_call(
        _in_proj_kernel,
        grid=(B, S // rows),
        in_specs=[
            pl.BlockSpec((None, rows, D), lambda b, t: (b, t, 0)),
            pl.BlockSpec((D, n_cols), lambda b, t: (0, col0 // n_cols), pipeline_mode=pl.Buffered(1)),
        ],
        out_specs=pl.BlockSpec((None, rows, n_cols), lambda b, t: (b, t, 0)),
        out_shape=jax.ShapeDtypeStruct((B, S, n_cols), BF16),
        compiler_params=_params(2),
        name="in_proj_rest",
    )(h, w)


def _in_proj_dil_kernel(h_ref, w_ref, cast_ref, o_ref, cast_out_ref, *scratch, dilation, cast_moves):
    rows = h_ref.shape[0]
    chunk_slabs = PROJ_CHUNK // LANES

    for dst, src, width in cast_moves:
        cast_out_ref[:, dst:dst + width] = cast_ref[:, src:src + width].astype(cast_out_ref.dtype)

    passes, rest = [], dilation
    while rest > 1:
        passes.append(min(rest, 4))
        rest //= passes[-1]

    def deinterleave(res, chunk):
        base = (chunk % 2) * chunk_slabs
        for i in range(chunk_slabs):
            scratch[0][base + i] = res[:, i * LANES:(i + 1) * LANES]
        src_ref, n_groups = scratch[0], 1
        for p, stride in enumerate(passes):
            group_rows = rows // n_groups
            last = p == len(passes) - 1
            for g in range(n_groups):
                for r in range(stride):
                    for i in range(chunk_slabs):
                        part = src_ref[base + i, pl.ds(g * group_rows + r, group_rows // stride, stride=stride), :]
                        if last:
                            col = (chunk * chunk_slabs + i) * LANES
                            o_ref[g + n_groups * r, :, col:col + LANES] = part.astype(o_ref.dtype)
                        else:
                            dst = (g + n_groups * r) * (group_rows // stride)
                            scratch[1][base + i, dst:dst + group_rows // stride, :] = part
            if not last:
                src_ref, n_groups = scratch[1], n_groups * stride

    n_chunks = w_ref.shape[1] // PROJ_CHUNK
    res = _dot(h_ref[...], w_ref[:, :PROJ_CHUNK])
    for i in range(n_chunks):
        nxt = _dot(h_ref[...], w_ref[:, (i + 1) * PROJ_CHUNK:(i + 2) * PROJ_CHUNK]) if i + 1 < n_chunks else None
        deinterleave(res, i)
        res = nxt


def _in_proj_dil(h, w, col0, n_cols, dilation, to_cast, cast_moves=None):
    B, S, D = h.shape
    rows = PROJ_ROWS
    nt = S // rows
    assert col0 % n_cols == 0 and n_cols % PROJ_CHUNK == 0
    cast_rows = to_cast.shape[0] // (B * nt)
    assert cast_rows * B * nt == to_cast.shape[0] and cast_rows % BF16_ROWS == 0
    if cast_moves is None:
        cast_moves = ((0, 0, to_cast.shape[1]),)
    assert sum(m[2] for m in cast_moves) == to_cast.shape[1]
    cast_spec = pl.BlockSpec((cast_rows, to_cast.shape[1]), lambda b, t: (b * nt + t, 0))
    n_slabs = 2 * PROJ_CHUNK // LANES
    scratch = [pltpu.VMEM((n_slabs, rows, LANES), F32)]
    if dilation > 4:
        scratch.append(pltpu.VMEM((n_slabs, rows, LANES), F32))
    return pl.pallas_call(
        functools.partial(_in_proj_dil_kernel, dilation=dilation, cast_moves=tuple(cast_moves)),
        grid=(B, nt),
        in_specs=[
            pl.BlockSpec((None, rows, D), lambda b, t: (b, t, 0)),
            pl.BlockSpec((D, n_cols), lambda b, t: (0, col0 // n_cols), pipeline_mode=pl.Buffered(1)),
            cast_spec,
        ],
        out_specs=[
            pl.BlockSpec((None, dilation, rows // dilation, n_cols), lambda b, t: (b, 0, t, 0)),
            cast_spec,
        ],
        out_shape=[jax.ShapeDtypeStruct((B, dilation, S // dilation, n_cols), BF16),
                   jax.ShapeDtypeStruct(to_cast.shape, BF16)],
        scratch_shapes=scratch,
        compiler_params=_params(2),
        name=f"in_proj_d{dilation}",
    )(h, w, to_cast)


def _pack_w_in_kernel(src_ref, valid_ref, wt_ref, o_ref):
    x = wt_ref[...]
    row = lax.broadcasted_iota(jnp.int32, x.shape, 0)
    x = jnp.where(row < valid_ref[pl.program_id(0)], x, 0.0)
    o_ref[...] = x.T.astype(o_ref.dtype)


def _pack_w_in(wt, segments, n_out):
    N, D = wt.shape
    n_blocks = n_out // PACK_COLS
    src = np.zeros((n_blocks,), np.int32)
    valid = np.zeros((n_blocks,), np.int32)
    for dst, s, width in segments:
        assert dst % PACK_COLS == 0
        for i in range(dst // PACK_COLS, -(-(dst + width) // PACK_COLS)):
            start = s + i * PACK_COLS - dst
            assert start % SUBLANES == 0 and start + PACK_COLS <= N
            src[i] = start // SUBLANES
            valid[i] = min(PACK_COLS, dst + width - i * PACK_COLS)
    return pl.pallas_call(
        _pack_w_in_kernel,
        grid_spec=pltpu.PrefetchScalarGridSpec(
            num_scalar_prefetch=2,
            grid=(n_blocks,),
            in_specs=[pl.BlockSpec((pl.Element(PACK_COLS), pl.Element(D)),
                                   lambda i, src, valid: (src[i] * SUBLANES, 0))],
            out_specs=pl.BlockSpec((D, PACK_COLS), lambda i, src, valid: (0, i)),
        ),
        out_shape=jax.ShapeDtypeStruct((D, n_out), BF16),
        compiler_params=_params(1),
        name="pack_w_in",
    )(jnp.asarray(src), jnp.asarray(valid), wt)


def _gla_kernel(q_ref, k_ref, v_ref, lr_ref, wup_ref, bgk_ref, nw_ref, *rest, n_casts):
    cast_refs, o_ref = rest[:n_casts], rest[n_casts]
    cast_out_refs, s_ref = rest[n_casts + 1:2 * n_casts + 1], rest[2 * n_casts + 1]
    C = GLA_BLOCK
    for src_ref, dst_ref in zip(cast_refs, cast_out_refs):
        dst_ref[...] = src_ref[...].astype(dst_ref.dtype)

    @pl.when(pl.program_id(1) == 0)
    def _():
        s_ref[...] = jnp.zeros_like(s_ref)

    row = lax.broadcasted_iota(jnp.int32, (C, C), 0)
    col = lax.broadcasted_iota(jnp.int32, (C, C), 1)
    causal = row >= col
    tri = jnp.where(causal, 1.0, 0.0).astype(BF16)
    ones = jnp.ones((C, LANES), BF16)
    scale = GLA_DK ** -0.5
    n_blk = q_ref.shape[0] // C
    heads = range(GLA_HEADS)
    ks = [slice(h * GLA_DK, (h + 1) * GLA_DK) for h in heads]
    vs = [slice(h * GLA_DV, (h + 1) * GLA_DV) for h in heads]
    units = [(blk, h) for blk in range(n_blk) for h in heads]
    rows = {u: slice(u[0] * C, (u[0] + 1) * C) for u in units}
    gks, b_cum, tot, att, q_dec, k_dec = {}, {}, {}, {}, {}, {}
    o_intra, q_in, k_end, decay = {}, {}, {}, {}
    for u in units:
        pre = _dot(lr_ref[rows[u], :], wup_ref[:, ks[u[1]]]) + bgk_ref[:, ks[u[1]]]
        log_sig = jnp.minimum(pre, 0.0) - jnp.log(1.0 + jnp.exp(-jnp.abs(pre)))
        gks[u] = _split_bf16(log_sig * (1.0 / GLA_GATE_NORM))
    for u in units:
        gk_hi, gk_lo = gks[u]
        b_cum[u] = _dot(tri, gk_hi) + _dot(tri, gk_lo)
        tot[u] = _dot_t(gk_hi, ones, ((0,), (0,))) + _dot_t(gk_lo, ones, ((0,), (0,)))
    for u in units:
        b = b_cum[u]
        b_mid = b[C // 2 - 1:C // 2, :]
        q_dec[u] = q_ref[rows[u], ks[u[1]]].astype(F32) * jnp.exp(b - b_mid)
        k_dec[u] = k_ref[rows[u], ks[u[1]]].astype(F32) * jnp.exp(b_mid - b)
        att[u] = _dot_t((q_dec[u] * scale).astype(BF16), k_dec[u].astype(BF16), ((1,), (1,)))
    for u in units:
        b = b_cum[u]
        b_last = b[C - 1:C, :]
        o_intra[u] = _dot(jnp.where(causal, att[u], 0.0).astype(BF16), v_ref[rows[u], vs[u[1]]])
        q_in[u] = (q_ref[rows[u], ks[u[1]]].astype(F32) * (jnp.exp(b) * scale)).astype(BF16)
        k_end[u] = (k_ref[rows[u], ks[u[1]]].astype(F32) * jnp.exp(b_last - b)).astype(BF16)
        decay[u] = jnp.concatenate([jnp.exp(tot[u])] * (GLA_DV // LANES), axis=1)
    for blk in range(n_blk):
        rs = slice(blk * C, (blk + 1) * C)
        os = []
        for h in heads:
            s_old = s_ref[h]
            os.append(o_intra[blk, h] + _dot(q_in[blk, h], s_old.astype(BF16)))
            s_ref[h] = decay[blk, h] * s_old + _dot_t(k_end[blk, h], v_ref[rs, vs[h]], ((0,), (0,)))
        ys = []
        for h, o in zip(heads, os):
            ms = jnp.mean(o * o, axis=-1, keepdims=True)
            ys.append((o * lax.rsqrt(ms + RMS_EPS) * nw_ref[...]).astype(o_ref.dtype))
        o_ref[rs, :] = jnp.concatenate(ys, axis=1)


def _gla(z, lr, wup, bgk, norm_w, B, S, to_cast):
    C = GLA_STEP_ROWS
    nc = S // C
    cast_specs = []
    for w in to_cast:
        rows = w.shape[0] // (B * nc)
        assert rows * B * nc == w.shape[0] and rows % BF16_ROWS == 0
        cast_specs.append(pl.BlockSpec((rows, w.shape[1]), lambda b, c: (b * nc + c, 0)))
    res = pl.pallas_call(
        functools.partial(_gla_kernel, n_casts=len(to_cast)),
        grid=(B, nc),
        in_specs=[
            pl.BlockSpec((C, GLA_QK_W), lambda b, c: (b * nc + c, 0)),
            pl.BlockSpec((C, GLA_QK_W), lambda b, c: (b * nc + c, 1)),
            pl.BlockSpec((C, GLA_V_W), lambda b, c: (b * nc + c, 1)),
            pl.BlockSpec((C, LANES), lambda b, c: (b * nc + c, 0)),
            pl.BlockSpec((LANES, GLA_QK_W), lambda b, c: (0, 0)),
            pl.BlockSpec((1, GLA_QK_W), lambda b, c: (0, 0)),
            pl.BlockSpec((1, GLA_DV), lambda b, c: (0, 0)),
        ] + cast_specs,
        out_specs=[pl.BlockSpec((C, GLA_V_W), lambda b, c: (b * nc + c, 0))] + cast_specs,
        out_shape=[jax.ShapeDtypeStruct((B * S, GLA_V_W), BF16)]
        + [jax.ShapeDtypeStruct(w.shape, BF16) for w in to_cast],
        scratch_shapes=[pltpu.VMEM((GLA_HEADS, GLA_DK, GLA_DV), F32)],
        compiler_params=_params(2),
        name="gla",
    )(z, z, z, lr, wup, bgk, norm_w, *to_cast)
    return res[0], res[1:]


def _dil_attn_kernel(q_ref, k_ref, v_ref, bucket_ref, table_ref, o_ref, lse_ref, bias_ref, *,
                     n_cls, n_blk, buckets, head0):
    scale = DIL_HEAD_DIM ** -0.5
    lane = lax.broadcasted_iota(jnp.int32, (DIL_BLOCK, LANES), 1)

    @pl.when((pl.program_id(0) == 0) & (pl.program_id(1) == 0))
    def _():
        bucket = bucket_ref[...]
        for h in range(DIL_SLOTS):
            bias = jnp.full(bucket.shape, NEG_INF, F32)
            for k in buckets:
                bias = jnp.where(bucket == k, table_ref[k, head0 + h] * LOG2E, bias)
            bias_ref[h] = bias

    def blocks(where, first):
        n_keys = DIL_BLOCK if first else 2 * DIL_BLOCK
        ones = jnp.ones((n_keys, DIL_HEAD_DIM), BF16)
        cols = [slice(h * DIL_HEAD_DIM, (h + 1) * DIL_HEAD_DIM) for h in range(DIL_SLOTS)]
        rows_q = [pl.ds(start, DIL_BLOCK) for _, start in where]
        rows_k = [rq if first else pl.ds(start - DIL_BLOCK, 2 * DIL_BLOCK) for rq, (_, start) in zip(rows_q, where)]
        units = [(u, h) for u in range(len(where)) for h in range(DIL_SLOTS)]
        scores = [
            _dot_t(q_ref[where[u][0], rows_q[u], cols[h]], k_ref[where[u][0], rows_k[u], cols[h]], ((1,), (1,)))
            * (scale * LOG2E) + (bias_ref[h, :, DIL_BLOCK:] if first else bias_ref[h])
            for u, h in units]
        maxes = [jnp.max(s, axis=-1, keepdims=True) for s in scores]
        probs = [jnp.exp2(s - m).astype(BF16) for s, m in zip(scores, maxes)]
        num_ls = [_dot(p, jnp.concatenate([v_ref[where[u][0], rows_k[u], cols[h]], ones], axis=1))
                  for p, (u, h) in zip(probs, units)]
        for u, (c, _) in enumerate(where):
            outs = []
            lse_tile = jnp.zeros((DIL_BLOCK, LANES), F32)
            for h in range(DIL_SLOTS):
                num_l, m = num_ls[u * DIL_SLOTS + h], maxes[u * DIL_SLOTS + h]
                l = num_l[:, DIL_HEAD_DIM:]
                outs.append((num_l[:, :DIL_HEAD_DIM] / l).astype(o_ref.dtype))
                lse_tile = jnp.where(lane == h, m * LN2 + jnp.log(l), lse_tile)
            o_ref[c, rows_q[u], :] = jnp.concatenate(outs, axis=1)
            lse_ref[c, rows_q[u], :] = lse_tile

    def row(n):
        return pl.multiple_of(n * DIL_BLOCK, DIL_BLOCK)

    if n_blk == 1:
        def group_of_classes(i, carry):
            blocks([(4 * i + j, 0) for j in range(4)], True)
            return carry
        lax.fori_loop(0, n_cls // 4, group_of_classes, 0)
    else:
        def one_class(c, carry):
            blocks([(c, 0)], True)

            def pair_of_blocks(i, inner):
                blocks([(c, row(2 * i + 1)), (c, row(2 * i + 2))], False)
                return inner
            lax.fori_loop(0, (n_blk - 1) // 2, pair_of_blocks, 0)
            if (n_blk - 1) % 2:
                blocks([(c, (n_blk - 1) * DIL_BLOCK)], False)
            return carry

        if n_cls == 1:
            one_class(0, 0)
        else:
            lax.fori_loop(0, n_cls, one_class, 0)


def _band_buckets(dilation):
    a_idx = np.arange(DIL_BLOCK)[:, None]
    c_idx = np.arange(2 * DIL_BLOCK)[None, :]
    steps = DIL_BLOCK + a_idx - c_idx
    in_band = (steps >= 0) & (steps <= DIL_STEPS)
    bucket = _t5_bucket(np.clip(steps, 0, None) * dilation)
    return np.where(in_band, bucket, -1).astype(np.int32)


def _dil_attn(qkv, col_block, rel_bias, group):
    B, d, L, _ = qkv.shape
    n_cls = d
    bucket = _band_buckets(d)
    buckets = tuple(int(k) for k in np.unique(bucket) if k >= 0)

    def part(p):
        return pl.BlockSpec((None, n_cls, L, DIL_W), lambda b, r: (b, r, 0, col_block + p))

    return pl.pallas_call(
        functools.partial(_dil_attn_kernel, n_cls=n_cls, n_blk=L // DIL_BLOCK, buckets=buckets,
                          head0=group * DIL_SLOTS),
        grid=(B, d // n_cls),
        in_specs=[
            part(0), part(1), part(2),
            pl.BlockSpec((DIL_BLOCK, 2 * DIL_BLOCK), lambda b, r: (0, 0)),
            pl.BlockSpec(memory_space=pltpu.SMEM),
        ],
        scratch_shapes=[pltpu.VMEM((DIL_SLOTS, DIL_BLOCK, 2 * DIL_BLOCK), F32)],
        out_specs=[
            pl.BlockSpec((None, n_cls, L, DIL_W), lambda b, r: (b, r, 0, 0)),
            pl.BlockSpec((None, n_cls, L, LANES), lambda b, r: (b, r, 0, 0)),
        ],
        out_shape=[
            jax.ShapeDtypeStruct((B, d, L, DIL_W), BF16),
            jax.ShapeDtypeStruct((B, d, L, LANES), F32),
        ],
        compiler_params=_params(2),
        name=f"dil_attn_d{d}",
    )(qkv, qkv, qkv, jnp.asarray(bucket), rel_bias)


def _t5_bucket(dist):
    max_exact = REL_BUCKETS // 2
    d = np.maximum(dist, 1).astype(np.float64)
    large = max_exact + (np.log(d / max_exact) / np.log(REL_MAX_DIST / max_exact)
                         * (REL_BUCKETS - max_exact)).astype(np.int64)
    large = np.minimum(large, REL_BUCKETS - 1)
    return np.where(dist < max_exact, dist, large).astype(np.int32)


def _merge_kernel(og_ref, g_ref, o1_ref, o4_ref, o16_ref, l1_ref, l4_ref, l16_ref, ga_ref, gb_ref,
                  wa_ref, wb_ref, bg_ref, o_ref, s4_ref, s16_ref, sl4_ref, sl16_ref):
    T = og_ref.shape[0]
    g = g_ref[...].astype(F32)
    a = _dot((og_ref[...].astype(F32) * (g * jax.nn.sigmoid(g))).astype(BF16), wa_ref[...])
    for d, grp_ref, lse_ref, s_ref, sl_ref in ((4, o4_ref, l4_ref, s4_ref, sl4_ref),
                                                (16, o16_ref, l16_ref, s16_ref, sl16_ref)):
        for r in range(d):
            rows = pl.ds(r, T // d, stride=d)
            sl_ref[rows, :] = lse_ref[r]
            for c in range(DIL_SLOTS):
                s_ref[c, rows, :] = grp_ref[r, :, c * LANES:(c + 1) * LANES].astype(F32)
    lses = (l1_ref[...], sl4_ref[...], sl16_ref[...])
    m = jnp.maximum(jnp.maximum(lses[0], lses[1]), lses[2])
    es = [jnp.exp(l - m) for l in lses]
    inv = 1.0 / (es[0] + es[1] + es[2])
    ws = [e * inv for e in es]
    heads = []
    for c in range(DIL_SLOTS):
        cs = slice(c * DIL_HEAD_DIM, (c + 1) * DIL_HEAD_DIM)
        mix = (ws[0][:, c:c + 1] * o1_ref[:, cs].astype(F32)
               + ws[1][:, c:c + 1] * s4_ref[c]
               + ws[2][:, c:c + 1] * s16_ref[c])
        heads.append(mix.astype(BF16))
    bm = _dot(jnp.concatenate(heads, axis=1), wb_ref[...])
    gate_a = jax.nn.sigmoid(ga_ref[...].astype(F32) + bg_ref[:, :D_MODEL])
    gate_b = jax.nn.sigmoid(gb_ref[...].astype(F32) + bg_ref[:, D_MODEL:])
    o_ref[...] = (gate_a * a + gate_b * bm).astype(o_ref.dtype)


def _merge(o_gla, z_gla, g_col_block, groups, z, gate_col_block, B, S, wa, wb, b_gate, bm):
    nt = S // bm
    (o1, l1), (o4, l4), (o16, l16) = groups
    const = lambda i: (0, 0)
    resident = dict(pipeline_mode=pl.Buffered(1))

    def natural(width):
        return pl.BlockSpec((None, None, bm, width), lambda i: (i // nt, 0, i % nt, 0))

    def class_major(d, width):
        return pl.BlockSpec((None, d, bm // d, width), lambda i: (i // nt, 0, i % nt, 0))

    return pl.pallas_call(
        _merge_kernel,
        grid=(B * nt,),
        in_specs=[
            pl.BlockSpec((bm, GLA_V_W), lambda i: (i, 0)),
            pl.BlockSpec((bm, GLA_V_W), lambda i: (i, g_col_block)),
            natural(DIL_W), class_major(4, DIL_W), class_major(16, DIL_W),
            natural(LANES), class_major(4, LANES), class_major(16, LANES),
            pl.BlockSpec((bm, D_MODEL), lambda i: (i, gate_col_block)),
            pl.BlockSpec((bm, D_MODEL), lambda i: (i, gate_col_block + 1)),
            pl.BlockSpec((GLA_V_W, D_MODEL), const, **resident),
            pl.BlockSpec((DIL_W, D_MODEL), const, **resident),
            pl.BlockSpec((1, 2 * D_MODEL), const),
        ],
        out_specs=pl.BlockSpec((bm, D_MODEL), lambda i: (i, 0)),
        out_shape=jax.ShapeDtypeStruct((B * S, D_MODEL), BF16),
        scratch_shapes=[
            pltpu.VMEM((DIL_SLOTS, bm, LANES), F32), pltpu.VMEM((DIL_SLOTS, bm, LANES), F32),
            pltpu.VMEM((bm, LANES), F32), pltpu.VMEM((bm, LANES), F32),
        ],
        compiler_params=_params(1),
        name="merge_branches",
    )(o_gla, z_gla, o1, o4, o16, l1, l4, l16, z, z, wa, wb, b_gate)


def _out_proj_kernel(m_ref, x_ref, w_ref, o_ref):
    for c in range(0, w_ref.shape[1], PROJ_CHUNK):
        cs = slice(c, c + PROJ_CHUNK)
        o_ref[:, cs] = x_ref[:, cs] + _dot(m_ref[...], w_ref[:, cs])


def _out_proj(merged, x2d, w):
    M, D = x2d.shape
    rows = OUT_PROJ_ROWS
    return pl.pallas_call(
        _out_proj_kernel,
        grid=(M // rows,),
        in_specs=[
            pl.BlockSpec((rows, D), lambda i: (i, 0)),
            pl.BlockSpec((rows, D), lambda i: (i, 0)),
            pl.BlockSpec((D, D), lambda i: (0, 0), pipeline_mode=pl.Buffered(1)),
        ],
        out_specs=pl.BlockSpec((rows, D), lambda i: (i, 0)),
        out_shape=jax.ShapeDtypeStruct((M, D), F32),
        compiler_params=_params(1),
        name="out_proj",
    )(merged, x2d, w)


def _ffn_part_kernel(*refs, first, last):
    refs = list(refs)
    x_ref = refs.pop(0)
    h_in_ref = None if first else refs.pop(0)
    nw_ref = refs.pop(0) if first else None
    wgu_ref, wd_ref = refs.pop(0), refs.pop(0)
    fw_ref = refs.pop(0) if last else None
    o_ref = refs.pop(0)
    h_ref = refs.pop(0) if first else h_in_ref
    bf = FFN_HIDDEN_BLOCK

    x = x_ref[...]
    if first:
        ms = jnp.mean(x * x, axis=-1, keepdims=True)
        h_ref[...] = (x * lax.rsqrt(ms + RMS_EPS) * nw_ref[...]).astype(h_ref.dtype)
    o_ref[...] = x
    for f in range(wd_ref.shape[0] // bf):
        gate_up = _dot(h_ref[...], wgu_ref[:, 2 * f * bf:(2 * f + 2) * bf])
        gate, up = gate_up[:, :bf], gate_up[:, bf:]
        act = (gate * jax.nn.sigmoid(gate) * up).astype(BF16)
        for c in range(0, D_MODEL, FFN_OUT_CHUNK):
            o_ref[:, c:c + FFN_OUT_CHUNK] += _dot(act, wd_ref[f * bf:(f + 1) * bf, c:c + FFN_OUT_CHUNK])
    if last:
        y = o_ref[...]
        ms = jnp.mean(y * y, axis=-1, keepdims=True)
        o_ref[...] = y * lax.rsqrt(ms + RMS_EPS) * fw_ref[...]


def _ffn_part(x, h, norm_w, w_in, w_out, final_w, blocks, first, last):
    M = x.shape[0]
    rows, bf = FFN_PART_ROWS, FFN_HIDDEN_BLOCK
    n_hidden = (blocks[1] - blocks[0]) * bf
    row_spec = pl.BlockSpec((rows, D_MODEL), lambda i: (i, 0))
    vec_spec = pl.BlockSpec((1, D_MODEL), lambda i: (0, 0))
    resident = dict(pipeline_mode=pl.Buffered(1))
    wgu_spec = pl.BlockSpec((pl.Element(D_MODEL), pl.Element(2 * n_hidden)),
                            lambda i: (0, 2 * blocks[0] * bf), **resident)
    wd_spec = pl.BlockSpec((pl.Element(n_hidden), pl.Element(D_MODEL)), lambda i: (blocks[0] * bf, 0), **resident)
    args, in_specs = [x], [row_spec]
    if first:
        args.append(norm_w), in_specs.append(vec_spec)
    else:
        args.append(h), in_specs.append(row_spec)
    args += [w_in, w_out]
    in_specs += [wgu_spec, wd_spec]
    if last:
        args.append(final_w), in_specs.append(vec_spec)
    out_shape = [jax.ShapeDtypeStruct((M, D_MODEL), F32)]
    out_specs = [row_spec]
    if first:
        out_shape.append(jax.ShapeDtypeStruct((M, D_MODEL), BF16)), out_specs.append(row_spec)
    return pl.pallas_call(
        functools.partial(_ffn_part_kernel, first=first, last=last),
        grid=(M // rows,),
        in_specs=in_specs,
        out_specs=out_specs,
        out_shape=out_shape,
        compiler_params=_params(1),
        name=f"swiglu_{blocks[0]}_{blocks[1]}",
    )(*args)


def _ffn_weight_moves(bf):
    moves = []
    for f in range(D_FF // bf):
        moves.append((2 * f * bf, f * bf, bf))
        moves.append(((2 * f + 1) * bf, D_FF + f * bf, bf))
    return moves


def kernel(x, attn_norm, w_in, w_gk_up, b_gk, gla_norm, b_gate, w_branch_gla, w_branch_dil,
           w_out, ffn_norm, w_ffn_in, w_ffn_out, rel_bias, final_norm):
    B, S, D = x.shape
    assert D == D_MODEL and S % (16 * DIL_BLOCK) == 0 and attn_norm.shape[0] == 1

    o_lr = 2 * GLA_QK_W + 2 * GLA_V_W
    o_dil = o_lr + GLA_LOWRANK
    o_gate = o_dil + 9 * DIL_W
    n_gla = o_lr
    n_rest = 2 * D_MODEL + 3 * DIL_W
    n_dil = 3 * DIL_W
    lr_col = n_gla
    col_rest = -(-(n_gla + LANES) // n_rest) * n_rest
    col_d4 = -(-(col_rest + n_rest) // n_dil) * n_dil
    col_d16 = col_d4 + n_dil
    w_all = _pack_w_in(
        jnp.swapaxes(w_in, 1, 2)[0],
        segments=((0, 0, n_gla), (lr_col, o_lr, GLA_LOWRANK), (col_rest, o_gate, 2 * D_MODEL),
                  (col_rest + 2 * D_MODEL, o_dil, 3 * DIL_W), (col_d4, o_dil + 3 * DIL_W, 6 * DIL_W)),
        n_out=col_d16 + n_dil)

    z_gla, h, lr = _in_proj_norm(x, attn_norm[0][None, :], w_all, n_gla, lr_col // LANES)
    z_rest = _in_proj(h, w_all, col_rest, n_rest)
    z_d4, w_ffn_in_bf = _in_proj_dil(h, w_all, col_d4, n_dil, DIL_DILATIONS[1], w_ffn_in[0],
                                     _ffn_weight_moves(FFN_HIDDEN_BLOCK))
    z_d16, w_ffn_out_bf = _in_proj_dil(h, w_all, col_d16, n_dil, DIL_DILATIONS[2], w_ffn_out[0])

    wup = jnp.concatenate([w_gk_up[0], jnp.zeros((LANES - GLA_LOWRANK, GLA_QK_W), F32)], axis=0).astype(BF16)
    o_gla, merge_w = _gla(z_gla.reshape(B * S, n_gla), lr.reshape(B * S, LANES), wup, b_gk[0][None, :],
                          gla_norm[0][None, :], B, S, (w_branch_gla[0], w_branch_dil[0], w_out[0]))

    z_d1 = z_rest.reshape(B, 1, S, n_rest)
    groups = [_dil_attn(zg, col_block, rel_bias, gi)
              for gi, (zg, col_block) in enumerate(((z_d1, 2 * D_MODEL // DIL_W), (z_d4, 0), (z_d16, 0)))]
    merged = _merge(o_gla, z_gla.reshape(B * S, n_gla), (2 * GLA_QK_W + GLA_V_W) // GLA_V_W, groups,
                    z_rest.reshape(B * S, n_rest), 0, B, S, merge_w[0], merge_w[1], b_gate[0][None, :], MERGE_ROWS)
    x1 = _out_proj(merged, x.reshape(B * S, D), merge_w[2])
    acc, h_ffn = x1, None
    for i, blocks in enumerate(FFN_PARTS):
        first, last = i == 0, i == len(FFN_PARTS) - 1
        res = _ffn_part(acc, h_ffn, ffn_norm[0][None, :], w_ffn_in_bf, w_ffn_out_bf, final_norm[None, :],
                        blocks, first, last)
        acc, h_ffn = (res[0], res[1]) if first else (res[0], h_ffn)
    return acc.reshape(B, S, D)
```
